```python
import math
import jax
import jax.numpy as jnp
from jax import lax
import numpy as np

D_MODEL = 2048
BATCH = 4
SEQ = 2048
DEPTH = 4

MEM_LEN = 256
Q_BLOCK = 128
HEAD_DIM = 128
DSA_HEADS = 4
DSA_TOPK_MAX = 256
IDX_HEADS = 16
IDX_DIM = 64
SB_HEADS = 4
NSA_HEADS = 4
CMP_LEN = 32
CMP_STRIDE = 16
CMP_HIDDEN = 256
SLC_LEN = 64
SLC_TOPK_MAX = 16
WINDOW = 512
N_NSA_BRANCH = 3
FORCE_SCORE = 1.0e4
MLA_HEADS = 4
Q_LORA = 512
KV_LORA = 512
QK_NOPE = 128
QK_ROPE = 64
V_DIM = 128
ROPE_BASE = 10000.0
N_MIXERS = 4
MIX_WIDTH = 512
REL_BUCKETS = 32
REL_MAX_DIST = 128
N_BIAS_HEADS = DSA_HEADS + NSA_HEADS
XA_HEADS = 4
D_FF = 5632
N_EXPERTS = 8
TOP_K_EXPERTS = 2
DN_ALPHA = (2 * DEPTH) ** 0.25
DN_BETA = (8 * DEPTH) ** -0.25
LN_EPS = 1e-5
RMS_EPS = 1e-6
NEG_BIG = -1.0e30

IN_SPLITS = (
    ('dsa_q', DSA_HEADS * HEAD_DIM), ('dsa_k', HEAD_DIM), ('dsa_v', HEAD_DIM),
    ('idx_q', IDX_HEADS * IDX_DIM), ('idx_k', IDX_DIM), ('idx_w', IDX_HEADS),
    ('sb_q', SB_HEADS * HEAD_DIM), ('sb_k', SB_HEADS * HEAD_DIM), ('sb_v', SB_HEADS * HEAD_DIM),
    ('nsa_q', NSA_HEADS * HEAD_DIM),
    ('nsa_kc', HEAD_DIM), ('nsa_vc', HEAD_DIM), ('nsa_ks', HEAD_DIM), ('nsa_vs', HEAD_DIM),
    ('nsa_kw', HEAD_DIM), ('nsa_vw', HEAD_DIM), ('nsa_g', NSA_HEADS * N_NSA_BRANCH),
    ('mla_cq', Q_LORA), ('mla_ckv', KV_LORA), ('mla_kr', QK_ROPE),
    ('gates', N_MIXERS * D_MODEL),
)
IN_WIDTH = sum(w for _, w in IN_SPLITS)

kernel_name = 'hybrid_dsa_sb_nsa_mla_deepnorm_trunk'


def split_columns(z):
    out = {}
    off = 0
    for name, width in IN_SPLITS:
        out[name] = z[..., off:off + width]
        off += width
    return out


def layer_norm(x, g, b):
    xf = x.astype(jnp.float32)
    mu = jnp.mean(xf, -1, keepdims=True)
    var = jnp.mean(jnp.square(xf - mu), -1, keepdims=True)
    y = (xf - mu) * lax.rsqrt(var + LN_EPS) * g.astype(jnp.float32) + b.astype(jnp.float32)
    return y.astype(x.dtype)


def rms_norm(x, g):
    xf = x.astype(jnp.float32)
    y = xf * lax.rsqrt(jnp.mean(jnp.square(xf), -1, keepdims=True) + RMS_EPS) * g.astype(jnp.float32)
    return y.astype(x.dtype)


def t5_bucket(dist):
    exact = REL_BUCKETS // 2
    d = jnp.maximum(dist, 0)
    log_ratio = jnp.log(jnp.maximum(d, 1).astype(jnp.float32) / exact) / math.log(REL_MAX_DIST / exact)
    far = jnp.minimum(exact + (log_ratio * (REL_BUCKETS - exact)).astype(jnp.int32), REL_BUCKETS - 1)
    return jnp.where(d < exact, d, far)


def rel_bias(table, dist):
    return table[t5_bucket(dist)].astype(jnp.float32)


def rope_tables(seq_len, dim):
    half = dim // 2
    inv = ROPE_BASE ** (-jnp.arange(half, dtype=jnp.float32) / half)
    ang = jnp.arange(seq_len, dtype=jnp.float32)[:, None] * inv[None, :]
    return jnp.cos(ang), jnp.sin(ang)


def rotary(x, cos, sin):
    half = x.shape[-1] // 2
    x1, x2 = x[..., :half], x[..., half:]
    return jnp.concatenate([x1 * cos - x2 * sin, x2 * cos + x1 * sin], axis=-1).astype(x.dtype)


def to_blocks(a):
    b, s = a.shape[0], a.shape[1]
    a = a.reshape((b, s // Q_BLOCK, Q_BLOCK) + a.shape[2:])
    return jnp.moveaxis(a, 1, 0)


def from_blocks(a):
    a = jnp.moveaxis(a, 0, 1)
    return a.reshape((a.shape[0], a.shape[1] * a.shape[2]) + a.shape[3:])


def block_starts(s):
    return jnp.arange(s // Q_BLOCK, dtype=jnp.int32) * Q_BLOCK


def batch_gather(a, idx):
    return jax.vmap(lambda t, i: t[i])(a, idx)


def dsa_attention(q, k, v, iq, ik, iw, table):
    b, s = q.shape[0], q.shape[1]
    topk = min(DSA_TOPK_MAX, s // 4)
    scale = HEAD_DIM ** -0.5
    spos = jnp.arange(s)

    def block(args):
        qb, iqb, iwb, t0 = args
        tq = t0 + jnp.arange(Q_BLOCK)
        score = jnp.einsum('bqhs,bqh->bqs', jax.nn.relu(jnp.einsum('bqhd,bsd->bqhs', iqb, ik)), iwb)
        score = jnp.where(spos[None, None, :] <= tq[None, :, None], score.astype(jnp.float32), -jnp.inf)
        _, idx = lax.top_k(score, topk)
        dist = tq[None, :, None] - idx
        kg = batch_gather(k, idx)
        vg = batch_gather(v, idx)
        logits = jnp.einsum('bqhd,bqkd->bhqk', qb, kg).astype(jnp.float32) * scale
        logits = logits + jnp.moveaxis(rel_bias(table, dist), -1, 1)
        logits = jnp.where((dist >= 0)[:, None], logits, -jnp.inf)
        p = jax.nn.softmax(logits, axis=-1).astype(vg.dtype)
        return jnp.einsum('bhqk,bqkd->bqhd', p, vg)

    out = lax.map(block, (to_blocks(q), to_blocks(iq), to_blocks(iw), block_starts(s)))
    return from_blocks(out)


def stick_breaking_attention(q, k, v):
    b, s, h, d = q.shape
    scale = d ** -0.5
    spos = jnp.arange(s)

    def block(args):
        qb, t0 = args
        tq = t0 + jnp.arange(Q_BLOCK)
        strict = spos[None, :] < tq[:, None]
        z = jnp.einsum('bqhd,bshd->bhqs', qb, k).astype(jnp.float32) * scale
        log_keep = jnp.where(strict, jax.nn.log_sigmoid(-z), 0.0)
        later = lax.cumsum(log_keep, axis=3, reverse=True) - log_keep
        a = jnp.where(strict, jnp.exp(jax.nn.log_sigmoid(z) + later), 0.0)
        return jnp.einsum('bhqs,bshd->bqhd', a.astype(v.dtype), v)

    out = lax.map(block, (to_blocks(q), block_starts(s)))
    return from_blocks(out)


def nsa_attention(q, kc, vc, ks, vs, kw, vw, gate_logits, pe_k, pe_v, ck_w1, ck_w2, cv_w1, cv_w2, table):
    b, s, h, d = q.shape
    scale = d ** -0.5
    tpos = jnp.arange(s)

    n_cmp = (s - CMP_LEN) // CMP_STRIDE + 1
    cmp_start = CMP_STRIDE * jnp.arange(n_cmp)
    cmp_idx = cmp_start[:, None] + jnp.arange(CMP_LEN)[None, :]

    def compress(x, pe, w1, w2):
        blk = (x[:, cmp_idx] + pe).reshape(b, n_cmp, CMP_LEN * d)
        return jax.nn.gelu(blk @ w1) @ w2

    k_cmp = compress(kc, pe_k, ck_w1, ck_w2)
    v_cmp = compress(vc, pe_v, cv_w1, cv_w2)
    cmp_ok = (cmp_start + CMP_LEN - 1)[None, :] <= tpos[:, None]
    logits = jnp.einsum('bthd,bnd->bhtn', q, k_cmp).astype(jnp.float32) * scale
    p_cmp = jax.nn.softmax(jnp.where(cmp_ok, logits, NEG_BIG), axis=-1) * cmp_ok
    o_cmp = jnp.einsum('bhtn,bnd->bthd', p_cmp.astype(v_cmp.dtype), v_cmp)

    n_slc = s // SLC_LEN
    n_sel = min(SLC_TOPK_MAX, n_slc)
    slc_start = SLC_LEN * jnp.arange(n_slc)
    cover = ((cmp_start[:, None] < slc_start[None, :] + SLC_LEN)
             & (cmp_start[:, None] + CMP_LEN > slc_start[None, :])).astype(jnp.float32)
    imp = jnp.einsum('bhtn,nj->btj', p_cmp, cover)
    cur = (tpos // SLC_LEN)[:, None]
    jb = jnp.arange(n_slc)[None, :]
    forced = (jb == 0) | (jb == cur) | (jb == cur - 1)
    imp = jnp.where(jb <= cur, imp + jnp.where(forced, FORCE_SCORE, 0.0), -jnp.inf)
    _, sel = lax.top_k(imp, n_sel)
    ks_blk = ks.reshape(b, n_slc, SLC_LEN, d)
    vs_blk = vs.reshape(b, n_slc, SLC_LEN, d)
    n_key = n_sel * SLC_LEN

    def sel_block(args):
        qb, selb, t0 = args
        tq = t0 + jnp.arange(Q_BLOCK)
        kpos = (selb[..., None] * SLC_LEN + jnp.arange(SLC_LEN)).reshape(b, Q_BLOCK, n_key)
        kg = batch_gather(ks_blk, selb).reshape(b, Q_BLOCK, n_key, d)
        vg = batch_gather(vs_blk, selb).reshape(b, Q_BLOCK, n_key, d)
        dist = tq[None, :, None] - kpos
        lg = jnp.einsum('bqhd,bqkd->bhqk', qb, kg).astype(jnp.float32) * scale
        lg = lg + jnp.moveaxis(rel_bias(table, dist), -1, 1)
        lg = jnp.where((dist >= 0)[:, None], lg, -jnp.inf)
        pr = jax.nn.softmax(lg, axis=-1).astype(vg.dtype)
        return jnp.einsum('bhqk,bqkd->bqhd', pr, vg)

    o_slc = from_blocks(lax.map(sel_block, (to_blocks(q), to_blocks(sel), block_starts(s))))

    nb = s // Q_BLOCK
    nwb = WINDOW // Q_BLOCK
    band_idx = jnp.arange(nb)[:, None] + jnp.arange(nwb + 1)[None, :]

    def band(x):
        xp = jnp.pad(x, ((0, 0), (WINDOW, 0), (0, 0))).reshape(b, nb + nwb, Q_BLOCK, d)
        return xp[:, band_idx].reshape(b, nb, (nwb + 1) * Q_BLOCK, d)

    kwb, vwb = band(kw), band(vw)
    qpos = tpos.reshape(nb, Q_BLOCK)
    kpos = (jnp.arange(nb)[:, None] - nwb) * Q_BLOCK + jnp.arange((nwb + 1) * Q_BLOCK)[None, :]
    dist = qpos[:, :, None] - kpos[:, None, :]
    ok = (dist >= 0) & (dist < WINDOW) & (kpos[:, None, :] >= 0)
    qw = q.reshape(b, nb, Q_BLOCK, h, d)
    lw = jnp.einsum('bnqhd,bnkd->bnhqk', qw, kwb).astype(jnp.float32) * scale
    lw = lw + jnp.moveaxis(rel_bias(table, dist), -1, 1)[None]
    lw = jnp.where(ok[None, :, None], lw, -jnp.inf)
    pw = jax.nn.softmax(lw, axis=-1).astype(vwb.dtype)
    o_win = jnp.einsum('bnhqk,bnkd->bnqhd', pw, vwb).reshape(b, s, h, d)

    g = jax.nn.sigmoid(gate_logits.astype(jnp.float32)).reshape(b, s, h, N_NSA_BRANCH).astype(q.dtype)
    return g[..., 0:1] * o_cmp + g[..., 1:2] * o_slc + g[..., 2:3] * o_win


def mla_attention(cq, ckv, kr, q_norm, kv_norm, w_uq, w_uk, w_uv):
    b, s = cq.shape[0], cq.shape[1]
    cos, sin = rope_tables(s, QK_ROPE)
    q = (rms_norm(cq, q_norm) @ w_uq).reshape(b, s, MLA_HEADS, QK_NOPE + QK_ROPE)
    q_nope = q[..., :QK_NOPE]
    q_pe = rotary(q[..., QK_NOPE:], cos[:, None, :], sin[:, None, :])
    c = rms_norm(ckv, kv_norm)
    k_nope = (c @ w_uk).reshape(b, s, MLA_HEADS, QK_NOPE)
    v = (c @ w_uv).reshape(b, s, MLA_HEADS, V_DIM)
    k_pe = rotary(kr, cos, sin)
    scale = (QK_NOPE + QK_ROPE) ** -0.5
    spos = jnp.arange(s)

    def block(args):
        qn, qp, t0 = args
        tq = t0 + jnp.arange(Q_BLOCK)
        logits = (jnp.einsum('bqhd,bshd->bhqs', qn, k_nope)
                  + jnp.einsum('bqhr,bsr->bhqs', qp, k_pe)).astype(jnp.float32) * scale
        logits = jnp.where(spos[None, :] <= tq[:, None], logits, -jnp.inf)
        p = jax.nn.softmax(logits, axis=-1).astype(v.dtype)
        return jnp.einsum('bhqs,bshd->bqhd', p, v)

    out = lax.map(block, (to_blocks(q_nope), to_blocks(q_pe), block_starts(s)))
    return from_blocks(out)


def hybrid_token_mixer(h, w_in, rel_table, mla_q_norm, mla_kv_norm, mla_w_uq, mla_w_uk, mla_w_uv,
                       nsa_pe_k, nsa_pe_v, nsa_ck_w1, nsa_ck_w2, nsa_cv_w1, nsa_cv_w2, w_branch, w_out):
    b, s, _ = h.shape
    p = split_columns(h @ w_in)
    o_dsa = dsa_attention(p['dsa_q'].reshape(b, s, DSA_HEADS, HEAD_DIM), p['dsa_k'], p['dsa_v'],
                          p['idx_q'].reshape(b, s, IDX_HEADS, IDX_DIM), p['idx_k'], p['idx_w'],
                          rel_table[:, :DSA_HEADS])
    o_sb = stick_breaking_attention(p['sb_q'].reshape(b, s, SB_HEADS, HEAD_DIM),
                                    p['sb_k'].reshape(b, s, SB_HEADS, HEAD_DIM),
                                    p['sb_v'].reshape(b, s, SB_HEADS, HEAD_DIM))
    o_nsa = nsa_attention(p['nsa_q'].reshape(b, s, NSA_HEADS, HEAD_DIM), p['nsa_kc'], p['nsa_vc'],
                          p['nsa_ks'], p['nsa_vs'], p['nsa_kw'], p['nsa_vw'], p['nsa_g'],
                          nsa_pe_k, nsa_pe_v, nsa_ck_w1, nsa_ck_w2, nsa_cv_w1, nsa_cv_w2,
                          rel_table[:, DSA_HEADS:])
    o_mla = mla_attention(p['mla_cq'], p['mla_ckv'], p['mla_kr'], mla_q_norm, mla_kv_norm,
                          mla_w_uq, mla_w_uk, mla_w_uv)
    branches = jnp.stack([o_dsa.reshape(b, s, MIX_WIDTH), o_sb.reshape(b, s, MIX_WIDTH),
                          o_nsa.reshape(b, s, MIX_WIDTH), o_mla.reshape(b, s, MIX_WIDTH)], axis=2)
    y = jnp.einsum('bsnc,ncd->bsnd', branches, w_branch)
    g = jax.nn.sigmoid(p['gates'].astype(jnp.float32)).reshape(b, s, N_MIXERS, D_MODEL).astype(y.dtype)
    return jnp.sum(g * y, axis=2) @ w_out


def memory_cross_attention(h, mem, wq, wk, wv, wo):
    b, s, _ = h.shape
    m = mem.shape[1]
    q = (h @ wq).reshape(b, s, XA_HEADS, HEAD_DIM)
    k = (mem @ wk).reshape(b, m, XA_HEADS, HEAD_DIM)
    v = (mem @ wv).reshape(b, m, XA_HEADS, HEAD_DIM)
    logits = jnp.einsum('bthd,bmhd->bhtm', q, k).astype(jnp.float32) * HEAD_DIM ** -0.5
    p = jax.nn.softmax(logits, axis=-1).astype(v.dtype)
    o = jnp.einsum('bhtm,bmhd->bthd', p, v).reshape(b, s, XA_HEADS * HEAD_DIM)
    return o @ wo


def swiglu(h, wg, wu, wd):
    return (jax.nn.silu(h @ wg) * (h @ wu)) @ wd


def moe_swiglu(h, router, wg, wu, wd):
    logits = (h @ router).astype(jnp.float32)
    top_val, top_idx = lax.top_k(logits, TOP_K_EXPERTS)
    weights = jax.nn.softmax(top_val, axis=-1)
    combine = jnp.sum(jax.nn.one_hot(top_idx, N_EXPERTS, dtype=jnp.float32) * weights[..., None], axis=-2)
    out = jnp.zeros_like(h)
    for e in range(N_EXPERTS):
        out = out + combine[..., e:e + 1].astype(h.dtype) * swiglu(h, wg[e], wu[e], wd[e])
    return out


def setup_inputs(seed: int = 0) -> dict:
    key = jax.random.key(seed)
    keys = iter(jax.random.split(key, 48))

    def nrm(shape, scale):
        return jax.random.normal(next(keys), shape, jnp.float32) * scale

    def gain(shape):
        return 1.0 + nrm(shape, 0.02)

    L = DEPTH
    ND = (DEPTH + 1) // 2
    NM = DEPTH // 2
    D = D_MODEL
    XW = XA_HEADS * HEAD_DIM
    return {
        'x': nrm((BATCH, SEQ, D), 1.0),
        'mem': nrm((BATCH, MEM_LEN, D), 1.0),
        'rel_table': nrm((REL_BUCKETS, N_BIAS_HEADS), 0.2),
        'w_in': nrm((L, D, IN_WIDTH), D ** -0.5),
        'mla_q_norm': gain((L, Q_LORA)),
        'mla_kv_norm': gain((L, KV_LORA)),
        'mla_w_uq': nrm((L, Q_LORA, MLA_HEADS * (QK_NOPE + QK_ROPE)), Q_LORA ** -0.5),
        'mla_w_uk': nrm((L, KV_LORA, MLA_HEADS * QK_NOPE), KV_LORA ** -0.5),
        'mla_w_uv': nrm((L, KV_LORA, MLA_HEADS * V_DIM), KV_LORA ** -0.5),
        'nsa_pe_k': nrm((L, CMP_LEN, HEAD_DIM), 0.1),
        'nsa_pe_v': nrm((L, CMP_LEN, HEAD_DIM), 0.1),
        'nsa_ck_w1': nrm((L, CMP_LEN * HEAD_DIM, CMP_HIDDEN), (CMP_LEN * HEAD_DIM) ** -0.5),
        'nsa_ck_w2': nrm((L, CMP_HIDDEN, HEAD_DIM), CMP_HIDDEN ** -0.5),
        'nsa_cv_w1': nrm((L, CMP_LEN * HEAD_DIM, CMP_HIDDEN), (CMP_LEN * HEAD_DIM) ** -0.5),
        'nsa_cv_w2': nrm((L, CMP_HIDDEN, HEAD_DIM), CMP_HIDDEN ** -0.5),
        'w_branch': nrm((L, N_MIXERS, MIX_WIDTH, D), MIX_WIDTH ** -0.5),
        'w_out': nrm((L, D, D), D ** -0.5 * DN_BETA),
        'ln1_g': gain((L, D)),
        'ln1_b': nrm((L, D), 0.02),
        'xa_wq': nrm((L, D, XW), D ** -0.5),
        'xa_wk': nrm((L, D, XW), D ** -0.5),
        'xa_wv': nrm((L, D, XW), D ** -0.5),
        'xa_wo': nrm((L, XW, D), XW ** -0.5 * DN_BETA),
        'ln2_g': gain((L, D)),
        'ln2_b': nrm((L, D), 0.02),
        'ffn_w_gate': nrm((ND, D, D_FF), D ** -0.5),
        'ffn_w_up': nrm((ND, D, D_FF), D ** -0.5),
        'ffn_w_down': nrm((ND, D_FF, D), D_FF ** -0.5 * DN_BETA),
        'moe_router': nrm((NM, D, N_EXPERTS), D ** -0.5),
        'moe_w_gate': nrm((NM, N_EXPERTS, D, D_FF), D ** -0.5),
        'moe_w_up': nrm((NM, N_EXPERTS, D, D_FF), D ** -0.5),
        'moe_w_down': nrm((NM, N_EXPERTS, D_FF, D), D_FF ** -0.5 * DN_BETA),
        'ln3_g': gain((L, D)),
        'ln3_b': nrm((L, D), 0.02),
    }


def reference(x, mem, rel_table, w_in, mla_q_norm, mla_kv_norm, mla_w_uq, mla_w_uk, mla_w_uv,
              nsa_pe_k, nsa_pe_v, nsa_ck_w1, nsa_ck_w2, nsa_cv_w1, nsa_cv_w2, w_branch, w_out,
              ln1_g, ln1_b, xa_wq, xa_wk, xa_wv, xa_wo, ln2_g, ln2_b,
              ffn_w_gate, ffn_w_up, ffn_w_down, moe_router, moe_w_gate, moe_w_up, moe_w_down,
              ln3_g, ln3_b):
    h = x
    for layer in range(DEPTH):
        mix = hybrid_token_mixer(h, w_in[layer], rel_table, mla_q_norm[layer], mla_kv_norm[layer],
                                 mla_w_uq[layer], mla_w_uk[layer], mla_w_uv[layer],
                                 nsa_pe_k[layer], nsa_pe_v[layer], nsa_ck_w1[layer], nsa_ck_w2[layer],
                                 nsa_cv_w1[layer], nsa_cv_w2[layer], w_branch[layer], w_out[layer])
        h = layer_norm(DN_ALPHA * h + mix, ln1_g[layer], ln1_b[layer])
        xa = memory_cross_attention(h, mem, xa_wq[layer], xa_wk[layer], xa_wv[layer], xa_wo[layer])
        h = layer_norm(DN_ALPHA * h + xa, ln2_g[layer], ln2_b[layer])
        if layer % 2 == 0:
            i = layer // 2
            f = swiglu(h, ffn_w_gate[i], ffn_w_up[i], ffn_w_down[i])
        else:
            i = layer // 2
            f = moe_swiglu(h, moe_router[i], moe_w_gate[i], moe_w_up[i], moe_w_down[i])
        h = layer_norm(DN_ALPHA * h + f, ln3_g[layer], ln3_b[layer])
    return h
```

```python
import functools
import math

import jax
import jax.numpy as jnp
import numpy as np
from jax import lax
from jax.experimental import pallas as pl
from jax.experimental.pallas import tpu as pltpu

F32 = jnp.float32
BF16 = jnp.bfloat16
I32 = jnp.int32

D_MODEL = 2048
BATCH = 4
SEQ = 2048
DEPTH = 4
TOKENS = BATCH * SEQ
MEM_LEN = 256
HEAD_DIM = 128
N_HEADS = 4
DSA_TOPK = min(256, SEQ // 4)
IDX_HEADS = 16
IDX_DIM = 64
CMP_LEN = 32
CMP_STRIDE = 16
CMP_HIDDEN = 256
N_CMP = (SEQ - CMP_LEN) // CMP_STRIDE + 1
SLC_LEN = 64
SLC_SHIFT = 6
N_SLC = SEQ // SLC_LEN
N_SEL = min(16, N_SLC)
WINDOW = 512
FORCE_SCORE = 1.0e4
Q_LORA = 512
KV_LORA = 512
QK_NOPE = 128
QK_ROPE = 64
ROPE_BASE = 10000.0
N_MIXERS = 4
MIX_WIDTH = 512
REL_BUCKETS = 32
REL_MAX_DIST = 128
D_FF = 5632
N_EXPERTS = 8
DN_ALPHA = (2 * DEPTH) ** 0.25
LN_EPS = 1e-5
RMS_EPS = 1e-6
NEG_BIG = -1.0e30

IN_SPLITS = (
    ('dsa_q', 512), ('dsa_k', 128), ('dsa_v', 128),
    ('idx_q', IDX_HEADS * IDX_DIM), ('idx_k', IDX_DIM), ('idx_w', IDX_HEADS),
    ('sb_q', 512), ('sb_k', 512), ('sb_v', 512),
    ('nsa_q', 512),
    ('nsa_kc', 128), ('nsa_vc', 128), ('nsa_ks', 128), ('nsa_vs', 128),
    ('nsa_kw', 128), ('nsa_vw', 128), ('nsa_g', 12),
    ('mla_cq', Q_LORA), ('mla_ckv', KV_LORA), ('mla_kr', QK_ROPE),
    ('gates', N_MIXERS * D_MODEL),
)

LANES = 128
Q_TILE = 128
K_CHUNK = 256
N_CHUNKS = SEQ // K_CHUNK
VMEM_LIMIT_BYTES = 56 * 1024 * 1024
INT_MIN = -2147483648
KEY_NEG_INF = -2139095041

_PACKED = (
    ('idx_q', 1024), ('dsa_q', 512), ('sb_q', 512), ('sb_k', 512), ('sb_v', 512), ('nsa_q', 512),
    ('mla_cq', 512), ('mla_ckv', 512),
    ('dsa_k', 128), ('dsa_v', 128), ('idx_k_lo', 128), ('idx_k_hi', 128), ('misc', 128),
    ('nsa_kc', 128), ('nsa_vc', 128), ('nsa_ks', 128), ('nsa_vs', 128), ('nsa_kw', 128), ('nsa_vw', 128),
    ('mla_kr', 128), ('gates', 8192),
)
_OFF = {}
_o = 0
for _n, _w in _PACKED:
    _OFF[_n] = _o
    _o += _w
Z_WIDTH = _o
MISC_IDXW = 0
MISC_NSAG = 16


def _cparams(n_axes, vmem=None):
    return pltpu.CompilerParams(dimension_semantics=("arbitrary",) * n_axes, vmem_limit_bytes=vmem)


def _dot(a, b):
    return jnp.dot(a, b, preferred_element_type=F32)


def _dot_nt(a, b):
    return lax.dot_general(a, b, (((1,), (1,)), ((), ())), preferred_element_type=F32)


def _layer_norm_rows(v, g, b):
    mu = jnp.mean(v, axis=-1, keepdims=True)
    d = v - mu
    var = jnp.mean(d * d, axis=-1, keepdims=True)
    return d * lax.rsqrt(var + LN_EPS) * g + b


def _matmul_kernel(x_ref, w_ref, o_ref):
    o_ref[...] = _dot(x_ref[...].astype(BF16), w_ref[...].astype(BF16)).astype(o_ref.dtype)


def _matmul(x, w, *, tm, tn, out_dtype, name):
    m, k = x.shape
    n = w.shape[1]
    return pl.pallas_call(
        _matmul_kernel,
        out_shape=jax.ShapeDtypeStruct((m, n), out_dtype),
        grid=(n // tn, m // tm),
        in_specs=[pl.BlockSpec((tm, k), lambda j, i: (i, 0)),
                  pl.BlockSpec((k, tn), lambda j, i: (0, j))],
        out_specs=pl.BlockSpec((tm, tn), lambda j, i: (i, j)),
        compiler_params=_cparams(2, VMEM_LIMIT_BYTES),
        name=name,
    )(x, w)


def _matmul_res_ln_kernel(x_ref, w_ref, h_ref, g_ref, b_ref, o_ref, ob_ref):
    y = _dot(x_ref[...], w_ref[...])
    out = _layer_norm_rows(DN_ALPHA * h_ref[...] + y, g_ref[...], b_ref[...])
    o_ref[...] = out
    ob_ref[...] = out.astype(BF16)


def _matmul_res_ln(x, w, h, g, b, *, tm, name):
    m, k = x.shape
    d = w.shape[1]
    return pl.pallas_call(
        _matmul_res_ln_kernel,
        out_shape=(jax.ShapeDtypeStruct((m, d), F32), jax.ShapeDtypeStruct((m, d), BF16)),
        grid=(m // tm,),
        in_specs=[pl.BlockSpec((tm, k), lambda i: (i, 0)),
                  pl.BlockSpec((k, d), lambda i: (0, 0)),
                  pl.BlockSpec((tm, d), lambda i: (i, 0)),
                  pl.BlockSpec((1, d), lambda i: (0, 0)),
                  pl.BlockSpec((1, d), lambda i: (0, 0))],
        out_specs=(pl.BlockSpec((tm, d), lambda i: (i, 0)), pl.BlockSpec((tm, d), lambda i: (i, 0))),
        compiler_params=_cparams(1, VMEM_LIMIT_BYTES),
        name=name,
    )(x, w, h, g, b)


def _res_ln_kernel(y_ref, h_ref, g_ref, b_ref, o_ref, ob_ref):
    out = _layer_norm_rows(DN_ALPHA * h_ref[...] + y_ref[...], g_ref[...], b_ref[...])
    o_ref[...] = out
    ob_ref[...] = out.astype(BF16)


def _res_ln(y, h, g, b, *, tm, name):
    m, d = h.shape
    return pl.pallas_call(
        _res_ln_kernel,
        out_shape=(jax.ShapeDtypeStruct((m, d), F32), jax.ShapeDtypeStruct((m, d), BF16)),
        grid=(m // tm,),
        in_specs=[pl.BlockSpec((tm, d), lambda i: (i, 0)),
                  pl.BlockSpec((tm, d), lambda i: (i, 0)),
                  pl.BlockSpec((1, d), lambda i: (0, 0)),
                  pl.BlockSpec((1, d), lambda i: (0, 0))],
        out_specs=(pl.BlockSpec((tm, d), lambda i: (i, 0)), pl.BlockSpec((tm, d), lambda i: (i, 0))),
        compiler_params=_cparams(1, VMEM_LIMIT_BYTES),
        name=name,
    )(y, h, g, b)


def _merge_kernel(b0, b1, b2, b3, g0, g1, g2, g3, wb_ref, o_ref):
    acc = None
    for n, (br, gr) in enumerate(((b0, g0), (b1, g1), (b2, g2), (b3, g3))):
        y = _dot(br[...], wb_ref[n])
        gy = jax.nn.sigmoid(gr[...].astype(F32)) * y
        acc = gy if acc is None else acc + gy
    o_ref[...] = acc.astype(o_ref.dtype)


def _merge(branches, z, wb, *, tm, tn):
    m = z.shape[0]
    gate_specs = []
    for n in range(N_MIXERS):
        base = (_OFF['gates'] + n * D_MODEL) // tn
        gate_specs.append(pl.BlockSpec((tm, tn), lambda j, i, base=base: (i, base + j)))
    return pl.pallas_call(
        _merge_kernel,
        out_shape=jax.ShapeDtypeStruct((m, D_MODEL), BF16),
        grid=(D_MODEL // tn, m // tm),
        in_specs=[pl.BlockSpec((tm, MIX_WIDTH), lambda j, i: (i, 0))] * N_MIXERS + gate_specs
        + [pl.BlockSpec((N_MIXERS, MIX_WIDTH, tn), lambda j, i: (0, 0, j))],
        out_specs=pl.BlockSpec((tm, tn), lambda j, i: (i, j)),
        compiler_params=_cparams(2, VMEM_LIMIT_BYTES),
        name="branch_merge",
    )(*branches, z, z, z, z, wb)


def _sortable_key(score):
    score = jnp.where(score == 0.0, 0.0, score)
    bits = pltpu.bitcast(score, I32)
    return bits ^ (jnp.right_shift(bits, 31) & 0x7FFFFFFF)


def _kth_largest_key(key_scr, k):
    def body(i, lo_u):
        cand_u = lo_u | jnp.left_shift(jnp.int32(1), 31 - i)
        cand_s = cand_u ^ INT_MIN
        cnt = jnp.zeros((Q_TILE, K_CHUNK), F32)
        for c in range(N_CHUNKS):
            cnt = cnt + jnp.where(key_scr[c] >= cand_s, 1.0, 0.0)
        tot = jnp.sum(cnt, axis=1, keepdims=True)
        return jnp.where(tot >= float(k), cand_u, lo_u)

    lo_u = lax.fori_loop(0, 32, body, jnp.zeros((Q_TILE, 1), I32))
    return lo_u ^ INT_MIN


def _slab_index(rel):
    return jnp.where(rel == 0, 2, jnp.where(rel == -128, 1, jnp.where(rel == -256, 0, 3)))


def _stack_heads(q):
    return jnp.concatenate([q[:, h * HEAD_DIM:(h + 1) * HEAD_DIM] for h in range(N_HEADS)], axis=0)


def _unstack_heads(o):
    return jnp.concatenate([o[h * Q_TILE:(h + 1) * Q_TILE] for h in range(N_HEADS)], axis=1)


def _mqa_masked_attention(qs, k_ref, v_ref, slab_ref, t0, c_lo, c_hi, mask_fn, m_scr, l_scr, acc_scr):
    scale = HEAD_DIM ** -0.5
    m_scr[...] = jnp.full(m_scr.shape, NEG_BIG, F32)
    l_scr[...] = jnp.zeros(l_scr.shape, F32)
    acc_scr[...] = jnp.zeros(acc_scr.shape, F32)

    def body(c, carry):
        start = pl.multiple_of(c * K_CHUNK, K_CHUNK)
        k = k_ref[pl.ds(start, K_CHUNK), :]
        v = v_ref[pl.ds(start, K_CHUNK), :]
        s = _dot_nt(qs, k) * scale + slab_ref[_slab_index(c * K_CHUNK - t0)]
        ok = mask_fn(c)
        s = jnp.where(ok, s, NEG_BIG)
        m_old = m_scr[...]
        m_new = jnp.maximum(m_old, jnp.max(s, axis=1, keepdims=True))
        p = jnp.where(ok, jnp.exp(s - m_new), 0.0)
        alpha = jnp.exp(m_old - m_new)
        l_scr[...] = alpha * l_scr[...] + jnp.sum(p, axis=1, keepdims=True)
        acc_scr[...] = alpha * acc_scr[...] + _dot(p.astype(BF16), v)
        m_scr[...] = m_new
        return carry

    lax.fori_loop(c_lo, c_hi, body, 0)
    return acc_scr[...] / l_scr[...]


def _row_pos4(t0):
    r = lax.broadcasted_iota(I32, (N_HEADS * Q_TILE, K_CHUNK), 0)
    return t0 + (r & (Q_TILE - 1))


def _col_pos4(c):
    return c * K_CHUNK + lax.broadcasted_iota(I32, (N_HEADS * Q_TILE, K_CHUNK), 1)


def _tile4(x):
    return jnp.concatenate([x] * N_HEADS, axis=0)


def _dsa_kernel(q_ref, iq_ref, misc_ref, k_ref, v_ref, iklo_ref, ikhi_ref, slab_ref, o_ref,
                key_scr, m_scr, l_scr, acc_scr):
    qb = pl.program_id(1)
    t0 = qb * Q_TILE
    n_chunks = qb // 2 + 1
    misc = misc_ref[...].astype(F32)
    row = t0 + lax.broadcasted_iota(I32, (Q_TILE, K_CHUNK), 0)

    key_scr[...] = jnp.full(key_scr.shape, KEY_NEG_INF, I32)

    def score_body(c, carry):
        start = pl.multiple_of(c * K_CHUNK, K_CHUNK)
        ik_lo = iklo_ref[pl.ds(start, K_CHUNK), :]
        ik_hi = ikhi_ref[pl.ds(start, K_CHUNK), :]
        acc = jnp.zeros((Q_TILE, K_CHUNK), F32)
        for p in range(IDX_HEADS // 2):
            pair = iq_ref[:, p * LANES:(p + 1) * LANES]
            w0 = misc[:, MISC_IDXW + 2 * p:MISC_IDXW + 2 * p + 1]
            w1 = misc[:, MISC_IDXW + 2 * p + 1:MISC_IDXW + 2 * p + 2]
            acc = acc + jnp.maximum(_dot_nt(pair, ik_lo), 0.0) * w0
            acc = acc + jnp.maximum(_dot_nt(pair, ik_hi), 0.0) * w1
        col = c * K_CHUNK + lax.broadcasted_iota(I32, (Q_TILE, K_CHUNK), 1)
        score = jnp.where(col <= row, acc, -jnp.inf)
        key_scr[c] = _sortable_key(score)
        return carry

    lax.fori_loop(0, n_chunks, score_body, 0)
    thr4 = _tile4(_kth_largest_key(key_scr, DSA_TOPK))
    row4 = _row_pos4(t0)

    def mask_fn(c):
        return (_tile4(key_scr[c]) >= thr4) & (_col_pos4(c) <= row4)

    qs = _stack_heads(q_ref[...])
    o = _mqa_masked_attention(qs, k_ref, v_ref, slab_ref, t0, 0, n_chunks, mask_fn, m_scr, l_scr, acc_scr)
    o_ref[...] = _unstack_heads(o).astype(o_ref.dtype)


def _zspec_rows(name, width, rows, index_fn):
    cb, rem = divmod(_OFF[name], width)
    assert rem == 0, name
    return pl.BlockSpec((rows, width), functools.partial(index_fn, cb))


def _dsa(z, slabs):
    nq = SEQ // Q_TILE

    def qidx(cb, b, i):
        return (b * nq + i, cb)

    def kidx(cb, b, i):
        return (b, cb)

    return pl.pallas_call(
        _dsa_kernel,
        out_shape=jax.ShapeDtypeStruct((TOKENS, MIX_WIDTH), BF16),
        grid=(BATCH, nq),
        in_specs=[_zspec_rows('dsa_q', 512, Q_TILE, qidx),
                  _zspec_rows('idx_q', 1024, Q_TILE, qidx),
                  _zspec_rows('misc', 128, Q_TILE, qidx),
                  _zspec_rows('dsa_k', 128, SEQ, kidx),
                  _zspec_rows('dsa_v', 128, SEQ, kidx),
                  _zspec_rows('idx_k_lo', 128, SEQ, kidx),
                  _zspec_rows('idx_k_hi', 128, SEQ, kidx),
                  pl.BlockSpec((4, N_HEADS * Q_TILE, K_CHUNK), lambda b, i: (0, 0, 0))],
        out_specs=pl.BlockSpec((Q_TILE, MIX_WIDTH), lambda b, i: (b * nq + i, 0)),
        scratch_shapes=[pltpu.VMEM((N_CHUNKS, Q_TILE, K_CHUNK), I32),
                        pltpu.VMEM((N_HEADS * Q_TILE, 1), F32),
                        pltpu.VMEM((N_HEADS * Q_TILE, 1), F32),
                        pltpu.VMEM((N_HEADS * Q_TILE, HEAD_DIM), F32)],
        compiler_params=_cparams(2, VMEM_LIMIT_BYTES),
        name="dsa_attention",
    )(z, z, z, z, z, z, z, slabs)


def _nsa_compress_kernel(xk_ref, xv_ref, pek_ref, pev_ref, kw1_ref, kw2_ref, vw1_ref, vw2_ref, ok_ref, ov_ref):
    half = CMP_STRIDE * HEAD_DIM

    def compress(x_ref, pe_ref, w1_ref, w2_ref):
        x = x_ref[...].astype(F32)
        a = _dot((x + pe_ref[:, :half]).astype(BF16), w1_ref[:half, :].astype(BF16))
        b = _dot((x + pe_ref[:, half:]).astype(BF16), w1_ref[half:, :].astype(BF16))
        hid = a + pltpu.roll(b, b.shape[0] - 1, 0)
        return _dot(jax.nn.gelu(hid).astype(BF16), w2_ref[...].astype(BF16))

    ok_ref[...] = compress(xk_ref, pek_ref, kw1_ref, kw2_ref)
    ov_ref[...] = compress(xv_ref, pev_ref, vw1_ref, vw2_ref)


def _nsa_compress(xk, xv, pek, pev, kw1, kw2, vw1, vw2):
    rows = xk.shape[0]
    full = lambda a: pl.BlockSpec(a.shape, lambda i: (0,) * a.ndim)
    args = (xk, xv, pek, pev, kw1, kw2, vw1, vw2)
    return pl.pallas_call(
        _nsa_compress_kernel,
        out_shape=(jax.ShapeDtypeStruct((rows, HEAD_DIM), F32), jax.ShapeDtypeStruct((rows, HEAD_DIM), F32)),
        grid=(1,),
        in_specs=[full(a) for a in args],
        out_specs=(pl.BlockSpec((rows, HEAD_DIM), lambda i: (0, 0)), pl.BlockSpec((rows, HEAD_DIM), lambda i: (0, 0))),
        compiler_params=_cparams(1, VMEM_LIMIT_BYTES),
        name="nsa_compress",
    )(*args)


def _nsa_kernel(q_ref, misc_ref, kc_ref, vc_ref, ks_ref, vs_ref, kw_ref, vw_ref, cov_ref, slab_ref, o_ref,
                key_scr, m_scr, l_scr, acc_scr):
    qb = pl.program_id(1)
    t0 = qb * Q_TILE
    scale = HEAD_DIM ** -0.5
    qs = _stack_heads(q_ref[...])
    misc = misc_ref[...].astype(F32)

    n_idx = lax.broadcasted_iota(I32, (N_HEADS * Q_TILE, LANES), 1)
    t_idx = t0 + (lax.broadcasted_iota(I32, (N_HEADS * Q_TILE, LANES), 0) & (Q_TILE - 1))
    cmp_ok = (CMP_STRIDE * n_idx + CMP_LEN - 1) <= t_idx
    lc = jnp.where(cmp_ok, _dot_nt(qs, kc_ref[...].astype(BF16)) * scale, NEG_BIG)
    e = jnp.exp(lc - jnp.max(lc, axis=1, keepdims=True))
    p_cmp = jnp.where(cmp_ok, e / jnp.sum(e, axis=1, keepdims=True), 0.0)
    o_cmp = _dot(p_cmp.astype(BF16), vc_ref[...].astype(BF16))

    p_sum = p_cmp[0:Q_TILE] + p_cmp[Q_TILE:2 * Q_TILE] + p_cmp[2 * Q_TILE:3 * Q_TILE] + p_cmp[3 * Q_TILE:]
    p_hi = p_sum.astype(BF16)
    p_lo = (p_sum - p_hi.astype(F32)).astype(BF16)
    cur = jnp.right_shift(t0 + lax.broadcasted_iota(I32, (Q_TILE, K_CHUNK), 0), SLC_SHIFT)
    for c in range(N_CHUNKS):
        cov = cov_ref[:, c * K_CHUNK:(c + 1) * K_CHUNK]
        imp = _dot(p_hi, cov) + _dot(p_lo, cov)
        jb = jnp.right_shift(c * K_CHUNK + lax.broadcasted_iota(I32, (Q_TILE, K_CHUNK), 1), SLC_SHIFT)
        forced = (jb == 0) | (jb == cur) | (jb == cur - 1)
        imp = jnp.where(jb <= cur, imp + jnp.where(forced, FORCE_SCORE, 0.0), -jnp.inf)
        key_scr[c] = _sortable_key(imp)
    thr4 = _tile4(_kth_largest_key(key_scr, N_SEL * SLC_LEN))
    row4 = _row_pos4(t0)

    def sel_mask(c):
        return (_tile4(key_scr[c]) >= thr4) & (_col_pos4(c) <= row4)

    o_slc = _mqa_masked_attention(qs, ks_ref, vs_ref, slab_ref, t0, 0, qb // 2 + 1, sel_mask,
                                  m_scr, l_scr, acc_scr)

    def win_mask(c):
        dist = row4 - _col_pos4(c)
        return (dist >= 0) & (dist < WINDOW)

    c_lo = jnp.maximum(qb - WINDOW // Q_TILE, 0) // 2
    o_win = _mqa_masked_attention(qs, kw_ref, vw_ref, slab_ref, t0, c_lo, qb // 2 + 1, win_mask,
                                  m_scr, l_scr, acc_scr)

    outs = []
    for h in range(N_HEADS):
        g = jax.nn.sigmoid(misc[:, MISC_NSAG + 3 * h:MISC_NSAG + 3 * h + 3])
        rows = slice(h * Q_TILE, (h + 1) * Q_TILE)
        outs.append(g[:, 0:1] * o_cmp[rows] + g[:, 1:2] * o_slc[rows] + g[:, 2:3] * o_win[rows])
    o_ref[...] = jnp.concatenate(outs, axis=1).astype(o_ref.dtype)


def _nsa(z, k_cmp, v_cmp, cov, slabs):
    nq = SEQ // Q_TILE

    def qidx(cb, b, i):
        return (b * nq + i, cb)

    def kidx(cb, b, i):
        return (b, cb)

    cmp_spec = pl.BlockSpec((LANES, HEAD_DIM), lambda b, i: (b, 0))
    return pl.pallas_call(
        _nsa_kernel,
        out_shape=jax.ShapeDtypeStruct((TOKENS, MIX_WIDTH), BF16),
        grid=(BATCH, nq),
        in_specs=[_zspec_rows('nsa_q', 512, Q_TILE, qidx),
                  _zspec_rows('misc', 128, Q_TILE, qidx),
                  cmp_spec, cmp_spec,
                  _zspec_rows('nsa_ks', 128, SEQ, kidx),
                  _zspec_rows('nsa_vs', 128, SEQ, kidx),
                  _zspec_rows('nsa_kw', 128, SEQ, kidx),
                  _zspec_rows('nsa_vw', 128, SEQ, kidx),
                  pl.BlockSpec((LANES, SEQ), lambda b, i: (0, 0)),
                  pl.BlockSpec((4, N_HEADS * Q_TILE, K_CHUNK), lambda b, i: (0, 0, 0))],
        out_specs=pl.BlockSpec((Q_TILE, MIX_WIDTH), lambda b, i: (b * nq + i, 0)),
        scratch_shapes=[pltpu.VMEM((N_CHUNKS, Q_TILE, K_CHUNK), I32),
                        pltpu.VMEM((N_HEADS * Q_TILE, 1), F32),
                        pltpu.VMEM((N_HEADS * Q_TILE, 1), F32),
                        pltpu.VMEM((N_HEADS * Q_TILE, HEAD_DIM), F32)],
        compiler_params=_cparams(2, VMEM_LIMIT_BYTES),
        name="nsa_attention",
    )(z, z, k_cmp, v_cmp, z, z, z, z, cov, slabs)


SB_TILE = 256


def _sb_kernel(q_ref, k_ref, v_ref, u_ref, o_ref):
    qb = pl.program_id(2)
    scale = HEAD_DIM ** -0.5
    q = q_ref[...]
    upper = u_ref[...]
    row = qb * SB_TILE + lax.broadcasted_iota(I32, (SB_TILE, SB_TILE), 0)
    lane = lax.broadcasted_iota(I32, (SB_TILE, SB_TILE), 1)

    def body(i, carry):
        later, acc = carry
        c = qb - i
        start = pl.multiple_of(c * SB_TILE, SB_TILE)
        k = k_ref[pl.ds(start, SB_TILE), :]
        v = v_ref[pl.ds(start, SB_TILE), :]
        zl = _dot_nt(q, k) * scale
        strict = (c * SB_TILE + lane) < row
        log_beta = jnp.minimum(zl, 0.0) - jnp.log1p(jnp.exp(-jnp.abs(zl)))
        log_keep = jnp.where(strict, log_beta - zl, 0.0)
        keep_hi = log_keep.astype(BF16)
        keep_lo = (log_keep - keep_hi.astype(F32)).astype(BF16)
        within = _dot(keep_hi, upper) + _dot(keep_lo, upper)
        a = jnp.where(strict, jnp.exp(log_beta + within + later), 0.0)
        acc = acc + _dot(a.astype(BF16), v)
        later = later + jnp.sum(log_keep, axis=1, keepdims=True)
        return later, acc

    _, acc = lax.fori_loop(0, qb + 1, body,
                           (jnp.zeros((SB_TILE, 1), F32), jnp.zeros((SB_TILE, HEAD_DIM), F32)))
    o_ref[...] = acc.astype(o_ref.dtype)


def _stick_breaking(z, upper):
    nq = SEQ // SB_TILE
    qcb, kcb, vcb = _OFF['sb_q'] // HEAD_DIM, _OFF['sb_k'] // HEAD_DIM, _OFF['sb_v'] // HEAD_DIM
    return pl.pallas_call(
        _sb_kernel,
        out_shape=jax.ShapeDtypeStruct((TOKENS, MIX_WIDTH), BF16),
        grid=(BATCH, N_HEADS, nq),
        in_specs=[pl.BlockSpec((SB_TILE, HEAD_DIM), lambda b, h, i: (b * nq + i, qcb + h)),
                  pl.BlockSpec((SEQ, HEAD_DIM), lambda b, h, i: (b, kcb + h)),
                  pl.BlockSpec((SEQ, HEAD_DIM), lambda b, h, i: (b, vcb + h)),
                  pl.BlockSpec((SB_TILE, SB_TILE), lambda b, h, i: (0, 0))],
        out_specs=pl.BlockSpec((SB_TILE, HEAD_DIM), lambda b, h, i: (b * nq + i, h)),
        compiler_params=_cparams(3, VMEM_LIMIT_BYTES),
        name="stick_breaking_attention",
    )(z, z, z, upper)


MLA_QK = 2 * HEAD_DIM


def _rope128(x, cos, nsin, psin):
    return x * cos + pltpu.roll(x, 96, 1) * nsin + pltpu.roll(x, 32, 1) * psin


def _mla_prep_kernel(cq_ref, ckv_ref, kr_ref, qn_ref, kvn_ref, wqn_ref, wqp_ref, wuk_ref, wuv_ref,
                     cos_ref, nsin_ref, psin_ref, q_ref, k_ref, v_ref):
    def rms(x, g):
        return (x * lax.rsqrt(jnp.mean(x * x, axis=-1, keepdims=True) + RMS_EPS) * g).astype(BF16)

    cos, nsin, psin = cos_ref[...], nsin_ref[...], psin_ref[...]
    xq = rms(cq_ref[...].astype(F32), qn_ref[...])
    xc = rms(ckv_ref[...].astype(F32), kvn_ref[...])
    q_nope = _dot(xq, wqn_ref[...])
    q_rope = _dot(xq, wqp_ref[...])
    k_nope = _dot(xc, wuk_ref[...])
    k_rope = _rope128(kr_ref[...].astype(F32), cos, nsin, psin)
    q_parts, k_parts = [], []
    for h in range(N_HEADS):
        cols = slice(h * HEAD_DIM, (h + 1) * HEAD_DIM)
        q_parts += [q_nope[:, cols], _rope128(q_rope[:, cols], cos, nsin, psin)]
        k_parts += [k_nope[:, cols], k_rope]
    q_ref[...] = jnp.concatenate(q_parts, axis=1).astype(BF16)
    k_ref[...] = jnp.concatenate(k_parts, axis=1).astype(BF16)
    v_ref[...] = _dot(xc, wuv_ref[...]).astype(BF16)


def _mla_prep(z, qn, kvn, wqn, wqp, wuk, wuv, cos, nsin, psin, *, tm):
    nt = SEQ // tm

    def zidx(cb, i):
        return (i, cb)

    wspec = pl.BlockSpec((Q_LORA, 512), lambda i: (0, 0))
    nspec = pl.BlockSpec((1, 512), lambda i: (0, 0))
    tspec = pl.BlockSpec((tm, LANES), lambda i: (i % nt, 0))
    return pl.pallas_call(
        _mla_prep_kernel,
        out_shape=(jax.ShapeDtypeStruct((TOKENS, N_HEADS * MLA_QK), BF16),
                   jax.ShapeDtypeStruct((TOKENS, N_HEADS * MLA_QK), BF16),
                   jax.ShapeDtypeStruct((TOKENS, N_HEADS * HEAD_DIM), BF16)),
        grid=(TOKENS // tm,),
        in_specs=[_zspec_rows('mla_cq', 512, tm, zidx), _zspec_rows('mla_ckv', 512, tm, zidx),
                  _zspec_rows('mla_kr', 128, tm, zidx), nspec, nspec, wspec, wspec, wspec, wspec,
                  tspec, tspec, tspec],
        out_specs=(pl.BlockSpec((tm, N_HEADS * MLA_QK), lambda i: (i, 0)),
                   pl.BlockSpec((tm, N_HEADS * MLA_QK), lambda i: (i, 0)),
                   pl.BlockSpec((tm, N_HEADS * HEAD_DIM), lambda i: (i, 0))),
        compiler_params=_cparams(1, VMEM_LIMIT_BYTES),
        name="mla_prep",
    )(z, z, z, qn, kvn, wqn, wqp, wuk, wuv, cos, nsin, psin)


FLASH_TILE = 256


def _flash_kernel(q_ref, k_ref, v_ref, o_ref, *, scale, causal, n_kv):
    qb = pl.program_id(2)
    q = q_ref[...]
    row = qb * FLASH_TILE + lax.broadcasted_iota(I32, (FLASH_TILE, FLASH_TILE), 0)
    lane = lax.broadcasted_iota(I32, (FLASH_TILE, FLASH_TILE), 1)

    def body(c, carry):
        m, l, acc = carry
        start = pl.multiple_of(c * FLASH_TILE, FLASH_TILE)
        k = k_ref[pl.ds(start, FLASH_TILE), :]
        v = v_ref[pl.ds(start, FLASH_TILE), :]
        s = _dot_nt(q, k) * scale
        if causal:
            s = jnp.where(c * FLASH_TILE + lane <= row, s, NEG_BIG)
        m_new = jnp.maximum(m, jnp.max(s, axis=1, keepdims=True))
        p = jnp.exp(s - m_new)
        alpha = jnp.exp(m - m_new)
        return m_new, alpha * l + jnp.sum(p, axis=1, keepdims=True), alpha * acc + _dot(p.astype(BF16), v)

    init = (jnp.full((FLASH_TILE, 1), NEG_BIG, F32), jnp.zeros((FLASH_TILE, 1), F32),
            jnp.zeros((FLASH_TILE, HEAD_DIM), F32))
    _, l, acc = lax.fori_loop(0, qb + 1 if causal else n_kv, body, init)
    o_ref[...] = (acc / l).astype(o_ref.dtype)


def _flash(q, k, v, *, dk, kv_len, scale, causal, name):
    nq = SEQ // FLASH_TILE
    return pl.pallas_call(
        functools.partial(_flash_kernel, scale=scale, causal=causal, n_kv=kv_len // FLASH_TILE),
        out_shape=jax.ShapeDtypeStruct((TOKENS, N_HEADS * HEAD_DIM), BF16),
        grid=(BATCH, N_HEADS, nq),
        in_specs=[pl.BlockSpec((FLASH_TILE, dk), lambda b, h, i: (b * nq + i, h)),
                  pl.BlockSpec((kv_len, dk), lambda b, h, i: (b, h)),
                  pl.BlockSpec((kv_len, HEAD_DIM), lambda b, h, i: (b, h))],
        out_specs=pl.BlockSpec((FLASH_TILE, HEAD_DIM), lambda b, h, i: (b * nq + i, h)),
        compiler_params=_cparams(3, VMEM_LIMIT_BYTES),
        name=name,
    )(q, k, v)


FFN_TM = 1024
FFN_SUB = 256


def _ffn_kernel(te_ref, nv_ref, x_ref, wg_ref, wu_ref, wd_ref, o_ref, wg_b, wu_b, wd_b):
    i = pl.program_id(0)
    j = pl.program_id(1)
    nv = nv_ref[i]

    @pl.when(nv > 0)
    def _():
        wg_b[...] = wg_ref[0].astype(BF16)
        wu_b[...] = wu_ref[0].astype(BF16)
        wd_b[...] = wd_ref[0].astype(BF16)

    for s in range(FFN_TM // FFN_SUB):
        rows = slice(s * FFN_SUB, (s + 1) * FFN_SUB)

        @pl.when(s * FFN_SUB < nv)
        def _():
            x = x_ref[rows, :].astype(BF16)
            g = _dot(x, wg_b[...])
            u = _dot(x, wu_b[...])
            y = _dot((g * jax.nn.sigmoid(g) * u).astype(BF16), wd_b[...])

            @pl.when(j == 0)
            def _():
                o_ref[rows, :] = y

            @pl.when(j > 0)
            def _():
                o_ref[rows, :] += y

        @pl.when((s * FFN_SUB >= nv) & (j == 0))
        def _():
            o_ref[rows, :] = jnp.zeros((FFN_SUB, D_MODEL), F32)


def _grouped_swiglu(tile_expert, tile_valid, x, wg, wu, wd, *, tf, name):
    r = x.shape[0]
    nff = D_FF // tf

    def jj(i, j, nv):
        return jnp.where(nv[i] > 0, j, nff - 1)

    grid_spec = pltpu.PrefetchScalarGridSpec(
        num_scalar_prefetch=2,
        grid=(r // FFN_TM, nff),
        in_specs=[pl.BlockSpec((FFN_TM, D_MODEL), lambda i, j, te, nv: (i, 0)),
                  pl.BlockSpec((1, D_MODEL, tf), lambda i, j, te, nv: (te[i], 0, jj(i, j, nv))),
                  pl.BlockSpec((1, D_MODEL, tf), lambda i, j, te, nv: (te[i], 0, jj(i, j, nv))),
                  pl.BlockSpec((1, tf, D_MODEL), lambda i, j, te, nv: (te[i], jj(i, j, nv), 0))],
        out_specs=pl.BlockSpec((FFN_TM, D_MODEL), lambda i, j, te, nv: (i, 0)),
        scratch_shapes=[pltpu.VMEM((D_MODEL, tf), BF16), pltpu.VMEM((D_MODEL, tf), BF16),
                        pltpu.VMEM((tf, D_MODEL), BF16)],
    )
    return pl.pallas_call(
        _ffn_kernel,
        out_shape=jax.ShapeDtypeStruct((r, D_MODEL), F32),
        grid_spec=grid_spec,
        compiler_params=_cparams(2, VMEM_LIMIT_BYTES),
        name=name,
    )(tile_expert, tile_valid, x, wg, wu, wd)


ROUTER_TM = 256
META_E0, META_E1, META_W0, META_W1, META_R0, META_R1 = range(6)


def _router_kernel(h_ref, r_ref, lt_ref, meta_ref, cnt_ref, run_scr):
    i = pl.program_id(0)

    @pl.when(i == 0)
    def _():
        run_scr[...] = jnp.zeros(run_scr.shape, F32)

    def split(x):
        hi = x.astype(BF16)
        return hi, (x - hi.astype(F32)).astype(BF16)

    h_hi, h_lo = split(h_ref[...])
    r_hi, r_lo = split(r_ref[...])
    logits = _dot(h_hi, r_hi) + (_dot(h_hi, r_lo) + _dot(h_lo, r_hi))
    lane = lax.broadcasted_iota(I32, (ROUTER_TM, LANES), 1).astype(F32)
    logits = jnp.where(lane < N_EXPERTS, logits, -jnp.inf)

    def top1(x):
        m = jnp.max(x, axis=1, keepdims=True)
        idx = jnp.min(jnp.where(x == m, lane, float(LANES)), axis=1, keepdims=True)
        return m, idx

    m0, e0 = top1(logits)
    m1, e1 = top1(jnp.where(lane == e0, -jnp.inf, logits))
    ex = jnp.exp(m1 - m0)
    w0 = 1.0 / (1.0 + ex)
    w1 = ex / (1.0 + ex)

    hot0 = jnp.where(lane == e0, 1.0, 0.0)
    hot1 = jnp.where(lane == e1, 1.0, 0.0)
    before0 = _dot(lt_ref[...], hot0.astype(BF16)) + run_scr[0:1, :]
    tot0 = jnp.sum(hot0, axis=0, keepdims=True)
    before1 = _dot(lt_ref[...], hot1.astype(BF16)) + run_scr[0:1, :] + tot0
    r0 = jnp.sum(hot0 * before0, axis=1, keepdims=True)
    r1 = jnp.sum(hot1 * before1, axis=1, keepdims=True)
    run_new = run_scr[0:1, :] + tot0 + jnp.sum(hot1, axis=0, keepdims=True)
    run_scr[...] = jnp.broadcast_to(run_new, run_scr.shape)
    cnt_ref[...] = jnp.broadcast_to(run_new, cnt_ref.shape)

    meta = jnp.zeros((ROUTER_TM, LANES), F32)
    for ln, val in ((META_E0, e0), (META_E1, e1), (META_W0, w0), (META_W1, w1),
                    (META_R0, r0), (META_R1, r1)):
        meta = jnp.where(lane == ln, val, meta)
    meta_ref[...] = meta


def _router(h, router_padded, lower_tri):
    return pl.pallas_call(
        _router_kernel,
        out_shape=(jax.ShapeDtypeStruct((TOKENS, LANES), F32), jax.ShapeDtypeStruct((8, LANES), F32)),
        grid=(TOKENS // ROUTER_TM,),
        in_specs=[pl.BlockSpec((ROUTER_TM, D_MODEL), lambda i: (i, 0)),
                  pl.BlockSpec((D_MODEL, LANES), lambda i: (0, 0)),
                  pl.BlockSpec((ROUTER_TM, ROUTER_TM), lambda i: (0, 0))],
        out_specs=(pl.BlockSpec((ROUTER_TM, LANES), lambda i: (i, 0)), pl.BlockSpec((8, LANES), lambda i: (0, 0))),
        scratch_shapes=[pltpu.VMEM((8, LANES), F32)],
        compiler_params=_cparams(1, VMEM_LIMIT_BYTES),
        name="moe_router",
    )(h, router_padded, lower_tri)


DISPATCH_TM = 512


def _dispatch_kernel(rows_ref, h_hbm, xs_in_hbm, xs_hbm, sem):
    del xs_in_hbm
    i = pl.program_id(0)

    def row_copy(tok, dst):
        return pltpu.make_async_copy(h_hbm.at[pl.ds(tok, 1)], xs_hbm.at[pl.ds(dst, 1)], sem)

    def start(r, carry):
        tok = i * DISPATCH_TM + r
        row_copy(tok, rows_ref[tok]).start()
        row_copy(tok, rows_ref[TOKENS + tok]).start()
        return carry

    lax.fori_loop(0, DISPATCH_TM, start, 0)

    def wait(r, carry):
        row_copy(0, 0).wait()
        row_copy(0, 0).wait()
        return carry

    lax.fori_loop(0, DISPATCH_TM, wait, 0)


def _dispatch(rows, h, n_rows):
    grid_spec = pltpu.PrefetchScalarGridSpec(
        num_scalar_prefetch=1,
        grid=(TOKENS // DISPATCH_TM,),
        in_specs=[pl.BlockSpec(memory_space=pl.ANY), pl.BlockSpec(memory_space=pl.ANY)],
        out_specs=pl.BlockSpec(memory_space=pl.ANY),
        scratch_shapes=[pltpu.SemaphoreType.DMA(())],
    )
    return pl.pallas_call(
        _dispatch_kernel,
        out_shape=jax.ShapeDtypeStruct((n_rows, D_MODEL), F32),
        grid_spec=grid_spec,
        input_output_aliases={2: 0},
        compiler_params=_cparams(1),
        name="moe_dispatch",
    )(rows, h, jnp.zeros((n_rows, D_MODEL), F32))


COMBINE_TM = 256


def _combine_kernel(rows_ref, y_hbm, h_ref, meta_ref, g_ref, b_ref, o_ref, ob_ref, buf, sem):
    i = pl.program_id(0)

    def row_copy(src, slot, r):
        return pltpu.make_async_copy(y_hbm.at[pl.ds(src, 1)], buf.at[slot, pl.ds(r, 1)], sem)

    def start(r, carry):
        tok = i * COMBINE_TM + r
        row_copy(rows_ref[tok], 0, r).start()
        row_copy(rows_ref[TOKENS + tok], 1, r).start()
        return carry

    lax.fori_loop(0, COMBINE_TM, start, 0)

    def wait(r, carry):
        row_copy(0, 0, 0).wait()
        row_copy(0, 1, 0).wait()
        return carry

    lax.fori_loop(0, COMBINE_TM, wait, 0)
    meta = meta_ref[...]
    f = meta[:, META_W0:META_W0 + 1] * buf[0] + meta[:, META_W1:META_W1 + 1] * buf[1]
    out = _layer_norm_rows(DN_ALPHA * h_ref[...] + f, g_ref[...], b_ref[...])
    o_ref[...] = out
    ob_ref[...] = out.astype(BF16)


def _combine(rows, y, h, meta, g, b):
    grid_spec = pltpu.PrefetchScalarGridSpec(
        num_scalar_prefetch=1,
        grid=(TOKENS // COMBINE_TM,),
        in_specs=[pl.BlockSpec(memory_space=pl.ANY),
                  pl.BlockSpec((COMBINE_TM, D_MODEL), lambda i, rows: (i, 0)),
                  pl.BlockSpec((COMBINE_TM, LANES), lambda i, rows: (i, 0)),
                  pl.BlockSpec((1, D_MODEL), lambda i, rows: (0, 0)),
                  pl.BlockSpec((1, D_MODEL), lambda i, rows: (0, 0))],
        out_specs=(pl.BlockSpec((COMBINE_TM, D_MODEL), lambda i, rows: (i, 0)),
                   pl.BlockSpec((COMBINE_TM, D_MODEL), lambda i, rows: (i, 0))),
        scratch_shapes=[pltpu.VMEM((2, COMBINE_TM, D_MODEL), F32), pltpu.SemaphoreType.DMA(())],
    )
    return pl.pallas_call(
        _combine_kernel,
        out_shape=(jax.ShapeDtypeStruct((TOKENS, D_MODEL), F32), jax.ShapeDtypeStruct((TOKENS, D_MODEL), BF16)),
        grid_spec=grid_spec,
        compiler_params=_cparams(1, VMEM_LIMIT_BYTES),
        name="moe_combine",
    )(rows, y, h, meta, g, b)


def _pack_w_in(w_in):
    src = {}
    off = 0
    for name, width in IN_SPLITS:
        src[name] = (off, off + width)
        off += width
    lead = w_in.shape[:-1]

    def cols(name):
        a, b = src[name]
        return w_in[..., a:b]

    def zeros(width):
        return jnp.zeros(lead + (width,), w_in.dtype)

    pieces = []
    for name, width in _PACKED:
        if name == 'idx_k_lo':
            pieces += [cols('idx_k'), zeros(64)]
        elif name == 'idx_k_hi':
            pieces += [zeros(64), cols('idx_k')]
        elif name == 'misc':
            pieces += [cols('idx_w'), cols('nsa_g'), zeros(128 - 28)]
        elif name == 'mla_kr':
            pieces += [cols('mla_kr'), zeros(64)]
        else:
            pieces.append(cols(name))
    return jnp.concatenate(pieces, axis=-1).astype(BF16)


def _t5_bucket(dist):
    exact = REL_BUCKETS // 2
    d = jnp.maximum(dist, 0)
    log_ratio = jnp.log(jnp.maximum(d, 1).astype(F32) / exact) / math.log(REL_MAX_DIST / exact)
    far = jnp.minimum(exact + (log_ratio * (REL_BUCKETS - exact)).astype(I32), REL_BUCKETS - 1)
    return jnp.where(d < exact, d, far)


def _bias_slabs(table4):
    i = np.arange(Q_TILE)[:, None]
    j = np.arange(K_CHUNK)[None, :]
    slabs = []
    for rel in (-256, -128, 0, -2 * SEQ):
        dist = jnp.asarray(i - j - rel, I32)
        b = table4[_t5_bucket(dist)].astype(F32)
        slabs.append(jnp.moveaxis(b, -1, 0).reshape(N_HEADS * Q_TILE, K_CHUNK))
    return jnp.stack(slabs)


def _rope_tables():
    half = QK_ROPE // 2
    inv = ROPE_BASE ** (-jnp.arange(half, dtype=F32) / half)
    ang = jnp.arange(SEQ, dtype=F32)[:, None] * inv[None, :]
    cos, sin = jnp.cos(ang), jnp.sin(ang)
    zero = jnp.zeros_like(cos)
    pad = jnp.zeros((SEQ, LANES - QK_ROPE), F32)
    cos_t = jnp.concatenate([cos, cos, pad], axis=1)
    nsin_t = jnp.concatenate([-sin, zero, pad], axis=1)
    psin_t = jnp.concatenate([zero, sin, pad], axis=1)
    return cos_t, nsin_t, psin_t


def _cover_expanded():
    n = np.arange(LANES)[:, None]
    s = np.arange(SEQ)[None, :]
    j = s // SLC_LEN
    cover = (CMP_STRIDE * n < SLC_LEN * j + SLC_LEN) & (CMP_STRIDE * n + CMP_LEN > SLC_LEN * j) & (n < N_CMP)
    return jnp.asarray(cover, BF16)


def _strict_upper(n):
    return jnp.asarray(np.arange(n)[:, None] > np.arange(n)[None, :], BF16)


def _strict_lower(n):
    return jnp.asarray(np.arange(n)[None, :] < np.arange(n)[:, None], BF16)


def _moe_layout(meta, counts):
    n_tiles = 2 * TOKENS // FFN_TM + N_EXPERTS
    cnt = counts[0, :N_EXPERTS].astype(I32)
    padded = ((cnt + FFN_TM - 1) // FFN_TM) * FFN_TM
    ends = jnp.cumsum(padded)
    offs = ends - padded
    e = meta[:, META_E0:META_E1 + 1].astype(I32)
    rank = meta[:, META_R0:META_R1 + 1].astype(I32)
    rows = (offs[e] + rank).T.reshape(-1)
    tile_start = jnp.arange(n_tiles, dtype=I32) * FFN_TM
    te = jnp.minimum(jnp.sum(tile_start[:, None] >= ends[None, :], axis=1), N_EXPERTS - 1).astype(I32)
    nv = jnp.clip(cnt[te] - (tile_start - offs[te]), 0, FFN_TM).astype(I32)
    return rows, te, nv, n_tiles * FFN_TM


def kernel(x, mem, rel_table, w_in, mla_q_norm, mla_kv_norm, mla_w_uq, mla_w_uk, mla_w_uv, nsa_pe_k, nsa_pe_v,
           nsa_ck_w1, nsa_ck_w2, nsa_cv_w1, nsa_cv_w2, w_branch, w_out, ln1_g, ln1_b, xa_wq, xa_wk, xa_wv, xa_wo,
           ln2_g, ln2_b, ffn_w_gate, ffn_w_up, ffn_w_down, moe_router, moe_w_gate, moe_w_up, moe_w_down,
           ln3_g, ln3_b):
    h = x.reshape(TOKENS, D_MODEL)
    hb = h.astype(BF16)
    mem_b = mem.reshape(BATCH * MEM_LEN, D_MODEL).astype(BF16)

    w_in_p = _pack_w_in(w_in)
    dsa_slabs = _bias_slabs(rel_table[:, :N_HEADS])
    nsa_slabs = _bias_slabs(rel_table[:, N_HEADS:])
    cos_t, nsin_t, psin_t = _rope_tables()
    cover = _cover_expanded()
    upper = _strict_upper(SB_TILE)
    lower = _strict_lower(ROUTER_TM)

    uq = mla_w_uq.reshape(DEPTH, Q_LORA, N_HEADS, QK_NOPE + QK_ROPE)
    w_qn = uq[..., :QK_NOPE].reshape(DEPTH, Q_LORA, N_HEADS * QK_NOPE).astype(BF16)
    w_qp = jnp.concatenate([uq[..., QK_NOPE:], jnp.zeros((DEPTH, Q_LORA, N_HEADS, LANES - QK_ROPE), F32)],
                           axis=-1).reshape(DEPTH, Q_LORA, N_HEADS * LANES).astype(BF16)
    w_uk = mla_w_uk.astype(BF16)
    w_uv = mla_w_uv.astype(BF16)
    w_br = w_branch.astype(BF16)
    w_o = w_out.astype(BF16)
    xq, xk, xv, xo = (w.astype(BF16) for w in (xa_wq, xa_wk, xa_wv, xa_wo))
    router_p = jnp.pad(moe_router, ((0, 0), (0, 0), (0, LANES - N_EXPERTS)))
    row2 = lambda a: a.reshape(DEPTH, 1, -1)
    g1, b1, g2, b2, g3, b3 = (row2(a) for a in (ln1_g, ln1_b, ln2_g, ln2_b, ln3_g, ln3_b))
    qn, kvn = row2(mla_q_norm), row2(mla_kv_norm)
    pek = nsa_pe_k.reshape(DEPTH, 1, CMP_LEN * HEAD_DIM)
    pev = nsa_pe_v.reshape(DEPTH, 1, CMP_LEN * HEAD_DIM)
    dense_te = jnp.zeros((TOKENS // FFN_TM,), I32)
    dense_nv = jnp.full((TOKENS // FFN_TM,), FFN_TM, I32)

    for layer in range(DEPTH):
        z = _matmul(hb, w_in_p[layer], tm=1024, tn=1024, out_dtype=BF16, name="in_proj")
        o_dsa = _dsa(z, dsa_slabs)
        o_sb = _stick_breaking(z, upper)
        group = lambda name: z[:, _OFF[name]:_OFF[name] + HEAD_DIM].reshape(BATCH * LANES, CMP_STRIDE * HEAD_DIM)
        k_cmp, v_cmp = _nsa_compress(group('nsa_kc'), group('nsa_vc'), pek[layer], pev[layer],
                                     nsa_ck_w1[layer], nsa_ck_w2[layer], nsa_cv_w1[layer], nsa_cv_w2[layer])
        o_nsa = _nsa(z, k_cmp, v_cmp, cover, nsa_slabs)
        q_cat, k_cat, v_mla = _mla_prep(z, qn[layer], kvn[layer], w_qn[layer], w_qp[layer], w_uk[layer],
                                        w_uv[layer], cos_t, nsin_t, psin_t, tm=512)
        o_mla = _flash(q_cat, k_cat, v_mla, dk=MLA_QK, kv_len=SEQ, scale=(QK_NOPE + QK_ROPE) ** -0.5,
                       causal=True, name="mla_attention")
        mixed = _merge((o_dsa, o_sb, o_nsa, o_mla), z, w_br[layer], tm=512, tn=512)
        h, hb = _matmul_res_ln(mixed, w_o[layer], h, g1[layer], b1[layer], tm=256, name="out_proj_ln")

        q_x = _matmul(hb, xq[layer], tm=1024, tn=512, out_dtype=BF16, name="xa_q_proj")
        k_x = _matmul(mem_b, xk[layer], tm=1024, tn=512, out_dtype=BF16, name="xa_k_proj")
        v_x = _matmul(mem_b, xv[layer], tm=1024, tn=512, out_dtype=BF16, name="xa_v_proj")
        o_x = _flash(q_x, k_x, v_x, dk=HEAD_DIM, kv_len=MEM_LEN, scale=HEAD_DIM ** -0.5, causal=False,
                     name="cross_attention")
        h, hb = _matmul_res_ln(o_x, xo[layer], h, g2[layer], b2[layer], tm=256, name="xa_out_proj_ln")

        i = layer // 2
        if layer % 2 == 0:
            y = _grouped_swiglu(dense_te, dense_nv, hb, ffn_w_gate[i][None], ffn_w_up[i][None],
                                ffn_w_down[i][None], tf=256, name="dense_swiglu")
            h, hb = _res_ln(y, h, g3[layer], b3[layer], tm=256, name="ffn_res_ln")
        else:
            meta, counts = _router(h, router_p[i], lower)
            rows, te, nv, n_rows = _moe_layout(meta, counts)
            xs = _dispatch(rows, h, n_rows)
            y = _grouped_swiglu(te, nv, xs, moe_w_gate[i], moe_w_up[i], moe_w_down[i], tf=256,
                                name="moe_swiglu")
            h, hb = _combine(rows, y, h, meta, g3[layer], b3[layer])
    return h.reshape(BATCH, SEQ, D_MODEL)
```

```python
import functools
import math

import jax
import jax.numpy as jnp
import numpy as np
from jax import lax
from jax.experimental import pallas as pl
from jax.experimental.pallas import tpu as pltpu

F32 = jnp.float32
BF16 = jnp.bfloat16
I32 = jnp.int32

D_MODEL = 2048
BATCH = 4
SEQ = 2048
DEPTH = 4
TOKENS = BATCH * SEQ
MEM_LEN = 256
HEAD_DIM = 128
N_HEADS = 4
DSA_TOPK = min(256, SEQ // 4)
IDX_HEADS = 16
IDX_DIM = 64
CMP_LEN = 32
CMP_STRIDE = 16
CMP_HIDDEN = 256
N_CMP = (SEQ - CMP_LEN) // CMP_STRIDE + 1
SLC_LEN = 64
SLC_SHIFT = 6
N_SLC = SEQ // SLC_LEN
N_SEL = min(16, N_SLC)
WINDOW = 512
FORCE_SCORE = 1.0e4
Q_LORA = 512
KV_LORA = 512
QK_NOPE = 128
QK_ROPE = 64
ROPE_BASE = 10000.0
N_MIXERS = 4
MIX_WIDTH = 512
REL_BUCKETS = 32
REL_MAX_DIST = 128
D_FF = 5632
N_EXPERTS = 8
DN_ALPHA = (2 * DEPTH) ** 0.25
LN_EPS = 1e-5
RMS_EPS = 1e-6
NEG_BIG = -1.0e30

IN_SPLITS = (
    ('dsa_q', 512), ('dsa_k', 128), ('dsa_v', 128),
    ('idx_q', IDX_HEADS * IDX_DIM), ('idx_k', IDX_DIM), ('idx_w', IDX_HEADS),
    ('sb_q', 512), ('sb_k', 512), ('sb_v', 512),
    ('nsa_q', 512),
    ('nsa_kc', 128), ('nsa_vc', 128), ('nsa_ks', 128), ('nsa_vs', 128),
    ('nsa_kw', 128), ('nsa_vw', 128), ('nsa_g', 12),
    ('mla_cq', Q_LORA), ('mla_ckv', KV_LORA), ('mla_kr', QK_ROPE),
    ('gates', N_MIXERS * D_MODEL),
)

LANES = 128
Q_TILE = 128
K_CHUNK = 256
N_CHUNKS = SEQ // K_CHUNK
VMEM_LIMIT_BYTES = 56 * 1024 * 1024
INT_MIN = -2147483648
KEY_NEG_INF = -2139095041

_PACKED = (
    ('idx_q', 1024), ('dsa_q', 512), ('sb_q', 512), ('sb_k', 512), ('sb_v', 512), ('nsa_q', 512),
    ('mla_cq', 512), ('mla_ckv', 512),
    ('dsa_k', 128), ('dsa_v', 128), ('idx_k_lo', 128), ('idx_k_hi', 128), ('misc', 128),
    ('nsa_kc', 128), ('nsa_vc', 128), ('nsa_ks', 128), ('nsa_vs', 128), ('nsa_kw', 128), ('nsa_vw', 128),
    ('mla_kr', 128), ('gates', 8192),
)
_OFF = {}
_o = 0
for _n, _w in _PACKED:
    _OFF[_n] = _o
    _o += _w
Z_WIDTH = _o
MISC_IDXW = 0
MISC_NSAG = 16


def _cparams(n_axes, vmem=None):
    return pltpu.CompilerParams(dimension_semantics=("arbitrary",) * n_axes, vmem_limit_bytes=vmem)


def _dot(a, b):
    return jnp.dot(a, b, preferred_element_type=F32)


def _dot_nt(a, b):
    return lax.dot_general(a, b, (((1,), (1,)), ((), ())), preferred_element_type=F32)


def _layer_norm_rows(v, g, b):
    mu = jnp.mean(v, axis=-1, keepdims=True)
    d = v - mu
    var = jnp.mean(d * d, axis=-1, keepdims=True)
    return d * lax.rsqrt(var + LN_EPS) * g + b


def _matmul_kernel(x_ref, w_ref, o_ref):
    o_ref[...] = _dot(x_ref[...].astype(BF16), w_ref[...].astype(BF16)).astype(o_ref.dtype)


def _matmul(x, w, layer, *, tm, tn, out_dtype, name):
    m, k = x.shape
    n = w.shape[2]
    return pl.pallas_call(
        _matmul_kernel,
        out_shape=jax.ShapeDtypeStruct((m, n), out_dtype),
        grid=(n // tn, m // tm),
        in_specs=[pl.BlockSpec((tm, k), lambda j, i: (i, 0)),
                  pl.BlockSpec((None, k, tn), lambda j, i: (layer, 0, j))],
        out_specs=pl.BlockSpec((tm, tn), lambda j, i: (i, j)),
        compiler_params=_cparams(2, VMEM_LIMIT_BYTES),
        name=name,
    )(x, w)


def _matmul_res_ln_kernel(x_ref, w_ref, h_ref, g_ref, b_ref, o_ref, ob_ref):
    y = _dot(x_ref[...], w_ref[...])
    out = _layer_norm_rows(DN_ALPHA * h_ref[...] + y, g_ref[...], b_ref[...])
    o_ref[...] = out
    ob_ref[...] = out.astype(BF16)


def _matmul_res_ln(x, w, h, g, b, layer, *, tm, name):
    m, k = x.shape
    d = w.shape[2]
    return pl.pallas_call(
        _matmul_res_ln_kernel,
        out_shape=(jax.ShapeDtypeStruct((m, d), F32), jax.ShapeDtypeStruct((m, d), BF16)),
        grid=(m // tm,),
        in_specs=[pl.BlockSpec((tm, k), lambda i: (i, 0)),
                  pl.BlockSpec((None, k, d), lambda i: (layer, 0, 0)),
                  pl.BlockSpec((tm, d), lambda i: (i, 0)),
                  pl.BlockSpec((None, 1, d), lambda i: (layer, 0, 0)),
                  pl.BlockSpec((None, 1, d), lambda i: (layer, 0, 0))],
        out_specs=(pl.BlockSpec((tm, d), lambda i: (i, 0)), pl.BlockSpec((tm, d), lambda i: (i, 0))),
        compiler_params=_cparams(1, VMEM_LIMIT_BYTES),
        name=name,
    )(x, w, h, g, b)


def _res_ln_kernel(y_ref, h_ref, g_ref, b_ref, o_ref, ob_ref):
    out = _layer_norm_rows(DN_ALPHA * h_ref[...] + y_ref[...], g_ref[...], b_ref[...])
    o_ref[...] = out
    ob_ref[...] = out.astype(BF16)


def _res_ln(y, h, g, b, layer, *, tm, name):
    m, d = h.shape
    return pl.pallas_call(
        _res_ln_kernel,
        out_shape=(jax.ShapeDtypeStruct((m, d), F32), jax.ShapeDtypeStruct((m, d), BF16)),
        grid=(m // tm,),
        in_specs=[pl.BlockSpec((tm, d), lambda i: (i, 0)),
                  pl.BlockSpec((tm, d), lambda i: (i, 0)),
                  pl.BlockSpec((None, 1, d), lambda i: (layer, 0, 0)),
                  pl.BlockSpec((None, 1, d), lambda i: (layer, 0, 0))],
        out_specs=(pl.BlockSpec((tm, d), lambda i: (i, 0)), pl.BlockSpec((tm, d), lambda i: (i, 0))),
        compiler_params=_cparams(1, VMEM_LIMIT_BYTES),
        name=name,
    )(y, h, g, b)


def _merge_kernel(b0, b1, b2, b3, g0, g1, g2, g3, wb_ref, o_ref):
    acc = None
    for n, (br, gr) in enumerate(((b0, g0), (b1, g1), (b2, g2), (b3, g3))):
        y = _dot(br[...], wb_ref[n])
        gy = jax.nn.sigmoid(gr[...].astype(F32)) * y
        acc = gy if acc is None else acc + gy
    o_ref[...] = acc.astype(o_ref.dtype)


def _merge(branches, z, wb, layer, *, tm, tn):
    m = z.shape[0]
    gate_specs = []
    for n in range(N_MIXERS):
        base = (_OFF['gates'] + n * D_MODEL) // tn
        gate_specs.append(pl.BlockSpec((tm, tn), lambda j, i, base=base: (i, base + j)))
    return pl.pallas_call(
        _merge_kernel,
        out_shape=jax.ShapeDtypeStruct((m, D_MODEL), BF16),
        grid=(D_MODEL // tn, m // tm),
        in_specs=[pl.BlockSpec((tm, MIX_WIDTH), lambda j, i: (i, 0))] * N_MIXERS + gate_specs
        + [pl.BlockSpec((None, N_MIXERS, MIX_WIDTH, tn), lambda j, i: (layer, 0, 0, j))],
        out_specs=pl.BlockSpec((tm, tn), lambda j, i: (i, j)),
        compiler_params=_cparams(2, VMEM_LIMIT_BYTES),
        name="branch_merge",
    )(*branches, z, z, z, z, wb)


def _sortable_key(score):
    score = jnp.where(score == 0.0, 0.0, score)
    bits = pltpu.bitcast(score, I32)
    return bits ^ (jnp.right_shift(bits, 31) & 0x7FFFFFFF)


def _kth_largest_key(key_scr, k, n_chunks):
    def body(i, lo_u):
        cand_u = lo_u | jnp.left_shift(jnp.int32(1), 31 - i)
        cand_s = cand_u ^ INT_MIN

        def count(c, cnt):
            return cnt + jnp.where(key_scr[c] >= cand_s, 1.0, 0.0)

        cnt = lax.fori_loop(0, n_chunks, count, jnp.zeros((Q_TILE, K_CHUNK), F32))
        tot = jnp.sum(cnt, axis=1, keepdims=True)
        return jnp.where(tot >= float(k), cand_u, lo_u)

    lo_u = lax.fori_loop(0, 32, body, jnp.zeros((Q_TILE, 1), I32))
    return lo_u ^ INT_MIN


def _slab_index(rel):
    return jnp.where(rel == 0, 2, jnp.where(rel == -128, 1, jnp.where(rel == -256, 0, 3)))


def _stack_heads(q):
    return jnp.concatenate([q[:, h * HEAD_DIM:(h + 1) * HEAD_DIM] for h in range(N_HEADS)], axis=0)


def _mqa_masked_attention(q_ref, k_ref, v_ref, slab_ref, t0, c_lo, c_hi, ok_fn, m_scr, l_scr, acc_scr):
    scale = HEAD_DIM ** -0.5
    m_scr[...] = jnp.full(m_scr.shape, NEG_BIG, F32)
    l_scr[...] = jnp.zeros(l_scr.shape, F32)
    acc_scr[...] = jnp.zeros(acc_scr.shape, F32)

    def body(c, carry):
        start = pl.multiple_of(c * K_CHUNK, K_CHUNK)
        k = k_ref[pl.ds(start, K_CHUNK), :]
        v = v_ref[pl.ds(start, K_CHUNK), :]
        slab = _slab_index(c * K_CHUNK - t0)
        ok = ok_fn(c)
        for h in range(N_HEADS):
            s = _dot_nt(q_ref[:, h * HEAD_DIM:(h + 1) * HEAD_DIM], k) * scale + slab_ref[slab, h]
            s = jnp.where(ok, s, NEG_BIG)
            m_old = m_scr[h]
            m_new = jnp.maximum(m_old, jnp.max(s, axis=1, keepdims=True))
            p = jnp.where(ok, jnp.exp(s - m_new), 0.0)
            alpha = jnp.exp(m_old - m_new)
            l_scr[h] = alpha * l_scr[h] + jnp.sum(p, axis=1, keepdims=True)
            acc_scr[h] = alpha * acc_scr[h] + _dot(p.astype(BF16), v)
            m_scr[h] = m_new
        return carry

    lax.fori_loop(c_lo, c_hi, body, 0)
    return [acc_scr[h] / l_scr[h] for h in range(N_HEADS)]


def _row_pos(t0):
    return t0 + lax.broadcasted_iota(I32, (Q_TILE, K_CHUNK), 0)


def _col_pos(c):
    return c * K_CHUNK + lax.broadcasted_iota(I32, (Q_TILE, K_CHUNK), 1)


_ATTN_SCRATCH = [pltpu.VMEM((N_CHUNKS, Q_TILE, K_CHUNK), I32),
                 pltpu.VMEM((N_HEADS, Q_TILE, 1), F32),
                 pltpu.VMEM((N_HEADS, Q_TILE, 1), F32),
                 pltpu.VMEM((N_HEADS, Q_TILE, HEAD_DIM), F32)]
_SLAB_SPEC = pl.BlockSpec((4, N_HEADS, Q_TILE, K_CHUNK), lambda b, i: (0, 0, 0, 0))


def _dsa_kernel(q_ref, iq_ref, misc_ref, k_ref, v_ref, iklo_ref, ikhi_ref, slab_ref, o_ref,
                key_scr, m_scr, l_scr, acc_scr):
    qb = pl.program_id(1)
    t0 = qb * Q_TILE
    n_chunks = qb // 2 + 1
    misc = misc_ref[...].astype(F32)
    row = t0 + lax.broadcasted_iota(I32, (Q_TILE, K_CHUNK), 0)

    key_scr[...] = jnp.full(key_scr.shape, KEY_NEG_INF, I32)

    def score_body(c, carry):
        start = pl.multiple_of(c * K_CHUNK, K_CHUNK)
        ik_lo = iklo_ref[pl.ds(start, K_CHUNK), :]
        ik_hi = ikhi_ref[pl.ds(start, K_CHUNK), :]
        acc = jnp.zeros((Q_TILE, K_CHUNK), F32)
        for p in range(IDX_HEADS // 2):
            pair = iq_ref[:, p * LANES:(p + 1) * LANES]
            w0 = misc[:, MISC_IDXW + 2 * p:MISC_IDXW + 2 * p + 1]
            w1 = misc[:, MISC_IDXW + 2 * p + 1:MISC_IDXW + 2 * p + 2]
            acc = acc + jnp.maximum(_dot_nt(pair, ik_lo), 0.0) * w0
            acc = acc + jnp.maximum(_dot_nt(pair, ik_hi), 0.0) * w1
        col = c * K_CHUNK + lax.broadcasted_iota(I32, (Q_TILE, K_CHUNK), 1)
        score = jnp.where(col <= row, acc, -jnp.inf)
        key_scr[c] = _sortable_key(score)
        return carry

    lax.fori_loop(0, n_chunks, score_body, 0)
    thr = _kth_largest_key(key_scr, DSA_TOPK, n_chunks)

    def ok_fn(c):
        return (key_scr[c] >= thr) & (_col_pos(c) <= row)

    o = _mqa_masked_attention(q_ref, k_ref, v_ref, slab_ref, t0, 0, n_chunks, ok_fn, m_scr, l_scr, acc_scr)
    o_ref[...] = jnp.concatenate(o, axis=1).astype(o_ref.dtype)


def _zspec_rows(name, width, rows, index_fn):
    cb, rem = divmod(_OFF[name], width)
    assert rem == 0, name
    return pl.BlockSpec((rows, width), functools.partial(index_fn, cb))


def _dsa(z, slabs):
    nq = SEQ // Q_TILE

    def qidx(cb, b, i):
        return (b * nq + i, cb)

    def kidx(cb, b, i):
        return (b, cb)

    return pl.pallas_call(
        _dsa_kernel,
        out_shape=jax.ShapeDtypeStruct((TOKENS, MIX_WIDTH), BF16),
        grid=(BATCH, nq),
        in_specs=[_zspec_rows('dsa_q', 512, Q_TILE, qidx),
                  _zspec_rows('idx_q', 1024, Q_TILE, qidx),
                  _zspec_rows('misc', 128, Q_TILE, qidx),
                  _zspec_rows('dsa_k', 128, SEQ, kidx),
                  _zspec_rows('dsa_v', 128, SEQ, kidx),
                  _zspec_rows('idx_k_lo', 128, SEQ, kidx),
                  _zspec_rows('idx_k_hi', 128, SEQ, kidx),
                  _SLAB_SPEC],
        out_specs=pl.BlockSpec((Q_TILE, MIX_WIDTH), lambda b, i: (b * nq + i, 0)),
        scratch_shapes=_ATTN_SCRATCH,
        compiler_params=_cparams(2, VMEM_LIMIT_BYTES),
        name="dsa_attention",
    )(z, z, z, z, z, z, z, slabs)


def _nsa_compress_kernel(xk_ref, xv_ref, pek_ref, pev_ref, kw1_ref, kw2_ref, vw1_ref, vw2_ref, ok_ref, ov_ref):
    half = CMP_STRIDE * HEAD_DIM

    def compress(x_ref, pe_ref, w1_ref, w2_ref):
        x = x_ref[...].astype(F32)
        a = _dot((x + pe_ref[:, :half]).astype(BF16), w1_ref[:half, :].astype(BF16))
        b = _dot((x + pe_ref[:, half:]).astype(BF16), w1_ref[half:, :].astype(BF16))
        hid = a + pltpu.roll(b, b.shape[0] - 1, 0)
        return _dot(jax.nn.gelu(hid).astype(BF16), w2_ref[...].astype(BF16))

    ok_ref[...] = compress(xk_ref, pek_ref, kw1_ref, kw2_ref)
    ov_ref[...] = compress(xv_ref, pev_ref, vw1_ref, vw2_ref)


def _nsa_compress(xk, xv, pek, pev, kw1, kw2, vw1, vw2):
    rows = xk.shape[0]
    full = lambda a: pl.BlockSpec(a.shape, lambda i: (0,) * a.ndim)
    args = (xk, xv, pek, pev, kw1, kw2, vw1, vw2)
    return pl.pallas_call(
        _nsa_compress_kernel,
        out_shape=(jax.ShapeDtypeStruct((rows, HEAD_DIM), F32), jax.ShapeDtypeStruct((rows, HEAD_DIM), F32)),
        grid=(1,),
        in_specs=[full(a) for a in args],
        out_specs=(pl.BlockSpec((rows, HEAD_DIM), lambda i: (0, 0)), pl.BlockSpec((rows, HEAD_DIM), lambda i: (0, 0))),
        compiler_params=_cparams(1, VMEM_LIMIT_BYTES),
        name="nsa_compress",
    )(*args)


def _nsa_kernel(q_ref, misc_ref, kc_ref, vc_ref, ks_ref, vs_ref, kw_ref, vw_ref, cov_ref, slab_ref, o_ref,
                key_scr, m_scr, l_scr, acc_scr):
    qb = pl.program_id(1)
    t0 = qb * Q_TILE
    scale = HEAD_DIM ** -0.5
    qs = _stack_heads(q_ref[...])
    misc = misc_ref[...].astype(F32)

    n_idx = lax.broadcasted_iota(I32, (N_HEADS * Q_TILE, LANES), 1)
    t_idx = t0 + (lax.broadcasted_iota(I32, (N_HEADS * Q_TILE, LANES), 0) & (Q_TILE - 1))
    cmp_ok = (CMP_STRIDE * n_idx + CMP_LEN - 1) <= t_idx
    lc = jnp.where(cmp_ok, _dot_nt(qs, kc_ref[...].astype(BF16)) * scale, NEG_BIG)
    e = jnp.exp(lc - jnp.max(lc, axis=1, keepdims=True))
    p_cmp = jnp.where(cmp_ok, e / jnp.sum(e, axis=1, keepdims=True), 0.0)
    o_cmp = _dot(p_cmp.astype(BF16), vc_ref[...].astype(BF16))

    p_sum = p_cmp[0:Q_TILE] + p_cmp[Q_TILE:2 * Q_TILE] + p_cmp[2 * Q_TILE:3 * Q_TILE] + p_cmp[3 * Q_TILE:]
    p_hi = p_sum.astype(BF16)
    p_lo = (p_sum - p_hi.astype(F32)).astype(BF16)
    cur = jnp.right_shift(t0 + lax.broadcasted_iota(I32, (Q_TILE, K_CHUNK), 0), SLC_SHIFT)
    for c in range(N_CHUNKS):
        cov = cov_ref[:, c * K_CHUNK:(c + 1) * K_CHUNK]
        imp = _dot(p_hi, cov) + _dot(p_lo, cov)
        jb = jnp.right_shift(c * K_CHUNK + lax.broadcasted_iota(I32, (Q_TILE, K_CHUNK), 1), SLC_SHIFT)
        forced = (jb == 0) | (jb == cur) | (jb == cur - 1)
        imp = jnp.where(jb <= cur, imp + jnp.where(forced, FORCE_SCORE, 0.0), -jnp.inf)
        key_scr[c] = _sortable_key(imp)
    n_chunks = qb // 2 + 1
    thr = _kth_largest_key(key_scr, N_SEL * SLC_LEN, n_chunks)
    row = _row_pos(t0)

    def sel_ok(c):
        return (key_scr[c] >= thr) & (_col_pos(c) <= row)

    o_slc = _mqa_masked_attention(q_ref, ks_ref, vs_ref, slab_ref, t0, 0, n_chunks, sel_ok, m_scr, l_scr, acc_scr)

    def win_ok(c):
        dist = row - _col_pos(c)
        return (dist >= 0) & (dist < WINDOW)

    c_lo = jnp.maximum(qb - WINDOW // Q_TILE, 0) // 2
    o_win = _mqa_masked_attention(q_ref, kw_ref, vw_ref, slab_ref, t0, c_lo, n_chunks, win_ok, m_scr, l_scr, acc_scr)

    outs = []
    for h in range(N_HEADS):
        g = jax.nn.sigmoid(misc[:, MISC_NSAG + 3 * h:MISC_NSAG + 3 * h + 3])
        rows = slice(h * Q_TILE, (h + 1) * Q_TILE)
        outs.append(g[:, 0:1] * o_cmp[rows] + g[:, 1:2] * o_slc[h] + g[:, 2:3] * o_win[h])
    o_ref[...] = jnp.concatenate(outs, axis=1).astype(o_ref.dtype)


def _nsa(z, k_cmp, v_cmp, cov, slabs):
    nq = SEQ // Q_TILE

    def qidx(cb, b, i):
        return (b * nq + i, cb)

    def kidx(cb, b, i):
        return (b, cb)

    cmp_spec = pl.BlockSpec((LANES, HEAD_DIM), lambda b, i: (b, 0))
    return pl.pallas_call(
        _nsa_kernel,
        out_shape=jax.ShapeDtypeStruct((TOKENS, MIX_WIDTH), BF16),
        grid=(BATCH, nq),
        in_specs=[_zspec_rows('nsa_q', 512, Q_TILE, qidx),
                  _zspec_rows('misc', 128, Q_TILE, qidx),
                  cmp_spec, cmp_spec,
                  _zspec_rows('nsa_ks', 128, SEQ, kidx),
                  _zspec_rows('nsa_vs', 128, SEQ, kidx),
                  _zspec_rows('nsa_kw', 128, SEQ, kidx),
                  _zspec_rows('nsa_vw', 128, SEQ, kidx),
                  pl.BlockSpec((LANES, SEQ), lambda b, i: (0, 0)),
                  _SLAB_SPEC],
        out_specs=pl.BlockSpec((Q_TILE, MIX_WIDTH), lambda b, i: (b * nq + i, 0)),
        scratch_shapes=_ATTN_SCRATCH,
        compiler_params=_cparams(2, VMEM_LIMIT_BYTES),
        name="nsa_attention",
    )(z, z, k_cmp, v_cmp, z, z, z, z, cov, slabs)


SB_TILE = 256


def _sb_kernel(q_ref, k_ref, v_ref, u_ref, o_ref):
    qb = pl.program_id(2)
    scale = HEAD_DIM ** -0.5
    q = q_ref[...]
    upper = u_ref[...]
    row = qb * SB_TILE + lax.broadcasted_iota(I32, (SB_TILE, SB_TILE), 0)
    lane = lax.broadcasted_iota(I32, (SB_TILE, SB_TILE), 1)

    def body(i, carry):
        later, acc = carry
        c = qb - i
        start = pl.multiple_of(c * SB_TILE, SB_TILE)
        k = k_ref[pl.ds(start, SB_TILE), :]
        v = v_ref[pl.ds(start, SB_TILE), :]
        zl = _dot_nt(q, k) * scale
        strict = (c * SB_TILE + lane) < row
        log_beta = jnp.minimum(zl, 0.0) - jnp.log1p(jnp.exp(-jnp.abs(zl)))
        log_keep = jnp.where(strict, log_beta - zl, 0.0)
        keep_hi = log_keep.astype(BF16)
        keep_lo = (log_keep - keep_hi.astype(F32)).astype(BF16)
        within = _dot(keep_hi, upper) + _dot(keep_lo, upper)
        a = jnp.where(strict, jnp.exp(log_beta + within + later), 0.0)
        acc = acc + _dot(a.astype(BF16), v)
        later = later + jnp.sum(log_keep, axis=1, keepdims=True)
        return later, acc

    _, acc = lax.fori_loop(0, qb + 1, body,
                           (jnp.zeros((SB_TILE, 1), F32), jnp.zeros((SB_TILE, HEAD_DIM), F32)))
    o_ref[...] = acc.astype(o_ref.dtype)


def _stick_breaking(z, upper):
    nq = SEQ // SB_TILE
    qcb, kcb, vcb = _OFF['sb_q'] // HEAD_DIM, _OFF['sb_k'] // HEAD_DIM, _OFF['sb_v'] // HEAD_DIM
    return pl.pallas_call(
        _sb_kernel,
        out_shape=jax.ShapeDtypeStruct((TOKENS, MIX_WIDTH), BF16),
        grid=(BATCH, N_HEADS, nq),
        in_specs=[pl.BlockSpec((SB_TILE, HEAD_DIM), lambda b, h, i: (b * nq + i, qcb + h)),
                  pl.BlockSpec((SEQ, HEAD_DIM), lambda b, h, i: (b, kcb + h)),
                  pl.BlockSpec((SEQ, HEAD_DIM), lambda b, h, i: (b, vcb + h)),
                  pl.BlockSpec((SB_TILE, SB_TILE), lambda b, h, i: (0, 0))],
        out_specs=pl.BlockSpec((SB_TILE, HEAD_DIM), lambda b, h, i: (b * nq + i, h)),
        compiler_params=_cparams(3, VMEM_LIMIT_BYTES),
        name="stick_breaking_attention",
    )(z, z, z, upper)


MLA_QK = 2 * HEAD_DIM


def _rope128(x, cos, nsin, psin):
    return x * cos + pltpu.roll(x, 96, 1) * nsin + pltpu.roll(x, 32, 1) * psin


def _mla_prep_kernel(cq_ref, ckv_ref, kr_ref, qn_ref, kvn_ref, wqn_ref, wqp_ref, wuk_ref, wuv_ref,
                     cos_ref, nsin_ref, psin_ref, q_ref, k_ref, v_ref):
    def rms(x, g):
        return (x * lax.rsqrt(jnp.mean(x * x, axis=-1, keepdims=True) + RMS_EPS) * g).astype(BF16)

    cos, nsin, psin = cos_ref[...], nsin_ref[...], psin_ref[...]
    xq = rms(cq_ref[...].astype(F32), qn_ref[...])
    xc = rms(ckv_ref[...].astype(F32), kvn_ref[...])
    q_nope = _dot(xq, wqn_ref[...])
    q_rope = _dot(xq, wqp_ref[...])
    k_nope = _dot(xc, wuk_ref[...])
    k_rope = _rope128(kr_ref[...].astype(F32), cos, nsin, psin)
    q_parts, k_parts = [], []
    for h in range(N_HEADS):
        cols = slice(h * HEAD_DIM, (h + 1) * HEAD_DIM)
        q_parts += [q_nope[:, cols], _rope128(q_rope[:, cols], cos, nsin, psin)]
        k_parts += [k_nope[:, cols], k_rope]
    q_ref[...] = jnp.concatenate(q_parts, axis=1).astype(BF16)
    k_ref[...] = jnp.concatenate(k_parts, axis=1).astype(BF16)
    v_ref[...] = _dot(xc, wuv_ref[...]).astype(BF16)


def _mla_prep(z, qn, kvn, wqn, wqp, wuk, wuv, cos, nsin, psin, *, tm):
    nt = SEQ // tm

    def zidx(cb, i):
        return (i, cb)

    wspec = pl.BlockSpec((Q_LORA, 512), lambda i: (0, 0))
    nspec = pl.BlockSpec((1, 512), lambda i: (0, 0))
    tspec = pl.BlockSpec((tm, LANES), lambda i: (i % nt, 0))
    return pl.pallas_call(
        _mla_prep_kernel,
        out_shape=(jax.ShapeDtypeStruct((TOKENS, N_HEADS * MLA_QK), BF16),
                   jax.ShapeDtypeStruct((TOKENS, N_HEADS * MLA_QK), BF16),
                   jax.ShapeDtypeStruct((TOKENS, N_HEADS * HEAD_DIM), BF16)),
        grid=(TOKENS // tm,),
        in_specs=[_zspec_rows('mla_cq', 512, tm, zidx), _zspec_rows('mla_ckv', 512, tm, zidx),
                  _zspec_rows('mla_kr', 128, tm, zidx), nspec, nspec, wspec, wspec, wspec, wspec,
                  tspec, tspec, tspec],
        out_specs=(pl.BlockSpec((tm, N_HEADS * MLA_QK), lambda i: (i, 0)),
                   pl.BlockSpec((tm, N_HEADS * MLA_QK), lambda i: (i, 0)),
                   pl.BlockSpec((tm, N_HEADS * HEAD_DIM), lambda i: (i, 0))),
        compiler_params=_cparams(1, VMEM_LIMIT_BYTES),
        name="mla_prep",
    )(z, z, z, qn, kvn, wqn, wqp, wuk, wuv, cos, nsin, psin)


FLASH_TILE = 256


def _flash_kernel(q_ref, k_ref, v_ref, o_ref, *, scale, causal, n_kv):
    qb = pl.program_id(2)
    q = q_ref[...]
    row = qb * FLASH_TILE + lax.broadcasted_iota(I32, (FLASH_TILE, FLASH_TILE), 0)
    lane = lax.broadcasted_iota(I32, (FLASH_TILE, FLASH_TILE), 1)

    def body(c, carry):
        m, l, acc = carry
        start = pl.multiple_of(c * FLASH_TILE, FLASH_TILE)
        k = k_ref[pl.ds(start, FLASH_TILE), :]
        v = v_ref[pl.ds(start, FLASH_TILE), :]
        s = _dot_nt(q, k) * scale
        if causal:
            s = jnp.where(c * FLASH_TILE + lane <= row, s, NEG_BIG)
        m_new = jnp.maximum(m, jnp.max(s, axis=1, keepdims=True))
        p = jnp.exp(s - m_new)
        alpha = jnp.exp(m - m_new)
        return m_new, alpha * l + jnp.sum(p, axis=1, keepdims=True), alpha * acc + _dot(p.astype(BF16), v)

    init = (jnp.full((FLASH_TILE, 1), NEG_BIG, F32), jnp.zeros((FLASH_TILE, 1), F32),
            jnp.zeros((FLASH_TILE, HEAD_DIM), F32))
    _, l, acc = lax.fori_loop(0, qb + 1 if causal else n_kv, body, init)
    o_ref[...] = (acc / l).astype(o_ref.dtype)


def _flash(q, k, v, *, dk, kv_len, scale, causal, name):
    nq = SEQ // FLASH_TILE
    return pl.pallas_call(
        functools.partial(_flash_kernel, scale=scale, causal=causal, n_kv=kv_len // FLASH_TILE),
        out_shape=jax.ShapeDtypeStruct((TOKENS, N_HEADS * HEAD_DIM), BF16),
        grid=(BATCH, N_HEADS, nq),
        in_specs=[pl.BlockSpec((FLASH_TILE, dk), lambda b, h, i: (b * nq + i, h)),
                  pl.BlockSpec((kv_len, dk), lambda b, h, i: (b, h)),
                  pl.BlockSpec((kv_len, HEAD_DIM), lambda b, h, i: (b, h))],
        out_specs=pl.BlockSpec((FLASH_TILE, HEAD_DIM), lambda b, h, i: (b * nq + i, h)),
        compiler_params=_cparams(3, VMEM_LIMIT_BYTES),
        name=name,
    )(q, k, v)


FFN_TM = 1024
FFN_SUB = 256
FFN_SUB_SHIFT = 8


def _swiglu_step(load_x, nv, j, wg_ref, wu_ref, wd_ref, o_ref, wg_b, wu_b, wd_b):
    @pl.when(nv > 0)
    def _():
        wg_b[...] = wg_ref[0].astype(BF16)
        wu_b[...] = wu_ref[0].astype(BF16)
        wd_b[...] = wd_ref[0].astype(BF16)

    for s in range(FFN_TM // FFN_SUB):
        rows = slice(s * FFN_SUB, (s + 1) * FFN_SUB)

        @pl.when(s * FFN_SUB < nv)
        def _():
            x = load_x(rows)
            g = _dot(x, wg_b[...])
            u = _dot(x, wu_b[...])
            y = _dot((g * jax.nn.sigmoid(g) * u).astype(BF16), wd_b[...])

            @pl.when(j == 0)
            def _():
                o_ref[rows, :] = y

            @pl.when(j > 0)
            def _():
                o_ref[rows, :] += y

        @pl.when((s * FFN_SUB >= nv) & (j == 0))
        def _():
            o_ref[rows, :] = jnp.zeros((FFN_SUB, D_MODEL), F32)


def _ffn_kernel(te_ref, nv_ref, x_ref, wg_ref, wu_ref, wd_ref, o_ref, wg_b, wu_b, wd_b):
    del te_ref
    nv = nv_ref[pl.program_id(0)]
    _swiglu_step(lambda rows: x_ref[rows, :], nv, pl.program_id(1), wg_ref, wu_ref, wd_ref, o_ref, wg_b, wu_b, wd_b)


def _moe_ffn_kernel(te_ref, nv_ref, tok_ref, h_hbm, wg_ref, wu_ref, wd_ref, o_ref, x_buf, wg_b, wu_b, wd_b, sem):
    del te_ref
    i = pl.program_id(0)
    j = pl.program_id(1)
    nv = nv_ref[i]

    @pl.when((j == 0) & (nv > 0))
    def _():
        n_rows = jnp.left_shift(jnp.right_shift(nv + (FFN_SUB - 1), FFN_SUB_SHIFT), FFN_SUB_SHIFT)

        def row_copy(tok, r):
            return pltpu.make_async_copy(h_hbm.at[pl.ds(tok, 1)], x_buf.at[pl.ds(r, 1)], sem)

        def start(r, carry):
            row_copy(tok_ref[i * FFN_TM + r], r).start()
            return carry

        lax.fori_loop(0, n_rows, start, 0)

        def wait(r, carry):
            row_copy(0, 0).wait()
            return carry

        lax.fori_loop(0, n_rows, wait, 0)

    _swiglu_step(lambda rows: x_buf[rows, :].astype(BF16), nv, j, wg_ref, wu_ref, wd_ref, o_ref, wg_b, wu_b, wd_b)


def _ffn_specs(tf, n_prefetch):
    nff = D_FF // tf

    def jj(i, j, nv):
        return jnp.where(nv[i] > 0, j, nff - 1)

    w_in = pl.BlockSpec((1, D_MODEL, tf), lambda i, j, te, nv, *_: (te[i], 0, jj(i, j, nv)))
    w_out = pl.BlockSpec((1, tf, D_MODEL), lambda i, j, te, nv, *_: (te[i], jj(i, j, nv), 0))
    out = pl.BlockSpec((FFN_TM, D_MODEL), lambda i, j, *_: (i, 0))
    scratch = [pltpu.VMEM((D_MODEL, tf), BF16), pltpu.VMEM((D_MODEL, tf), BF16), pltpu.VMEM((tf, D_MODEL), BF16)]
    return nff, w_in, w_out, out, scratch


def _grouped_swiglu(tile_expert, tile_valid, x, wg, wu, wd, *, tf, name):
    r = x.shape[0]
    nff, w_in, w_out, out, scratch = _ffn_specs(tf, 2)
    grid_spec = pltpu.PrefetchScalarGridSpec(
        num_scalar_prefetch=2,
        grid=(r // FFN_TM, nff),
        in_specs=[pl.BlockSpec((FFN_TM, D_MODEL), lambda i, j, *_: (i, 0)), w_in, w_in, w_out],
        out_specs=out,
        scratch_shapes=scratch,
    )
    return pl.pallas_call(
        _ffn_kernel,
        out_shape=jax.ShapeDtypeStruct((r, D_MODEL), F32),
        grid_spec=grid_spec,
        compiler_params=_cparams(2, VMEM_LIMIT_BYTES),
        name=name,
    )(tile_expert, tile_valid, x, wg, wu, wd)


def _gathered_swiglu(tile_expert, tile_valid, row_token, h, wg, wu, wd, *, tf, name):
    r = row_token.shape[0]
    nff, w_in, w_out, out, scratch = _ffn_specs(tf, 3)
    grid_spec = pltpu.PrefetchScalarGridSpec(
        num_scalar_prefetch=3,
        grid=(r // FFN_TM, nff),
        in_specs=[pl.BlockSpec(memory_space=pl.ANY), w_in, w_in, w_out],
        out_specs=out,
        scratch_shapes=[pltpu.VMEM((FFN_TM, D_MODEL), F32)] + scratch + [pltpu.SemaphoreType.DMA(())],
    )
    return pl.pallas_call(
        _moe_ffn_kernel,
        out_shape=jax.ShapeDtypeStruct((r, D_MODEL), F32),
        grid_spec=grid_spec,
        compiler_params=_cparams(2, VMEM_LIMIT_BYTES),
        name=name,
    )(tile_expert, tile_valid, row_token, h, wg, wu, wd)


ROUTER_TM = 256
META_E0, META_E1, META_W0, META_W1, META_R0, META_R1 = range(6)


def _router_kernel(h_ref, r_ref, lt_ref, meta_ref, cnt_ref, run_scr):
    i = pl.program_id(0)

    @pl.when(i == 0)
    def _():
        run_scr[...] = jnp.zeros(run_scr.shape, F32)

    def split(x):
        hi = x.astype(BF16)
        return hi, (x - hi.astype(F32)).astype(BF16)

    h_hi, h_lo = split(h_ref[...])
    r_hi, r_lo = split(r_ref[...])
    logits = _dot(h_hi, r_hi) + (_dot(h_hi, r_lo) + _dot(h_lo, r_hi))
    lane = lax.broadcasted_iota(I32, (ROUTER_TM, LANES), 1).astype(F32)
    logits = jnp.where(lane < N_EXPERTS, logits, -jnp.inf)

    def top1(x):
        m = jnp.max(x, axis=1, keepdims=True)
        idx = jnp.min(jnp.where(x == m, lane, float(LANES)), axis=1, keepdims=True)
        return m, idx

    m0, e0 = top1(logits)
    m1, e1 = top1(jnp.where(lane == e0, -jnp.inf, logits))
    ex = jnp.exp(m1 - m0)
    w0 = 1.0 / (1.0 + ex)
    w1 = ex / (1.0 + ex)

    hot0 = jnp.where(lane == e0, 1.0, 0.0)
    hot1 = jnp.where(lane == e1, 1.0, 0.0)
    before0 = _dot(lt_ref[...], hot0.astype(BF16)) + run_scr[0:1, :]
    tot0 = jnp.sum(hot0, axis=0, keepdims=True)
    before1 = _dot(lt_ref[...], hot1.astype(BF16)) + run_scr[0:1, :] + tot0
    r0 = jnp.sum(hot0 * before0, axis=1, keepdims=True)
    r1 = jnp.sum(hot1 * before1, axis=1, keepdims=True)
    run_new = run_scr[0:1, :] + tot0 + jnp.sum(hot1, axis=0, keepdims=True)
    run_scr[...] = jnp.broadcast_to(run_new, run_scr.shape)
    cnt_ref[...] = jnp.broadcast_to(run_new, cnt_ref.shape)

    meta = jnp.zeros((ROUTER_TM, LANES), F32)
    for ln, val in ((META_E0, e0), (META_E1, e1), (META_W0, w0), (META_W1, w1),
                    (META_R0, r0), (META_R1, r1)):
        meta = jnp.where(lane == ln, val, meta)
    meta_ref[...] = meta


def _router(h, router_padded, lower_tri):
    return pl.pallas_call(
        _router_kernel,
        out_shape=(jax.ShapeDtypeStruct((TOKENS, LANES), F32), jax.ShapeDtypeStruct((8, LANES), F32)),
        grid=(TOKENS // ROUTER_TM,),
        in_specs=[pl.BlockSpec((ROUTER_TM, D_MODEL), lambda i: (i, 0)),
                  pl.BlockSpec((D_MODEL, LANES), lambda i: (0, 0)),
                  pl.BlockSpec((ROUTER_TM, ROUTER_TM), lambda i: (0, 0))],
        out_specs=(pl.BlockSpec((ROUTER_TM, LANES), lambda i: (i, 0)), pl.BlockSpec((8, LANES), lambda i: (0, 0))),
        scratch_shapes=[pltpu.VMEM((8, LANES), F32)],
        compiler_params=_cparams(1, VMEM_LIMIT_BYTES),
        name="moe_router",
    )(h, router_padded, lower_tri)


COMBINE_TM = 256


def _combine_kernel(rows_ref, y_hbm, h_ref, meta_ref, g_ref, b_ref, o_ref, ob_ref, buf, sem):
    i = pl.program_id(0)

    def row_copy(src, slot, r):
        return pltpu.make_async_copy(y_hbm.at[pl.ds(src, 1)], buf.at[slot, pl.ds(r, 1)], sem)

    def start(r, carry):
        tok = i * COMBINE_TM + r
        row_copy(rows_ref[tok], 0, r).start()
        row_copy(rows_ref[TOKENS + tok], 1, r).start()
        return carry

    lax.fori_loop(0, COMBINE_TM, start, 0)

    def wait(r, carry):
        row_copy(0, 0, 0).wait()
        row_copy(0, 1, 0).wait()
        return carry

    lax.fori_loop(0, COMBINE_TM, wait, 0)
    meta = meta_ref[...]
    f = meta[:, META_W0:META_W0 + 1] * buf[0] + meta[:, META_W1:META_W1 + 1] * buf[1]
    out = _layer_norm_rows(DN_ALPHA * h_ref[...] + f, g_ref[...], b_ref[...])
    o_ref[...] = out
    ob_ref[...] = out.astype(BF16)


def _combine(rows, y, h, meta, g, b, layer):
    grid_spec = pltpu.PrefetchScalarGridSpec(
        num_scalar_prefetch=1,
        grid=(TOKENS // COMBINE_TM,),
        in_specs=[pl.BlockSpec(memory_space=pl.ANY),
                  pl.BlockSpec((COMBINE_TM, D_MODEL), lambda i, rows: (i, 0)),
                  pl.BlockSpec((COMBINE_TM, LANES), lambda i, rows: (i, 0)),
                  pl.BlockSpec((None, 1, D_MODEL), lambda i, rows: (layer, 0, 0)),
                  pl.BlockSpec((None, 1, D_MODEL), lambda i, rows: (layer, 0, 0))],
        out_specs=(pl.BlockSpec((COMBINE_TM, D_MODEL), lambda i, rows: (i, 0)),
                   pl.BlockSpec((COMBINE_TM, D_MODEL), lambda i, rows: (i, 0))),
        scratch_shapes=[pltpu.VMEM((2, COMBINE_TM, D_MODEL), F32), pltpu.SemaphoreType.DMA(())],
    )
    return pl.pallas_call(
        _combine_kernel,
        out_shape=(jax.ShapeDtypeStruct((TOKENS, D_MODEL), F32), jax.ShapeDtypeStruct((TOKENS, D_MODEL), BF16)),
        grid_spec=grid_spec,
        compiler_params=_cparams(1, VMEM_LIMIT_BYTES),
        name="moe_combine",
    )(rows, y, h, meta, g, b)


def _pack_w_in(w_in):
    src = {}
    off = 0
    for name, width in IN_SPLITS:
        src[name] = (off, off + width)
        off += width
    lead = w_in.shape[:-1]

    def cols(name):
        a, b = src[name]
        return w_in[..., a:b]

    def zeros(width):
        return jnp.zeros(lead + (width,), w_in.dtype)

    pieces = []
    for name, width in _PACKED:
        if name == 'idx_k_lo':
            pieces += [cols('idx_k'), zeros(64)]
        elif name == 'idx_k_hi':
            pieces += [zeros(64), cols('idx_k')]
        elif name == 'misc':
            pieces += [cols('idx_w'), cols('nsa_g'), zeros(128 - 28)]
        elif name == 'mla_kr':
            pieces += [cols('mla_kr'), zeros(64)]
        else:
            pieces.append(cols(name))
    return jnp.concatenate(pieces, axis=-1).astype(BF16)


def _t5_bucket_np(dist):
    exact = REL_BUCKETS // 2
    d = np.maximum(dist, 0)
    log_ratio = np.log(np.maximum(d, 1).astype(np.float32) / exact) / math.log(REL_MAX_DIST / exact)
    far = np.minimum(exact + (log_ratio * (REL_BUCKETS - exact)).astype(np.int32), REL_BUCKETS - 1)
    return np.where(d < exact, d, far).astype(np.int32)


def _bias_slabs(table4):
    i = np.arange(Q_TILE)[:, None]
    j = np.arange(K_CHUNK)[None, :]
    bucket = jnp.asarray(np.stack([_t5_bucket_np(i - j - rel) for rel in (-256, -128, 0, -2 * SEQ)]))
    tab = table4.astype(F32)
    out = jnp.zeros((4, N_HEADS, Q_TILE, K_CHUNK), F32)
    for b in range(REL_BUCKETS):
        out = jnp.where((bucket == b)[:, None], tab[b][None, :, None, None], out)
    return out


def _rope_tables():
    half = QK_ROPE // 2
    inv = ROPE_BASE ** (-jnp.arange(half, dtype=F32) / half)
    ang = jnp.arange(SEQ, dtype=F32)[:, None] * inv[None, :]
    cos, sin = jnp.cos(ang), jnp.sin(ang)
    zero = jnp.zeros_like(cos)
    pad = jnp.zeros((SEQ, LANES - QK_ROPE), F32)
    cos_t = jnp.concatenate([cos, cos, pad], axis=1)
    nsin_t = jnp.concatenate([-sin, zero, pad], axis=1)
    psin_t = jnp.concatenate([zero, sin, pad], axis=1)
    return cos_t, nsin_t, psin_t


def _cover_expanded():
    n = np.arange(LANES)[:, None]
    s = np.arange(SEQ)[None, :]
    j = s // SLC_LEN
    cover = (CMP_STRIDE * n < SLC_LEN * j + SLC_LEN) & (CMP_STRIDE * n + CMP_LEN > SLC_LEN * j) & (n < N_CMP)
    return jnp.asarray(cover, BF16)


def _strict_upper(n):
    return jnp.asarray(np.arange(n)[:, None] > np.arange(n)[None, :], BF16)


def _strict_lower(n):
    return jnp.asarray(np.arange(n)[None, :] < np.arange(n)[:, None], BF16)


def _moe_layout(meta, counts):
    n_tiles = 2 * TOKENS // FFN_TM + N_EXPERTS
    cnt = counts[0, :N_EXPERTS].astype(I32)
    padded = ((cnt + FFN_TM - 1) // FFN_TM) * FFN_TM
    ends = jnp.cumsum(padded)
    offs = ends - padded
    e = meta[:, META_E0:META_E1 + 1].astype(I32)
    rank = meta[:, META_R0:META_R1 + 1].astype(I32)
    rows = (offs[e] + rank).T.reshape(-1)
    token = jnp.tile(jnp.arange(TOKENS, dtype=I32), 2)
    row_token = jnp.zeros((n_tiles * FFN_TM,), I32).at[rows].set(token)
    tile_start = jnp.arange(n_tiles, dtype=I32) * FFN_TM
    te = jnp.minimum(jnp.sum(tile_start[:, None] >= ends[None, :], axis=1), N_EXPERTS - 1).astype(I32)
    nv = jnp.clip(cnt[te] - (tile_start - offs[te]), 0, FFN_TM).astype(I32)
    return rows, row_token, te, nv


def kernel(x, mem, rel_table, w_in, mla_q_norm, mla_kv_norm, mla_w_uq, mla_w_uk, mla_w_uv, nsa_pe_k, nsa_pe_v,
           nsa_ck_w1, nsa_ck_w2, nsa_cv_w1, nsa_cv_w2, w_branch, w_out, ln1_g, ln1_b, xa_wq, xa_wk, xa_wv, xa_wo,
           ln2_g, ln2_b, ffn_w_gate, ffn_w_up, ffn_w_down, moe_router, moe_w_gate, moe_w_up, moe_w_down,
           ln3_g, ln3_b):
    h = x.reshape(TOKENS, D_MODEL)
    hb = h.astype(BF16)
    mem_b = mem.reshape(BATCH * MEM_LEN, D_MODEL).astype(BF16)

    w_in_p = _pack_w_in(w_in)
    dsa_slabs = _bias_slabs(rel_table[:, :N_HEADS])
    nsa_slabs = _bias_slabs(rel_table[:, N_HEADS:])
    cos_t, nsin_t, psin_t = _rope_tables()
    cover = _cover_expanded()
    upper = _strict_upper(SB_TILE)
    lower = _strict_lower(ROUTER_TM)

    uq = mla_w_uq.reshape(DEPTH, Q_LORA, N_HEADS, QK_NOPE + QK_ROPE)
    w_qn = uq[..., :QK_NOPE].reshape(DEPTH, Q_LORA, N_HEADS * QK_NOPE).astype(BF16)
    w_qp = jnp.concatenate([uq[..., QK_NOPE:], jnp.zeros((DEPTH, Q_LORA, N_HEADS, LANES - QK_ROPE), F32)],
                           axis=-1).reshape(DEPTH, Q_LORA, N_HEADS * LANES).astype(BF16)
    w_uk = mla_w_uk.astype(BF16)
    w_uv = mla_w_uv.astype(BF16)
    w_br = w_branch.astype(BF16)
    w_o = w_out.astype(BF16)
    xq, xk, xv, xo = (w.astype(BF16) for w in (xa_wq, xa_wk, xa_wv, xa_wo))
    router_p = jnp.pad(moe_router, ((0, 0), (0, 0), (0, LANES - N_EXPERTS)))
    row2 = lambda a: a.reshape(DEPTH, 1, -1)
    g1, b1, g2, b2, g3, b3 = (row2(a) for a in (ln1_g, ln1_b, ln2_g, ln2_b, ln3_g, ln3_b))
    qn, kvn = row2(mla_q_norm), row2(mla_kv_norm)
    pek = nsa_pe_k.reshape(DEPTH, 1, CMP_LEN * HEAD_DIM)
    pev = nsa_pe_v.reshape(DEPTH, 1, CMP_LEN * HEAD_DIM)
    dense_nv = jnp.full((TOKENS // FFN_TM,), FFN_TM, I32)
    moe_wg = moe_w_gate.reshape((-1,) + moe_w_gate.shape[2:])
    moe_wu = moe_w_up.reshape((-1,) + moe_w_up.shape[2:])
    moe_wd = moe_w_down.reshape((-1,) + moe_w_down.shape[2:])

    for layer in range(DEPTH):
        z = _matmul(hb, w_in_p, layer, tm=1024, tn=1024, out_dtype=BF16, name="in_proj")
        o_dsa = _dsa(z, dsa_slabs)
        o_sb = _stick_breaking(z, upper)
        group = lambda name: z[:, _OFF[name]:_OFF[name] + HEAD_DIM].reshape(BATCH * LANES, CMP_STRIDE * HEAD_DIM)
        k_cmp, v_cmp = _nsa_compress(group('nsa_kc'), group('nsa_vc'), pek[layer], pev[layer],
                                     nsa_ck_w1[layer], nsa_ck_w2[layer], nsa_cv_w1[layer], nsa_cv_w2[layer])
        o_nsa = _nsa(z, k_cmp, v_cmp, cover, nsa_slabs)
        q_cat, k_cat, v_mla = _mla_prep(z, qn[layer], kvn[layer], w_qn[layer], w_qp[layer], w_uk[layer],
                                        w_uv[layer], cos_t, nsin_t, psin_t, tm=512)
        o_mla = _flash(q_cat, k_cat, v_mla, dk=MLA_QK, kv_len=SEQ, scale=(QK_NOPE + QK_ROPE) ** -0.5,
                       causal=True, name="mla_attention")
        mixed = _merge((o_dsa, o_sb, o_nsa, o_mla), z, w_br, layer, tm=512, tn=512)
        h, hb = _matmul_res_ln(mixed, w_o, h, g1, b1, layer, tm=256, name="out_proj_ln")

        q_x = _matmul(hb, xq, layer, tm=1024, tn=512, out_dtype=BF16, name="xa_q_proj")
        k_x = _matmul(mem_b, xk, layer, tm=1024, tn=512, out_dtype=BF16, name="xa_k_proj")
        v_x = _matmul(mem_b, xv, layer, tm=1024, tn=512, out_dtype=BF16, name="xa_v_proj")
        o_x = _flash(q_x, k_x, v_x, dk=HEAD_DIM, kv_len=MEM_LEN, scale=HEAD_DIM ** -0.5, causal=False,
                     name="cross_attention")
        h, hb = _matmul_res_ln(o_x, xo, h, g2, b2, layer, tm=256, name="xa_out_proj_ln")

        i = layer // 2
        if layer % 2 == 0:
            dense_te = jnp.full((TOKENS // FFN_TM,), i, I32)
            y = _grouped_swiglu(dense_te, dense_nv, hb, ffn_w_gate, ffn_w_up, ffn_w_down, tf=256,
                                name="dense_swiglu")
            h, hb = _res_ln(y, h, g3, b3, layer, tm=256, name="ffn_res_ln")
        else:
            meta, counts = _router(h, router_p[i], lower)
            rows, row_token, te, nv = _moe_layout(meta, counts)
            y = _gathered_swiglu(te + i * N_EXPERTS, nv, row_token, h, moe_wg, moe_wu, moe_wd, tf=256,
                                 name="moe_swiglu")
            h, hb = _combine(rows, y, h, meta, g3, b3, layer)
    return h.reshape(BATCH, SEQ, D_MODEL)
```

```python
import functools
import math

import jax
import jax.numpy as jnp
import numpy as np
from jax import lax
from jax.experimental import pallas as pl
from jax.experimental.pallas import tpu as pltpu

F32 = jnp.float32
BF16 = jnp.bfloat16
I32 = jnp.int32

D_MODEL = 2048
BATCH = 4
SEQ = 2048
DEPTH = 4
TOKENS = BATCH * SEQ
MEM_LEN = 256
HEAD_DIM = 128
N_HEADS = 4
DSA_TOPK = min(256, SEQ // 4)
IDX_HEADS = 16
IDX_DIM = 64
CMP_LEN = 32
CMP_STRIDE = 16
CMP_HIDDEN = 256
N_CMP = (SEQ - CMP_LEN) // CMP_STRIDE + 1
SLC_LEN = 64
SLC_SHIFT = 6
N_SLC = SEQ // SLC_LEN
N_SEL = min(16, N_SLC)
WINDOW = 512
FORCE_SCORE = 1.0e4
Q_LORA = 512
KV_LORA = 512
QK_NOPE = 128
QK_ROPE = 64
ROPE_BASE = 10000.0
N_MIXERS = 4
MIX_WIDTH = 512
REL_BUCKETS = 32
REL_MAX_DIST = 128
D_FF = 5632
N_EXPERTS = 8
DN_ALPHA = (2 * DEPTH) ** 0.25
LN_EPS = 1e-5
RMS_EPS = 1e-6
NEG_BIG = -1.0e30

IN_SPLITS = (
    ('dsa_q', 512), ('dsa_k', 128), ('dsa_v', 128),
    ('idx_q', IDX_HEADS * IDX_DIM), ('idx_k', IDX_DIM), ('idx_w', IDX_HEADS),
    ('sb_q', 512), ('sb_k', 512), ('sb_v', 512),
    ('nsa_q', 512),
    ('nsa_kc', 128), ('nsa_vc', 128), ('nsa_ks', 128), ('nsa_vs', 128),
    ('nsa_kw', 128), ('nsa_vw', 128), ('nsa_g', 12),
    ('mla_cq', Q_LORA), ('mla_ckv', KV_LORA), ('mla_kr', QK_ROPE),
    ('gates', N_MIXERS * D_MODEL),
)

LANES = 128
Q_TILE = 128
K_CHUNK = 256
N_CHUNKS = SEQ // K_CHUNK
VMEM_LIMIT_BYTES = 56 * 1024 * 1024
INT_MIN = -2147483648
KEY_NEG_INF = -2139095041

_PACKED = (
    ('idx_q', 1024), ('dsa_q', 512), ('sb_q', 512), ('sb_k', 512), ('sb_v', 512), ('nsa_q', 512),
    ('mla_cq', 512), ('mla_ckv', 512),
    ('dsa_k', 128), ('dsa_v', 128), ('idx_k_lo', 128), ('idx_k_hi', 128), ('misc', 128),
    ('nsa_kc', 128), ('nsa_vc', 128), ('nsa_ks', 128), ('nsa_vs', 128), ('nsa_kw', 128), ('nsa_vw', 128),
    ('mla_kr', 128), ('gates', 8192),
)
_OFF = {}
_o = 0
for _n, _w in _PACKED:
    _OFF[_n] = _o
    _o += _w
Z_WIDTH = _o
MISC_IDXW = 0
MISC_NSAG = 16


def _cparams(n_axes, vmem=None):
    return pltpu.CompilerParams(dimension_semantics=("arbitrary",) * n_axes, vmem_limit_bytes=vmem)


def _dot(a, b):
    return jnp.dot(a, b, preferred_element_type=F32)


def _dot_nt(a, b):
    return lax.dot_general(a, b, (((1,), (1,)), ((), ())), preferred_element_type=F32)


def _layer_norm_rows(v, g, b):
    mu = jnp.mean(v, axis=-1, keepdims=True)
    d = v - mu
    var = jnp.mean(d * d, axis=-1, keepdims=True)
    return d * lax.rsqrt(var + LN_EPS) * g + b


def _matmul_kernel(x_ref, w_ref, o_ref):
    o_ref[...] = _dot(x_ref[...].astype(BF16), w_ref[...].astype(BF16)).astype(o_ref.dtype)


def _matmul(x, w, layer, *, tm, tn, out_dtype, name):
    m, k = x.shape
    n = w.shape[2]
    return pl.pallas_call(
        _matmul_kernel,
        out_shape=jax.ShapeDtypeStruct((m, n), out_dtype),
        grid=(n // tn, m // tm),
        in_specs=[pl.BlockSpec((tm, k), lambda j, i: (i, 0)),
                  pl.BlockSpec((None, k, tn), lambda j, i: (layer, 0, j))],
        out_specs=pl.BlockSpec((tm, tn), lambda j, i: (i, j)),
        compiler_params=_cparams(2, VMEM_LIMIT_BYTES),
        name=name,
    )(x, w)


def _matmul_res_ln_kernel(x_ref, w_ref, h_ref, g_ref, b_ref, o_ref, ob_ref):
    y = _dot(x_ref[...], w_ref[...])
    out = _layer_norm_rows(DN_ALPHA * h_ref[...] + y, g_ref[...], b_ref[...])
    o_ref[...] = out
    ob_ref[...] = out.astype(BF16)


def _matmul_res_ln(x, w, h, g, b, layer, *, tm, name):
    m, k = x.shape
    d = w.shape[2]
    return pl.pallas_call(
        _matmul_res_ln_kernel,
        out_shape=(jax.ShapeDtypeStruct((m, d), F32), jax.ShapeDtypeStruct((m, d), BF16)),
        grid=(m // tm,),
        in_specs=[pl.BlockSpec((tm, k), lambda i: (i, 0)),
                  pl.BlockSpec((None, k, d), lambda i: (layer, 0, 0)),
                  pl.BlockSpec((tm, d), lambda i: (i, 0)),
                  pl.BlockSpec((None, 1, d), lambda i: (layer, 0, 0)),
                  pl.BlockSpec((None, 1, d), lambda i: (layer, 0, 0))],
        out_specs=(pl.BlockSpec((tm, d), lambda i: (i, 0)), pl.BlockSpec((tm, d), lambda i: (i, 0))),
        compiler_params=_cparams(1, VMEM_LIMIT_BYTES),
        name=name,
    )(x, w, h, g, b)


def _res_ln_kernel(y_ref, h_ref, g_ref, b_ref, o_ref, ob_ref):
    out = _layer_norm_rows(DN_ALPHA * h_ref[...] + y_ref[...], g_ref[...], b_ref[...])
    o_ref[...] = out
    ob_ref[...] = out.astype(BF16)


def _res_ln(y, h, g, b, layer, *, tm, name):
    m, d = h.shape
    return pl.pallas_call(
        _res_ln_kernel,
        out_shape=(jax.ShapeDtypeStruct((m, d), F32), jax.ShapeDtypeStruct((m, d), BF16)),
        grid=(m // tm,),
        in_specs=[pl.BlockSpec((tm, d), lambda i: (i, 0)),
                  pl.BlockSpec((tm, d), lambda i: (i, 0)),
                  pl.BlockSpec((None, 1, d), lambda i: (layer, 0, 0)),
                  pl.BlockSpec((None, 1, d), lambda i: (layer, 0, 0))],
        out_specs=(pl.BlockSpec((tm, d), lambda i: (i, 0)), pl.BlockSpec((tm, d), lambda i: (i, 0))),
        compiler_params=_cparams(1, VMEM_LIMIT_BYTES),
        name=name,
    )(y, h, g, b)


def _merge_kernel(b0, b1, b2, b3, g0, g1, g2, g3, wb_ref, o_ref):
    acc = None
    for n, (br, gr) in enumerate(((b0, g0), (b1, g1), (b2, g2), (b3, g3))):
        y = _dot(br[...], wb_ref[n])
        gy = jax.nn.sigmoid(gr[...].astype(F32)) * y
        acc = gy if acc is None else acc + gy
    o_ref[...] = acc.astype(o_ref.dtype)


def _merge(branches, z, wb, layer, *, tm, tn):
    m = z.shape[0]
    gate_specs = []
    for n in range(N_MIXERS):
        base = (_OFF['gates'] + n * D_MODEL) // tn
        gate_specs.append(pl.BlockSpec((tm, tn), lambda j, i, base=base: (i, base + j)))
    return pl.pallas_call(
        _merge_kernel,
        out_shape=jax.ShapeDtypeStruct((m, D_MODEL), BF16),
        grid=(D_MODEL // tn, m // tm),
        in_specs=[pl.BlockSpec((tm, MIX_WIDTH), lambda j, i: (i, 0))] * N_MIXERS + gate_specs
        + [pl.BlockSpec((None, N_MIXERS, MIX_WIDTH, tn), lambda j, i: (layer, 0, 0, j))],
        out_specs=pl.BlockSpec((tm, tn), lambda j, i: (i, j)),
        compiler_params=_cparams(2, VMEM_LIMIT_BYTES),
        name="branch_merge",
    )(*branches, z, z, z, z, wb)


def _sortable_key(score):
    score = jnp.where(score == 0.0, 0.0, score)
    bits = pltpu.bitcast(score, I32)
    return bits ^ (jnp.right_shift(bits, 31) & 0x7FFFFFFF)


def _kth_largest_key(key_scr, k):
    def body(i, lo_u):
        cand_u = lo_u | jnp.left_shift(jnp.int32(1), 31 - i)
        cand_s = cand_u ^ INT_MIN
        cnt = jnp.zeros((Q_TILE, K_CHUNK), F32)
        for c in range(N_CHUNKS):
            cnt = cnt + jnp.where(key_scr[c] >= cand_s, 1.0, 0.0)
        tot = jnp.sum(cnt, axis=1, keepdims=True)
        return jnp.where(tot >= float(k), cand_u, lo_u)

    lo_u = lax.fori_loop(0, 32, body, jnp.zeros((Q_TILE, 1), I32))
    return lo_u ^ INT_MIN


def _slab_index(rel):
    return jnp.where(rel == 0, 2, jnp.where(rel == -128, 1, jnp.where(rel == -256, 0, 3)))


def _stack_heads(q):
    return jnp.concatenate([q[:, h * HEAD_DIM:(h + 1) * HEAD_DIM] for h in range(N_HEADS)], axis=0)


def _mqa_masked_attention(qs, k_ref, v_ref, slab_ref, t0, chunks, ok_fn, s_scr, p_scr):
    scale = HEAD_DIM ** -0.5
    width = len(chunks) * K_CHUNK
    values = []
    for n, c in enumerate(chunks):
        cols = slice(n * K_CHUNK, (n + 1) * K_CHUNK)
        start = c * K_CHUNK if isinstance(c, int) else pl.multiple_of(jnp.maximum(c, 0) * K_CHUNK, K_CHUNK)
        madd = jnp.where(ok_fn(c), 0.0, NEG_BIG)
        s4 = _dot_nt(qs, k_ref[pl.ds(start, K_CHUNK), :]) * scale
        values.append(v_ref[pl.ds(start, K_CHUNK), :])
        slab = _slab_index(c * K_CHUNK - t0)
        for h in range(N_HEADS):
            s_scr[h, :, cols] = s4[h * Q_TILE:(h + 1) * Q_TILE] + slab_ref[slab, h] + madd
    norms = []
    for h in range(N_HEADS):
        s = s_scr[h, :, :width]
        p = jnp.exp(s - jnp.max(s, axis=1, keepdims=True))
        norms.append(jnp.sum(p, axis=1, keepdims=True))
        p_scr[h * Q_TILE:(h + 1) * Q_TILE, :width] = p.astype(BF16)
    o4 = _dot(p_scr[:, :width], jnp.concatenate(values, axis=0))
    return [o4[h * Q_TILE:(h + 1) * Q_TILE] / norms[h] for h in range(N_HEADS)]


def _row_pos(t0):
    return t0 + lax.broadcasted_iota(I32, (Q_TILE, K_CHUNK), 0)


def _col_pos(c):
    return c * K_CHUNK + lax.broadcasted_iota(I32, (Q_TILE, K_CHUNK), 1)


_ALL_CHUNKS = list(range(N_CHUNKS))
_ATTN_SCRATCH = [pltpu.VMEM((N_CHUNKS, Q_TILE, K_CHUNK), I32),
                 pltpu.VMEM((N_HEADS, Q_TILE, SEQ), F32),
                 pltpu.VMEM((N_HEADS * Q_TILE, SEQ), BF16)]
_SLAB_SPEC = pl.BlockSpec((4, N_HEADS, Q_TILE, K_CHUNK), lambda b, i: (0, 0, 0, 0))


def _dsa_kernel(q_ref, iq_ref, misc_ref, k_ref, v_ref, iklo_ref, ikhi_ref, slab_ref, o_ref,
                key_scr, s_scr, p_scr):
    qb = pl.program_id(1)
    t0 = qb * Q_TILE
    n_chunks = qb // 2 + 1
    misc = misc_ref[...].astype(F32)
    row = t0 + lax.broadcasted_iota(I32, (Q_TILE, K_CHUNK), 0)

    key_scr[...] = jnp.full(key_scr.shape, KEY_NEG_INF, I32)

    def score_body(c, carry):
        start = pl.multiple_of(c * K_CHUNK, K_CHUNK)
        ik_lo = iklo_ref[pl.ds(start, K_CHUNK), :]
        ik_hi = ikhi_ref[pl.ds(start, K_CHUNK), :]
        acc = jnp.zeros((Q_TILE, K_CHUNK), F32)
        for p in range(IDX_HEADS // 2):
            pair = iq_ref[:, p * LANES:(p + 1) * LANES]
            w0 = misc[:, MISC_IDXW + 2 * p:MISC_IDXW + 2 * p + 1]
            w1 = misc[:, MISC_IDXW + 2 * p + 1:MISC_IDXW + 2 * p + 2]
            acc = acc + jnp.maximum(_dot_nt(pair, ik_lo), 0.0) * w0
            acc = acc + jnp.maximum(_dot_nt(pair, ik_hi), 0.0) * w1
        col = c * K_CHUNK + lax.broadcasted_iota(I32, (Q_TILE, K_CHUNK), 1)
        score = jnp.where(col <= row, acc, -jnp.inf)
        key_scr[c] = _sortable_key(score)
        return carry

    lax.fori_loop(0, n_chunks, score_body, 0)
    thr = _kth_largest_key(key_scr, DSA_TOPK)

    def ok_fn(c):
        return (key_scr[c] >= thr) & (_col_pos(c) <= row)

    o = _mqa_masked_attention(_stack_heads(q_ref[...]), k_ref, v_ref, slab_ref, t0, _ALL_CHUNKS, ok_fn, s_scr, p_scr)
    o_ref[...] = jnp.concatenate(o, axis=1).astype(o_ref.dtype)


def _zspec_rows(name, width, rows, index_fn):
    cb, rem = divmod(_OFF[name], width)
    assert rem == 0, name
    return pl.BlockSpec((rows, width), functools.partial(index_fn, cb))


def _dsa(z, slabs):
    nq = SEQ // Q_TILE

    def qidx(cb, b, i):
        return (b * nq + i, cb)

    def kidx(cb, b, i):
        return (b, cb)

    return pl.pallas_call(
        _dsa_kernel,
        out_shape=jax.ShapeDtypeStruct((TOKENS, MIX_WIDTH), BF16),
        grid=(BATCH, nq),
        in_specs=[_zspec_rows('dsa_q', 512, Q_TILE, qidx),
                  _zspec_rows('idx_q', 1024, Q_TILE, qidx),
                  _zspec_rows('misc', 128, Q_TILE, qidx),
                  _zspec_rows('dsa_k', 128, SEQ, kidx),
                  _zspec_rows('dsa_v', 128, SEQ, kidx),
                  _zspec_rows('idx_k_lo', 128, SEQ, kidx),
                  _zspec_rows('idx_k_hi', 128, SEQ, kidx),
                  _SLAB_SPEC],
        out_specs=pl.BlockSpec((Q_TILE, MIX_WIDTH), lambda b, i: (b * nq + i, 0)),
        scratch_shapes=_ATTN_SCRATCH,
        compiler_params=_cparams(2, VMEM_LIMIT_BYTES),
        name="dsa_attention",
    )(z, z, z, z, z, z, z, slabs)


def _nsa_compress_kernel(xk_ref, xv_ref, pek_ref, pev_ref, kw1_ref, kw2_ref, vw1_ref, vw2_ref, ok_ref, ov_ref):
    half = CMP_STRIDE * HEAD_DIM

    def compress(x_ref, pe_ref, w1_ref, w2_ref):
        x = x_ref[...].astype(F32)
        a = _dot((x + pe_ref[:, :half]).astype(BF16), w1_ref[:half, :].astype(BF16))
        b = _dot((x + pe_ref[:, half:]).astype(BF16), w1_ref[half:, :].astype(BF16))
        hid = a + pltpu.roll(b, b.shape[0] - 1, 0)
        return _dot(jax.nn.gelu(hid).astype(BF16), w2_ref[...].astype(BF16))

    ok_ref[...] = compress(xk_ref, pek_ref, kw1_ref, kw2_ref)
    ov_ref[...] = compress(xv_ref, pev_ref, vw1_ref, vw2_ref)


def _nsa_compress(xk, xv, pek, pev, kw1, kw2, vw1, vw2):
    rows = xk.shape[0]
    full = lambda a: pl.BlockSpec(a.shape, lambda i: (0,) * a.ndim)
    args = (xk, xv, pek, pev, kw1, kw2, vw1, vw2)
    return pl.pallas_call(
        _nsa_compress_kernel,
        out_shape=(jax.ShapeDtypeStruct((rows, HEAD_DIM), F32), jax.ShapeDtypeStruct((rows, HEAD_DIM), F32)),
        grid=(1,),
        in_specs=[full(a) for a in args],
        out_specs=(pl.BlockSpec((rows, HEAD_DIM), lambda i: (0, 0)), pl.BlockSpec((rows, HEAD_DIM), lambda i: (0, 0))),
        compiler_params=_cparams(1, VMEM_LIMIT_BYTES),
        name="nsa_compress",
    )(*args)


def _nsa_kernel(q_ref, misc_ref, kc_ref, vc_ref, ks_ref, vs_ref, kw_ref, vw_ref, cov_ref, slab_ref, o_ref,
                key_scr, s_scr, p_scr):
    qb = pl.program_id(1)
    t0 = qb * Q_TILE
    scale = HEAD_DIM ** -0.5
    qs = _stack_heads(q_ref[...])
    misc = misc_ref[...].astype(F32)

    n_idx = lax.broadcasted_iota(I32, (N_HEADS * Q_TILE, LANES), 1)
    t_idx = t0 + (lax.broadcasted_iota(I32, (N_HEADS * Q_TILE, LANES), 0) & (Q_TILE - 1))
    cmp_ok = (CMP_STRIDE * n_idx + CMP_LEN - 1) <= t_idx
    lc = jnp.where(cmp_ok, _dot_nt(qs, kc_ref[...].astype(BF16)) * scale, NEG_BIG)
    e = jnp.exp(lc - jnp.max(lc, axis=1, keepdims=True))
    p_cmp = jnp.where(cmp_ok, e / jnp.sum(e, axis=1, keepdims=True), 0.0)
    o_cmp = _dot(p_cmp.astype(BF16), vc_ref[...].astype(BF16))

    p_sum = p_cmp[0:Q_TILE] + p_cmp[Q_TILE:2 * Q_TILE] + p_cmp[2 * Q_TILE:3 * Q_TILE] + p_cmp[3 * Q_TILE:]
    p_hi = p_sum.astype(BF16)
    p_lo = (p_sum - p_hi.astype(F32)).astype(BF16)
    cur = jnp.right_shift(t0 + lax.broadcasted_iota(I32, (Q_TILE, K_CHUNK), 0), SLC_SHIFT)
    for c in range(N_CHUNKS):
        cov = cov_ref[:, c * K_CHUNK:(c + 1) * K_CHUNK]
        imp = _dot(p_hi, cov) + _dot(p_lo, cov)
        jb = jnp.right_shift(c * K_CHUNK + lax.broadcasted_iota(I32, (Q_TILE, K_CHUNK), 1), SLC_SHIFT)
        forced = (jb == 0) | (jb == cur) | (jb == cur - 1)
        imp = jnp.where(jb <= cur, imp + jnp.where(forced, FORCE_SCORE, 0.0), -jnp.inf)
        key_scr[c] = _sortable_key(imp)
    thr = _kth_largest_key(key_scr, N_SEL * SLC_LEN)
    row = _row_pos(t0)

    def sel_ok(c):
        return (key_scr[c] >= thr) & (_col_pos(c) <= row)

    o_slc = _mqa_masked_attention(qs, ks_ref, vs_ref, slab_ref, t0, _ALL_CHUNKS, sel_ok, s_scr, p_scr)

    def win_ok(c):
        col = _col_pos(c)
        dist = row - col
        return (dist >= 0) & (dist < WINDOW) & (col >= 0)

    win_chunks = [qb // 2 - 2 + n for n in range(3)]
    o_win = _mqa_masked_attention(qs, kw_ref, vw_ref, slab_ref, t0, win_chunks, win_ok, s_scr, p_scr)

    outs = []
    for h in range(N_HEADS):
        g = jax.nn.sigmoid(misc[:, MISC_NSAG + 3 * h:MISC_NSAG + 3 * h + 3])
        rows = slice(h * Q_TILE, (h + 1) * Q_TILE)
        outs.append(g[:, 0:1] * o_cmp[rows] + g[:, 1:2] * o_slc[h] + g[:, 2:3] * o_win[h])
    o_ref[...] = jnp.concatenate(outs, axis=1).astype(o_ref.dtype)


def _nsa(z, k_cmp, v_cmp, cov, slabs):
    nq = SEQ // Q_TILE

    def qidx(cb, b, i):
        return (b * nq + i, cb)

    def kidx(cb, b, i):
        return (b, cb)

    cmp_spec = pl.BlockSpec((LANES, HEAD_DIM), lambda b, i: (b, 0))
    return pl.pallas_call(
        _nsa_kernel,
        out_shape=jax.ShapeDtypeStruct((TOKENS, MIX_WIDTH), BF16),
        grid=(BATCH, nq),
        in_specs=[_zspec_rows('nsa_q', 512, Q_TILE, qidx),
                  _zspec_rows('misc', 128, Q_TILE, qidx),
                  cmp_spec, cmp_spec,
                  _zspec_rows('nsa_ks', 128, SEQ, kidx),
                  _zspec_rows('nsa_vs', 128, SEQ, kidx),
                  _zspec_rows('nsa_kw', 128, SEQ, kidx),
                  _zspec_rows('nsa_vw', 128, SEQ, kidx),
                  pl.BlockSpec((LANES, SEQ), lambda b, i: (0, 0)),
                  _SLAB_SPEC],
        out_specs=pl.BlockSpec((Q_TILE, MIX_WIDTH), lambda b, i: (b * nq + i, 0)),
        scratch_shapes=_ATTN_SCRATCH,
        compiler_params=_cparams(2, VMEM_LIMIT_BYTES),
        name="nsa_attention",
    )(z, z, k_cmp, v_cmp, z, z, z, z, cov, slabs)


SB_TILE = 256


def _sb_kernel(q_ref, k_ref, v_ref, u_ref, o_ref):
    qb = pl.program_id(2)
    scale = HEAD_DIM ** -0.5
    q = q_ref[...]
    upper = u_ref[...]
    row = qb * SB_TILE + lax.broadcasted_iota(I32, (SB_TILE, SB_TILE), 0)
    lane = lax.broadcasted_iota(I32, (SB_TILE, SB_TILE), 1)

    def body(i, carry):
        later, acc = carry
        c = qb - i
        start = pl.multiple_of(c * SB_TILE, SB_TILE)
        k = k_ref[pl.ds(start, SB_TILE), :]
        v = v_ref[pl.ds(start, SB_TILE), :]
        zl = _dot_nt(q, k) * scale
        strict = (c * SB_TILE + lane) < row
        log_beta = jnp.minimum(zl, 0.0) - jnp.log1p(jnp.exp(-jnp.abs(zl)))
        log_keep = jnp.where(strict, log_beta - zl, 0.0)
        keep_hi = log_keep.astype(BF16)
        keep_lo = (log_keep - keep_hi.astype(F32)).astype(BF16)
        within = _dot(keep_hi, upper) + _dot(keep_lo, upper)
        a = jnp.where(strict, jnp.exp(log_beta + within + later), 0.0)
        acc = acc + _dot(a.astype(BF16), v)
        later = later + jnp.sum(log_keep, axis=1, keepdims=True)
        return later, acc

    _, acc = lax.fori_loop(0, qb + 1, body,
                           (jnp.zeros((SB_TILE, 1), F32), jnp.zeros((SB_TILE, HEAD_DIM), F32)))
    o_ref[...] = acc.astype(o_ref.dtype)


def _stick_breaking(z, upper):
    nq = SEQ // SB_TILE
    qcb, kcb, vcb = _OFF['sb_q'] // HEAD_DIM, _OFF['sb_k'] // HEAD_DIM, _OFF['sb_v'] // HEAD_DIM
    return pl.pallas_call(
        _sb_kernel,
        out_shape=jax.ShapeDtypeStruct((TOKENS, MIX_WIDTH), BF16),
        grid=(BATCH, N_HEADS, nq),
        in_specs=[pl.BlockSpec((SB_TILE, HEAD_DIM), lambda b, h, i: (b * nq + i, qcb + h)),
                  pl.BlockSpec((SEQ, HEAD_DIM), lambda b, h, i: (b, kcb + h)),
                  pl.BlockSpec((SEQ, HEAD_DIM), lambda b, h, i: (b, vcb + h)),
                  pl.BlockSpec((SB_TILE, SB_TILE), lambda b, h, i: (0, 0))],
        out_specs=pl.BlockSpec((SB_TILE, HEAD_DIM), lambda b, h, i: (b * nq + i, h)),
        compiler_params=_cparams(3, VMEM_LIMIT_BYTES),
        name="stick_breaking_attention",
    )(z, z, z, upper)


MLA_QK = 2 * HEAD_DIM


def _rope128(x, cos, nsin, psin):
    return x * cos + pltpu.roll(x, 96, 1) * nsin + pltpu.roll(x, 32, 1) * psin


def _mla_prep_kernel(cq_ref, ckv_ref, kr_ref, qn_ref, kvn_ref, wqn_ref, wqp_ref, wuk_ref, wuv_ref,
                     cos_ref, nsin_ref, psin_ref, q_ref, k_ref, v_ref):
    def rms(x, g):
        return (x * lax.rsqrt(jnp.mean(x * x, axis=-1, keepdims=True) + RMS_EPS) * g).astype(BF16)

    cos, nsin, psin = cos_ref[...], nsin_ref[...], psin_ref[...]
    xq = rms(cq_ref[...].astype(F32), qn_ref[...])
    xc = rms(ckv_ref[...].astype(F32), kvn_ref[...])
    q_nope = _dot(xq, wqn_ref[...])
    q_rope = _dot(xq, wqp_ref[...])
    k_nope = _dot(xc, wuk_ref[...])
    k_rope = _rope128(kr_ref[...].astype(F32), cos, nsin, psin)
    q_parts, k_parts = [], []
    for h in range(N_HEADS):
        cols = slice(h * HEAD_DIM, (h + 1) * HEAD_DIM)
        q_parts += [q_nope[:, cols], _rope128(q_rope[:, cols], cos, nsin, psin)]
        k_parts += [k_nope[:, cols], k_rope]
    q_ref[...] = jnp.concatenate(q_parts, axis=1).astype(BF16)
    k_ref[...] = jnp.concatenate(k_parts, axis=1).astype(BF16)
    v_ref[...] = _dot(xc, wuv_ref[...]).astype(BF16)


def _mla_prep(z, qn, kvn, wqn, wqp, wuk, wuv, cos, nsin, psin, *, tm):
    nt = SEQ // tm

    def zidx(cb, i):
        return (i, cb)

    wspec = pl.BlockSpec((Q_LORA, 512), lambda i: (0, 0))
    nspec = pl.BlockSpec((1, 512), lambda i: (0, 0))
    tspec = pl.BlockSpec((tm, LANES), lambda i: (i % nt, 0))
    return pl.pallas_call(
        _mla_prep_kernel,
        out_shape=(jax.ShapeDtypeStruct((TOKENS, N_HEADS * MLA_QK), BF16),
                   jax.ShapeDtypeStruct((TOKENS, N_HEADS * MLA_QK), BF16),
                   jax.ShapeDtypeStruct((TOKENS, N_HEADS * HEAD_DIM), BF16)),
        grid=(TOKENS // tm,),
        in_specs=[_zspec_rows('mla_cq', 512, tm, zidx), _zspec_rows('mla_ckv', 512, tm, zidx),
                  _zspec_rows('mla_kr', 128, tm, zidx), nspec, nspec, wspec, wspec, wspec, wspec,
                  tspec, tspec, tspec],
        out_specs=(pl.BlockSpec((tm, N_HEADS * MLA_QK), lambda i: (i, 0)),
                   pl.BlockSpec((tm, N_HEADS * MLA_QK), lambda i: (i, 0)),
                   pl.BlockSpec((tm, N_HEADS * HEAD_DIM), lambda i: (i, 0))),
        compiler_params=_cparams(1, VMEM_LIMIT_BYTES),
        name="mla_prep",
    )(z, z, z, qn, kvn, wqn, wqp, wuk, wuv, cos, nsin, psin)


FLASH_TILE = 256


def _flash_kernel(q_ref, k_ref, v_ref, o_ref, *, scale, causal, n_kv):
    qb = pl.program_id(2)
    q = q_ref[...]
    row = qb * FLASH_TILE + lax.broadcasted_iota(I32, (FLASH_TILE, FLASH_TILE), 0)
    lane = lax.broadcasted_iota(I32, (FLASH_TILE, FLASH_TILE), 1)

    def body(c, carry):
        m, l, acc = carry
        start = pl.multiple_of(c * FLASH_TILE, FLASH_TILE)
        k = k_ref[pl.ds(start, FLASH_TILE), :]
        v = v_ref[pl.ds(start, FLASH_TILE), :]
        s = _dot_nt(q, k) * scale
        if causal:
            s = jnp.where(c * FLASH_TILE + lane <= row, s, NEG_BIG)
        m_new = jnp.maximum(m, jnp.max(s, axis=1, keepdims=True))
        p = jnp.exp(s - m_new)
        alpha = jnp.exp(m - m_new)
        return m_new, alpha * l + jnp.sum(p, axis=1, keepdims=True), alpha * acc + _dot(p.astype(BF16), v)

    init = (jnp.full((FLASH_TILE, 1), NEG_BIG, F32), jnp.zeros((FLASH_TILE, 1), F32),
            jnp.zeros((FLASH_TILE, HEAD_DIM), F32))
    _, l, acc = lax.fori_loop(0, qb + 1 if causal else n_kv, body, init)
    o_ref[...] = (acc / l).astype(o_ref.dtype)


def _flash(q, k, v, *, dk, kv_len, scale, causal, name):
    nq = SEQ // FLASH_TILE
    return pl.pallas_call(
        functools.partial(_flash_kernel, scale=scale, causal=causal, n_kv=kv_len // FLASH_TILE),
        out_shape=jax.ShapeDtypeStruct((TOKENS, N_HEADS * HEAD_DIM), BF16),
        grid=(BATCH, N_HEADS, nq),
        in_specs=[pl.BlockSpec((FLASH_TILE, dk), lambda b, h, i: (b * nq + i, h)),
                  pl.BlockSpec((kv_len, dk), lambda b, h, i: (b, h)),
                  pl.BlockSpec((kv_len, HEAD_DIM), lambda b, h, i: (b, h))],
        out_specs=pl.BlockSpec((FLASH_TILE, HEAD_DIM), lambda b, h, i: (b * nq + i, h)),
        compiler_params=_cparams(3, VMEM_LIMIT_BYTES),
        name=name,
    )(q, k, v)


FFN_TM = 1024
FFN_SUB = 256
FFN_SUB_SHIFT = 8


def _swiglu_step(load_x, nv, j, wg_ref, wu_ref, wd_ref, o_ref, wg_b, wu_b, wd_b):
    @pl.when(nv > 0)
    def _():
        wg_b[...] = wg_ref[0].astype(BF16)
        wu_b[...] = wu_ref[0].astype(BF16)
        wd_b[...] = wd_ref[0].astype(BF16)

    @pl.when(j == 0)
    def _():
        o_ref[...] = jnp.zeros(o_ref.shape, F32)

    def accumulate(rows):
        x = load_x(rows)
        g = _dot(x, wg_b[...])
        u = _dot(x, wu_b[...])
        o_ref[rows, :] += _dot((g * jax.nn.sigmoid(g) * u).astype(BF16), wd_b[...])

    @pl.when(nv == FFN_TM)
    def _():
        for s in range(FFN_TM // FFN_SUB):
            accumulate(slice(s * FFN_SUB, (s + 1) * FFN_SUB))

    for s in range(FFN_TM // FFN_SUB):
        @pl.when((s * FFN_SUB < nv) & (nv < FFN_TM))
        def _():
            accumulate(slice(s * FFN_SUB, (s + 1) * FFN_SUB))


def _ffn_kernel(te_ref, nv_ref, x_ref, wg_ref, wu_ref, wd_ref, o_ref, wg_b, wu_b, wd_b):
    del te_ref
    nv = nv_ref[pl.program_id(0)]
    _swiglu_step(lambda rows: x_ref[rows, :], nv, pl.program_id(1), wg_ref, wu_ref, wd_ref, o_ref, wg_b, wu_b, wd_b)


def _moe_ffn_kernel(te_ref, nv_ref, tok_ref, h_hbm, wg_ref, wu_ref, wd_ref, o_ref, x_buf, wg_b, wu_b, wd_b, sem):
    del te_ref
    i = pl.program_id(0)
    j = pl.program_id(1)
    nv = nv_ref[i]

    @pl.when((j == 0) & (nv > 0))
    def _():
        n_rows = jnp.left_shift(jnp.right_shift(nv + (FFN_SUB - 1), FFN_SUB_SHIFT), FFN_SUB_SHIFT)

        def row_copy(tok, r):
            return pltpu.make_async_copy(h_hbm.at[pl.ds(tok, 1)], x_buf.at[pl.ds(r, 1)], sem)

        def start(r, carry):
            row_copy(tok_ref[i * FFN_TM + r], r).start()
            return carry

        lax.fori_loop(0, n_rows, start, 0)

        def wait(r, carry):
            row_copy(0, 0).wait()
            return carry

        lax.fori_loop(0, n_rows, wait, 0)

    _swiglu_step(lambda rows: x_buf[rows, :].astype(BF16), nv, j, wg_ref, wu_ref, wd_ref, o_ref, wg_b, wu_b, wd_b)


def _ffn_specs(tf, n_prefetch):
    nff = D_FF // tf

    def jj(i, j, nv):
        return jnp.where(nv[i] > 0, j, nff - 1)

    w_in = pl.BlockSpec((1, D_MODEL, tf), lambda i, j, te, nv, *_: (te[i], 0, jj(i, j, nv)))
    w_out = pl.BlockSpec((1, tf, D_MODEL), lambda i, j, te, nv, *_: (te[i], jj(i, j, nv), 0))
    out = pl.BlockSpec((FFN_TM, D_MODEL), lambda i, j, *_: (i, 0))
    scratch = [pltpu.VMEM((D_MODEL, tf), BF16), pltpu.VMEM((D_MODEL, tf), BF16), pltpu.VMEM((tf, D_MODEL), BF16)]
    return nff, w_in, w_out, out, scratch


def _grouped_swiglu(tile_expert, tile_valid, x, wg, wu, wd, *, tf, name):
    r = x.shape[0]
    nff, w_in, w_out, out, scratch = _ffn_specs(tf, 2)
    grid_spec = pltpu.PrefetchScalarGridSpec(
        num_scalar_prefetch=2,
        grid=(r // FFN_TM, nff),
        in_specs=[pl.BlockSpec((FFN_TM, D_MODEL), lambda i, j, *_: (i, 0)), w_in, w_in, w_out],
        out_specs=out,
        scratch_shapes=scratch,
    )
    return pl.pallas_call(
        _ffn_kernel,
        out_shape=jax.ShapeDtypeStruct((r, D_MODEL), F32),
        grid_spec=grid_spec,
        compiler_params=_cparams(2, VMEM_LIMIT_BYTES),
        name=name,
    )(tile_expert, tile_valid, x, wg, wu, wd)


def _gathered_swiglu(tile_expert, tile_valid, row_token, h, wg, wu, wd, *, tf, name):
    r = row_token.shape[0]
    nff, w_in, w_out, out, scratch = _ffn_specs(tf, 3)
    grid_spec = pltpu.PrefetchScalarGridSpec(
        num_scalar_prefetch=3,
        grid=(r // FFN_TM, nff),
        in_specs=[pl.BlockSpec(memory_space=pl.ANY), w_in, w_in, w_out],
        out_specs=out,
        scratch_shapes=[pltpu.VMEM((FFN_TM, D_MODEL), F32)] + scratch + [pltpu.SemaphoreType.DMA(())],
    )
    return pl.pallas_call(
        _moe_ffn_kernel,
        out_shape=jax.ShapeDtypeStruct((r, D_MODEL), F32),
        grid_spec=grid_spec,
        compiler_params=_cparams(2, VMEM_LIMIT_BYTES),
        name=name,
    )(tile_expert, tile_valid, row_token, h, wg, wu, wd)


ROUTER_TM = 256
META_E0, META_E1, META_W0, META_W1, META_R0, META_R1 = range(6)


def _router_kernel(h_ref, r_ref, lt_ref, meta_ref, cnt_ref, run_scr):
    i = pl.program_id(0)

    @pl.when(i == 0)
    def _():
        run_scr[...] = jnp.zeros(run_scr.shape, F32)

    def split(x):
        hi = x.astype(BF16)
        return hi, (x - hi.astype(F32)).astype(BF16)

    h_hi, h_lo = split(h_ref[...])
    r_hi, r_lo = split(r_ref[...])
    logits = _dot(h_hi, r_hi) + (_dot(h_hi, r_lo) + _dot(h_lo, r_hi))
    lane = lax.broadcasted_iota(I32, (ROUTER_TM, LANES), 1).astype(F32)
    logits = jnp.where(lane < N_EXPERTS, logits, -jnp.inf)

    def top1(x):
        m = jnp.max(x, axis=1, keepdims=True)
        idx = jnp.min(jnp.where(x == m, lane, float(LANES)), axis=1, keepdims=True)
        return m, idx

    m0, e0 = top1(logits)
    m1, e1 = top1(jnp.where(lane == e0, -jnp.inf, logits))
    ex = jnp.exp(m1 - m0)
    w0 = 1.0 / (1.0 + ex)
    w1 = ex / (1.0 + ex)

    hot0 = jnp.where(lane == e0, 1.0, 0.0)
    hot1 = jnp.where(lane == e1, 1.0, 0.0)
    before0 = _dot(lt_ref[...], hot0.astype(BF16)) + run_scr[0:1, :]
    tot0 = jnp.sum(hot0, axis=0, keepdims=True)
    before1 = _dot(lt_ref[...], hot1.astype(BF16)) + run_scr[0:1, :] + tot0
    r0 = jnp.sum(hot0 * before0, axis=1, keepdims=True)
    r1 = jnp.sum(hot1 * before1, axis=1, keepdims=True)
    run_new = run_scr[0:1, :] + tot0 + jnp.sum(hot1, axis=0, keepdims=True)
    run_scr[...] = jnp.broadcast_to(run_new, run_scr.shape)
    cnt_ref[...] = jnp.broadcast_to(run_new, cnt_ref.shape)

    meta = jnp.zeros((ROUTER_TM, LANES), F32)
    for ln, val in ((META_E0, e0), (META_E1, e1), (META_W0, w0), (META_W1, w1),
                    (META_R0, r0), (META_R1, r1)):
        meta = jnp.where(lane == ln, val, meta)
    meta_ref[...] = meta


def _router(h, router_padded, lower_tri):
    return pl.pallas_call(
        _router_kernel,
        out_shape=(jax.ShapeDtypeStruct((TOKENS, LANES), F32), jax.ShapeDtypeStruct((8, LANES), F32)),
        grid=(TOKENS // ROUTER_TM,),
        in_specs=[pl.BlockSpec((ROUTER_TM, D_MODEL), lambda i: (i, 0)),
                  pl.BlockSpec((D_MODEL, LANES), lambda i: (0, 0)),
                  pl.BlockSpec((ROUTER_TM, ROUTER_TM), lambda i: (0, 0))],
        out_specs=(pl.BlockSpec((ROUTER_TM, LANES), lambda i: (i, 0)), pl.BlockSpec((8, LANES), lambda i: (0, 0))),
        scratch_shapes=[pltpu.VMEM((8, LANES), F32)],
        compiler_params=_cparams(1, VMEM_LIMIT_BYTES),
        name="moe_router",
    )(h, router_padded, lower_tri)


COMBINE_TM = 256


def _combine_kernel(rows_ref, y_hbm, h_ref, meta_ref, g_ref, b_ref, o_ref, ob_ref, buf, sem):
    i = pl.program_id(0)

    def row_copy(src, slot, r):
        return pltpu.make_async_copy(y_hbm.at[pl.ds(src, 1)], buf.at[slot, pl.ds(r, 1)], sem)

    def start(r, carry):
        tok = i * COMBINE_TM + r
        row_copy(rows_ref[tok], 0, r).start()
        row_copy(rows_ref[TOKENS + tok], 1, r).start()
        return carry

    lax.fori_loop(0, COMBINE_TM, start, 0)

    def wait(r, carry):
        row_copy(0, 0, 0).wait()
        row_copy(0, 1, 0).wait()
        return carry

    lax.fori_loop(0, COMBINE_TM, wait, 0)
    meta = meta_ref[...]
    f = meta[:, META_W0:META_W0 + 1] * buf[0] + meta[:, META_W1:META_W1 + 1] * buf[1]
    out = _layer_norm_rows(DN_ALPHA * h_ref[...] + f, g_ref[...], b_ref[...])
    o_ref[...] = out
    ob_ref[...] = out.astype(BF16)


def _combine(rows, y, h, meta, g, b, layer):
    grid_spec = pltpu.PrefetchScalarGridSpec(
        num_scalar_prefetch=1,
        grid=(TOKENS // COMBINE_TM,),
        in_specs=[pl.BlockSpec(memory_space=pl.ANY),
                  pl.BlockSpec((COMBINE_TM, D_MODEL), lambda i, rows: (i, 0)),
                  pl.BlockSpec((COMBINE_TM, LANES), lambda i, rows: (i, 0)),
                  pl.BlockSpec((None, 1, D_MODEL), lambda i, rows: (layer, 0, 0)),
                  pl.BlockSpec((None, 1, D_MODEL), lambda i, rows: (layer, 0, 0))],
        out_specs=(pl.BlockSpec((COMBINE_TM, D_MODEL), lambda i, rows: (i, 0)),
                   pl.BlockSpec((COMBINE_TM, D_MODEL), lambda i, rows: (i, 0))),
        scratch_shapes=[pltpu.VMEM((2, COMBINE_TM, D_MODEL), F32), pltpu.SemaphoreType.DMA(())],
    )
    return pl.pallas_call(
        _combine_kernel,
        out_shape=(jax.ShapeDtypeStruct((TOKENS, D_MODEL), F32), jax.ShapeDtypeStruct((TOKENS, D_MODEL), BF16)),
        grid_spec=grid_spec,
        compiler_params=_cparams(1, VMEM_LIMIT_BYTES),
        name="moe_combine",
    )(rows, y, h, meta, g, b)


def _pack_w_in(w_in):
    src = {}
    off = 0
    for name, width in IN_SPLITS:
        src[name] = (off, off + width)
        off += width
    lead = w_in.shape[:-1]

    def cols(name):
        a, b = src[name]
        return w_in[..., a:b]

    def zeros(width):
        return jnp.zeros(lead + (width,), w_in.dtype)

    pieces = []
    for name, width in _PACKED:
        if name == 'idx_k_lo':
            pieces += [cols('idx_k'), zeros(64)]
        elif name == 'idx_k_hi':
            pieces += [zeros(64), cols('idx_k')]
        elif name == 'misc':
            pieces += [cols('idx_w'), cols('nsa_g'), zeros(128 - 28)]
        elif name == 'mla_kr':
            pieces += [cols('mla_kr'), zeros(64)]
        else:
            pieces.append(cols(name))
    return jnp.concatenate(pieces, axis=-1).astype(BF16)


def _t5_bucket_np(dist):
    exact = REL_BUCKETS // 2
    d = np.maximum(dist, 0)
    log_ratio = np.log(np.maximum(d, 1).astype(np.float32) / exact) / math.log(REL_MAX_DIST / exact)
    far = np.minimum(exact + (log_ratio * (REL_BUCKETS - exact)).astype(np.int32), REL_BUCKETS - 1)
    return np.where(d < exact, d, far).astype(np.int32)


def _bias_slabs(table4):
    i = np.arange(Q_TILE)[:, None]
    j = np.arange(K_CHUNK)[None, :]
    bucket = jnp.asarray(np.stack([_t5_bucket_np(i - j - rel) for rel in (-256, -128, 0, -2 * SEQ)]))
    tab = table4.astype(F32)
    out = jnp.zeros((4, N_HEADS, Q_TILE, K_CHUNK), F32)
    for b in range(REL_BUCKETS):
        out = jnp.where((bucket == b)[:, None], tab[b][None, :, None, None], out)
    return out


def _rope_tables():
    half = QK_ROPE // 2
    inv = ROPE_BASE ** (-jnp.arange(half, dtype=F32) / half)
    ang = jnp.arange(SEQ, dtype=F32)[:, None] * inv[None, :]
    cos, sin = jnp.cos(ang), jnp.sin(ang)
    zero = jnp.zeros_like(cos)
    pad = jnp.zeros((SEQ, LANES - QK_ROPE), F32)
    cos_t = jnp.concatenate([cos, cos, pad], axis=1)
    nsin_t = jnp.concatenate([-sin, zero, pad], axis=1)
    psin_t = jnp.concatenate([zero, sin, pad], axis=1)
    return cos_t, nsin_t, psin_t


def _cover_expanded():
    n = np.arange(LANES)[:, None]
    s = np.arange(SEQ)[None, :]
    j = s // SLC_LEN
    cover = (CMP_STRIDE * n < SLC_LEN * j + SLC_LEN) & (CMP_STRIDE * n + CMP_LEN > SLC_LEN * j) & (n < N_CMP)
    return jnp.asarray(cover, BF16)


def _strict_upper(n):
    return jnp.asarray(np.arange(n)[:, None] > np.arange(n)[None, :], BF16)


def _strict_lower(n):
    return jnp.asarray(np.arange(n)[None, :] < np.arange(n)[:, None], BF16)


def _moe_layout(meta, counts):
    n_tiles = 2 * TOKENS // FFN_TM + N_EXPERTS
    cnt = counts[0, :N_EXPERTS].astype(I32)
    padded = ((cnt + FFN_TM - 1) // FFN_TM) * FFN_TM
    ends = jnp.cumsum(padded)
    offs = ends - padded
    e = meta[:, META_E0:META_E1 + 1].astype(I32)
    rank = meta[:, META_R0:META_R1 + 1].astype(I32)
    rows = (offs[e] + rank).T.reshape(-1)
    token = jnp.tile(jnp.arange(TOKENS, dtype=I32), 2)
    row_token = jnp.zeros((n_tiles * FFN_TM,), I32).at[rows].set(token)
    tile_start = jnp.arange(n_tiles, dtype=I32) * FFN_TM
    te = jnp.minimum(jnp.sum(tile_start[:, None] >= ends[None, :], axis=1), N_EXPERTS - 1).astype(I32)
    nv = jnp.clip(cnt[te] - (tile_start - offs[te]), 0, FFN_TM).astype(I32)
    return rows, row_token, te, nv


def kernel(x, mem, rel_table, w_in, mla_q_norm, mla_kv_norm, mla_w_uq, mla_w_uk, mla_w_uv, nsa_pe_k, nsa_pe_v,
           nsa_ck_w1, nsa_ck_w2, nsa_cv_w1, nsa_cv_w2, w_branch, w_out, ln1_g, ln1_b, xa_wq, xa_wk, xa_wv, xa_wo,
           ln2_g, ln2_b, ffn_w_gate, ffn_w_up, ffn_w_down, moe_router, moe_w_gate, moe_w_up, moe_w_down,
           ln3_g, ln3_b):
    h = x.reshape(TOKENS, D_MODEL)
    hb = h.astype(BF16)
    mem_b = mem.reshape(BATCH * MEM_LEN, D_MODEL).astype(BF16)

    w_in_p = _pack_w_in(w_in)
    dsa_slabs = _bias_slabs(rel_table[:, :N_HEADS])
    nsa_slabs = _bias_slabs(rel_table[:, N_HEADS:])
    cos_t, nsin_t, psin_t = _rope_tables()
    cover = _cover_expanded()
    upper = _strict_upper(SB_TILE)
    lower = _strict_lower(ROUTER_TM)

    uq = mla_w_uq.reshape(DEPTH, Q_LORA, N_HEADS, QK_NOPE + QK_ROPE)
    w_qn = uq[..., :QK_NOPE].reshape(DEPTH, Q_LORA, N_HEADS * QK_NOPE).astype(BF16)
    w_qp = jnp.concatenate([uq[..., QK_NOPE:], jnp.zeros((DEPTH, Q_LORA, N_HEADS, LANES - QK_ROPE), F32)],
                           axis=-1).reshape(DEPTH, Q_LORA, N_HEADS * LANES).astype(BF16)
    w_uk = mla_w_uk.astype(BF16)
    w_uv = mla_w_uv.astype(BF16)
    w_br = w_branch.astype(BF16)
    w_o = w_out.astype(BF16)
    xq, xk, xv, xo = (w.astype(BF16) for w in (xa_wq, xa_wk, xa_wv, xa_wo))
    router_p = jnp.pad(moe_router, ((0, 0), (0, 0), (0, LANES - N_EXPERTS)))
    row2 = lambda a: a.reshape(DEPTH, 1, -1)
    g1, b1, g2, b2, g3, b3 = (row2(a) for a in (ln1_g, ln1_b, ln2_g, ln2_b, ln3_g, ln3_b))
    qn, kvn = row2(mla_q_norm), row2(mla_kv_norm)
    pek = nsa_pe_k.reshape(DEPTH, 1, CMP_LEN * HEAD_DIM)
    pev = nsa_pe_v.reshape(DEPTH, 1, CMP_LEN * HEAD_DIM)
    dense_nv = jnp.full((TOKENS // FFN_TM,), FFN_TM, I32)
    moe_wg = moe_w_gate.reshape((-1,) + moe_w_gate.shape[2:])
    moe_wu = moe_w_up.reshape((-1,) + moe_w_up.shape[2:])
    moe_wd = moe_w_down.reshape((-1,) + moe_w_down.shape[2:])

    for layer in range(DEPTH):
        z = _matmul(hb, w_in_p, layer, tm=1024, tn=1024, out_dtype=BF16, name="in_proj")
        o_dsa = _dsa(z, dsa_slabs)
        o_sb = _stick_breaking(z, upper)
        group = lambda name: z[:, _OFF[name]:_OFF[name] + HEAD_DIM].reshape(BATCH * LANES, CMP_STRIDE * HEAD_DIM)
        k_cmp, v_cmp = _nsa_compress(group('nsa_kc'), group('nsa_vc'), pek[layer], pev[layer],
                                     nsa_ck_w1[layer], nsa_ck_w2[layer], nsa_cv_w1[layer], nsa_cv_w2[layer])
        o_nsa = _nsa(z, k_cmp, v_cmp, cover, nsa_slabs)
        q_cat, k_cat, v_mla = _mla_prep(z, qn[layer], kvn[layer], w_qn[layer], w_qp[layer], w_uk[layer],
                                        w_uv[layer], cos_t, nsin_t, psin_t, tm=512)
        o_mla = _flash(q_cat, k_cat, v_mla, dk=MLA_QK, kv_len=SEQ, scale=(QK_NOPE + QK_ROPE) ** -0.5,
                       causal=True, name="mla_attention")
        mixed = _merge((o_dsa, o_sb, o_nsa, o_mla), z, w_br, layer, tm=512, tn=512)
        h, hb = _matmul_res_ln(mixed, w_o, h, g1, b1, layer, tm=256, name="out_proj_ln")

        q_x = _matmul(hb, xq, layer, tm=1024, tn=512, out_dtype=BF16, name="xa_q_proj")
        k_x = _matmul(mem_b, xk, layer, tm=1024, tn=512, out_dtype=BF16, name="xa_k_proj")
        v_x = _matmul(mem_b, xv, layer, tm=1024, tn=512, out_dtype=BF16, name="xa_v_proj")
        o_x = _flash(q_x, k_x, v_x, dk=HEAD_DIM, kv_len=MEM_LEN, scale=HEAD_DIM ** -0.5, causal=False,
                     name="cross_attention")
        h, hb = _matmul_res_ln(o_x, xo, h, g2, b2, layer, tm=256, name="xa_out_proj_ln")

        i = layer // 2
        if layer % 2 == 0:
            dense_te = jnp.full((TOKENS // FFN_TM,), i, I32)
            y = _grouped_swiglu(dense_te, dense_nv, hb, ffn_w_gate, ffn_w_up, ffn_w_down, tf=256,
                                name="dense_swiglu")
            h, hb = _res_ln(y, h, g3, b3, layer, tm=256, name="ffn_res_ln")
        else:
            meta, counts = _router(h, router_p[i], lower)
            rows, row_token, te, nv = _moe_layout(meta, counts)
            y = _gathered_swiglu(te + i * N_EXPERTS, nv, row_token, h, moe_wg, moe_wu, moe_wd, tf=256,
                                 name="moe_swiglu")
            h, hb = _combine(rows, y, h, meta, g3, b3, layer)
    return h.reshape(BATCH, SEQ, D_MODEL)
```

```python
import functools
import math

import jax
import jax.numpy as jnp
import numpy as np
from jax import lax
from jax.experimental import pallas as pl
from jax.experimental.pallas import tpu as pltpu

F32 = jnp.float32
BF16 = jnp.bfloat16
I32 = jnp.int32

D_MODEL = 2048
BATCH = 4
SEQ = 2048
DEPTH = 4
TOKENS = BATCH * SEQ
MEM_LEN = 256
HEAD_DIM = 128
N_HEADS = 4
DSA_TOPK = min(256, SEQ // 4)
IDX_HEADS = 16
IDX_DIM = 64
CMP_LEN = 32
CMP_STRIDE = 16
CMP_HIDDEN = 256
N_CMP = (SEQ - CMP_LEN) // CMP_STRIDE + 1
SLC_LEN = 64
SLC_SHIFT = 6
N_SLC = SEQ // SLC_LEN
N_SEL = min(16, N_SLC)
WINDOW = 512
FORCE_SCORE = 1.0e4
Q_LORA = 512
KV_LORA = 512
QK_NOPE = 128
QK_ROPE = 64
ROPE_BASE = 10000.0
N_MIXERS = 4
MIX_WIDTH = 512
REL_BUCKETS = 32
REL_MAX_DIST = 128
D_FF = 5632
N_EXPERTS = 8
DN_ALPHA = (2 * DEPTH) ** 0.25
LN_EPS = 1e-5
RMS_EPS = 1e-6
NEG_BIG = -1.0e30

IN_SPLITS = (
    ('dsa_q', 512), ('dsa_k', 128), ('dsa_v', 128),
    ('idx_q', IDX_HEADS * IDX_DIM), ('idx_k', IDX_DIM), ('idx_w', IDX_HEADS),
    ('sb_q', 512), ('sb_k', 512), ('sb_v', 512),
    ('nsa_q', 512),
    ('nsa_kc', 128), ('nsa_vc', 128), ('nsa_ks', 128), ('nsa_vs', 128),
    ('nsa_kw', 128), ('nsa_vw', 128), ('nsa_g', 12),
    ('mla_cq', Q_LORA), ('mla_ckv', KV_LORA), ('mla_kr', QK_ROPE),
    ('gates', N_MIXERS * D_MODEL),
)

LANES = 128
Q_TILE = 128
K_CHUNK = 256
N_CHUNKS = SEQ // K_CHUNK
VMEM_LIMIT_BYTES = 56 * 1024 * 1024
INT_MIN = -2147483648
KEY_NEG_INF = -2139095041

_PACKED = (
    ('idx_q', 1024), ('dsa_q', 512), ('sb_q', 512), ('sb_k', 512), ('sb_v', 512), ('nsa_q', 512),
    ('mla_cq', 512), ('mla_ckv', 512),
    ('dsa_k', 128), ('dsa_v', 128), ('idx_k_lo', 128), ('idx_k_hi', 128), ('misc', 128),
    ('nsa_kc', 128), ('nsa_vc', 128), ('nsa_ks', 128), ('nsa_vs', 128), ('nsa_kw', 128), ('nsa_vw', 128),
    ('mla_kr', 128), ('gates', 8192),
)
_OFF = {}
_o = 0
for _n, _w in _PACKED:
    _OFF[_n] = _o
    _o += _w
Z_WIDTH = _o
MISC_IDXW = 0
MISC_NSAG = 16


def _cparams(n_axes, vmem=None):
    return pltpu.CompilerParams(dimension_semantics=("arbitrary",) * n_axes, vmem_limit_bytes=vmem)


def _dot(a, b):
    return jnp.dot(a, b, preferred_element_type=F32)


def _dot_nt(a, b):
    return lax.dot_general(a, b, (((1,), (1,)), ((), ())), preferred_element_type=F32)


def _layer_norm_rows(v, g, b):
    mu = jnp.mean(v, axis=-1, keepdims=True)
    d = v - mu
    var = jnp.mean(d * d, axis=-1, keepdims=True)
    return d * lax.rsqrt(var + LN_EPS) * g + b


def _matmul_kernel(x_ref, w_ref, o_ref):
    o_ref[...] = _dot(x_ref[...].astype(BF16), w_ref[...].astype(BF16)).astype(o_ref.dtype)


def _matmul(x, w, layer, *, tm, tn, out_dtype, name):
    m, k = x.shape
    n = w.shape[2]
    return pl.pallas_call(
        _matmul_kernel,
        out_shape=jax.ShapeDtypeStruct((m, n), out_dtype),
        grid=(n // tn, m // tm),
        in_specs=[pl.BlockSpec((tm, k), lambda j, i: (i, 0)),
                  pl.BlockSpec((None, k, tn), lambda j, i: (layer, 0, j))],
        out_specs=pl.BlockSpec((tm, tn), lambda j, i: (i, j)),
        compiler_params=_cparams(2, VMEM_LIMIT_BYTES),
        name=name,
    )(x, w)


def _matmul_res_ln_kernel(x_ref, w_ref, h_ref, g_ref, b_ref, o_ref, ob_ref):
    y = _dot(x_ref[...], w_ref[...])
    out = _layer_norm_rows(DN_ALPHA * h_ref[...] + y, g_ref[...], b_ref[...])
    o_ref[...] = out
    ob_ref[...] = out.astype(BF16)


def _matmul_res_ln(x, w, h, g, b, layer, *, tm, name):
    m, k = x.shape
    d = w.shape[2]
    return pl.pallas_call(
        _matmul_res_ln_kernel,
        out_shape=(jax.ShapeDtypeStruct((m, d), F32), jax.ShapeDtypeStruct((m, d), BF16)),
        grid=(m // tm,),
        in_specs=[pl.BlockSpec((tm, k), lambda i: (i, 0)),
                  pl.BlockSpec((None, k, d), lambda i: (layer, 0, 0)),
                  pl.BlockSpec((tm, d), lambda i: (i, 0)),
                  pl.BlockSpec((None, 1, d), lambda i: (layer, 0, 0)),
                  pl.BlockSpec((None, 1, d), lambda i: (layer, 0, 0))],
        out_specs=(pl.BlockSpec((tm, d), lambda i: (i, 0)), pl.BlockSpec((tm, d), lambda i: (i, 0))),
        compiler_params=_cparams(1, VMEM_LIMIT_BYTES),
        name=name,
    )(x, w, h, g, b)


def _res_ln_kernel(y_ref, h_ref, g_ref, b_ref, o_ref, ob_ref):
    out = _layer_norm_rows(DN_ALPHA * h_ref[...] + y_ref[...], g_ref[...], b_ref[...])
    o_ref[...] = out
    ob_ref[...] = out.astype(BF16)


def _res_ln(y, h, g, b, layer, *, tm, name):
    m, d = h.shape
    return pl.pallas_call(
        _res_ln_kernel,
        out_shape=(jax.ShapeDtypeStruct((m, d), F32), jax.ShapeDtypeStruct((m, d), BF16)),
        grid=(m // tm,),
        in_specs=[pl.BlockSpec((tm, d), lambda i: (i, 0)),
                  pl.BlockSpec((tm, d), lambda i: (i, 0)),
                  pl.BlockSpec((None, 1, d), lambda i: (layer, 0, 0)),
                  pl.BlockSpec((None, 1, d), lambda i: (layer, 0, 0))],
        out_specs=(pl.BlockSpec((tm, d), lambda i: (i, 0)), pl.BlockSpec((tm, d), lambda i: (i, 0))),
        compiler_params=_cparams(1, VMEM_LIMIT_BYTES),
        name=name,
    )(y, h, g, b)


def _merge_kernel(b0, b1, b2, b3, g0, g1, g2, g3, wb_ref, o_ref):
    acc = None
    for n, (br, gr) in enumerate(((b0, g0), (b1, g1), (b2, g2), (b3, g3))):
        y = _dot(br[...], wb_ref[n])
        gy = jax.nn.sigmoid(gr[...].astype(F32)) * y
        acc = gy if acc is None else acc + gy
    o_ref[...] = acc.astype(o_ref.dtype)


def _merge(branches, z, wb, layer, *, tm, tn):
    m = z.shape[0]
    gate_specs = []
    for n in range(N_MIXERS):
        base = (_OFF['gates'] + n * D_MODEL) // tn
        gate_specs.append(pl.BlockSpec((tm, tn), lambda j, i, base=base: (i, base + j)))
    return pl.pallas_call(
        _merge_kernel,
        out_shape=jax.ShapeDtypeStruct((m, D_MODEL), BF16),
        grid=(D_MODEL // tn, m // tm),
        in_specs=[pl.BlockSpec((tm, MIX_WIDTH), lambda j, i: (i, 0))] * N_MIXERS + gate_specs
        + [pl.BlockSpec((None, N_MIXERS, MIX_WIDTH, tn), lambda j, i: (layer, 0, 0, j))],
        out_specs=pl.BlockSpec((tm, tn), lambda j, i: (i, j)),
        compiler_params=_cparams(2, VMEM_LIMIT_BYTES),
        name="branch_merge",
    )(*branches, z, z, z, z, wb)


def _sortable_key(score):
    score = jnp.where(score == 0.0, 0.0, score)
    bits = pltpu.bitcast(score, I32)
    return bits ^ (jnp.right_shift(bits, 31) & 0x7FFFFFFF)


def _kth_largest_key(key_scr, k, n_chunks, thr_scr):
    def search(n):
        def body(i, lo_u):
            cand_u = lo_u | jnp.left_shift(jnp.int32(1), 31 - i)
            cand_s = cand_u ^ INT_MIN
            cnt = jnp.zeros((Q_TILE, K_CHUNK), F32)
            for c in range(n):
                cnt = cnt + jnp.where(key_scr[c] >= cand_s, 1.0, 0.0)
            tot = jnp.sum(cnt, axis=1, keepdims=True)
            return jnp.where(tot >= float(k), cand_u, lo_u)

        return lax.fori_loop(0, 32, body, jnp.zeros((Q_TILE, 1), I32)) ^ INT_MIN

    n_even = jnp.left_shift(jnp.right_shift(n_chunks + 1, 1), 1)
    for n in range(2, N_CHUNKS + 1, 2):
        @pl.when(n_even == n)
        def _():
            thr_scr[...] = search(n)

    return thr_scr[...]


def _slab_index(rel):
    return jnp.where(rel == 0, 2, jnp.where(rel == -128, 1, jnp.where(rel == -256, 0, 3)))


def _stack_heads(q):
    return jnp.concatenate([q[:, h * HEAD_DIM:(h + 1) * HEAD_DIM] for h in range(N_HEADS)], axis=0)


def _mqa_masked_attention(qs, k_ref, v_ref, slab_ref, t0, chunks, ok_fn, s_scr, p_scr):
    scale = HEAD_DIM ** -0.5
    width = len(chunks) * K_CHUNK
    values = []
    for n, c in enumerate(chunks):
        cols = slice(n * K_CHUNK, (n + 1) * K_CHUNK)
        start = c * K_CHUNK if isinstance(c, int) else pl.multiple_of(jnp.maximum(c, 0) * K_CHUNK, K_CHUNK)
        madd = jnp.where(ok_fn(c), 0.0, NEG_BIG)
        s4 = _dot_nt(qs, k_ref[pl.ds(start, K_CHUNK), :]) * scale
        values.append(v_ref[pl.ds(start, K_CHUNK), :])
        slab = _slab_index(c * K_CHUNK - t0)
        for h in range(N_HEADS):
            s_scr[h, :, cols] = s4[h * Q_TILE:(h + 1) * Q_TILE] + slab_ref[slab, h] + madd
    norms = []
    for h in range(N_HEADS):
        s = s_scr[h, :, :width]
        p = jnp.exp(s - jnp.max(s, axis=1, keepdims=True))
        norms.append(jnp.sum(p, axis=1, keepdims=True))
        p_scr[h * Q_TILE:(h + 1) * Q_TILE, :width] = p.astype(BF16)
    o4 = _dot(p_scr[:, :width], jnp.concatenate(values, axis=0))
    return [o4[h * Q_TILE:(h + 1) * Q_TILE] / norms[h] for h in range(N_HEADS)]


def _row_pos(t0):
    return t0 + lax.broadcasted_iota(I32, (Q_TILE, K_CHUNK), 0)


def _col_pos(c):
    return c * K_CHUNK + lax.broadcasted_iota(I32, (Q_TILE, K_CHUNK), 1)


_ALL_CHUNKS = list(range(N_CHUNKS))
_ATTN_SCRATCH = [pltpu.VMEM((N_CHUNKS, Q_TILE, K_CHUNK), I32),
                 pltpu.VMEM((Q_TILE, 1), I32),
                 pltpu.VMEM((N_HEADS, Q_TILE, SEQ), F32),
                 pltpu.VMEM((N_HEADS * Q_TILE, SEQ), BF16)]
_SLAB_SPEC = pl.BlockSpec((4, N_HEADS, Q_TILE, K_CHUNK), lambda b, i: (0, 0, 0, 0))


def _dsa_kernel(q_ref, iq_ref, misc_ref, k_ref, v_ref, iklo_ref, ikhi_ref, slab_ref, o_ref,
                key_scr, thr_scr, s_scr, p_scr):
    qb = pl.program_id(1)
    t0 = qb * Q_TILE
    n_chunks = qb // 2 + 1
    misc = misc_ref[...].astype(F32)
    row = t0 + lax.broadcasted_iota(I32, (Q_TILE, K_CHUNK), 0)

    key_scr[...] = jnp.full(key_scr.shape, KEY_NEG_INF, I32)

    pairs = jnp.concatenate([iq_ref[:, p * LANES:(p + 1) * LANES] for p in range(IDX_HEADS // 2)], axis=0)

    def score_body(c, carry):
        start = pl.multiple_of(c * K_CHUNK, K_CHUNK)
        s_lo = _dot_nt(pairs, iklo_ref[pl.ds(start, K_CHUNK), :])
        s_hi = _dot_nt(pairs, ikhi_ref[pl.ds(start, K_CHUNK), :])
        acc = jnp.zeros((Q_TILE, K_CHUNK), F32)
        for p in range(IDX_HEADS // 2):
            rows = slice(p * Q_TILE, (p + 1) * Q_TILE)
            w0 = misc[:, MISC_IDXW + 2 * p:MISC_IDXW + 2 * p + 1]
            w1 = misc[:, MISC_IDXW + 2 * p + 1:MISC_IDXW + 2 * p + 2]
            acc = acc + jnp.maximum(s_lo[rows], 0.0) * w0
            acc = acc + jnp.maximum(s_hi[rows], 0.0) * w1
        col = c * K_CHUNK + lax.broadcasted_iota(I32, (Q_TILE, K_CHUNK), 1)
        score = jnp.where(col <= row, acc, -jnp.inf)
        key_scr[c] = _sortable_key(score)
        return carry

    lax.fori_loop(0, n_chunks, score_body, 0)
    thr = _kth_largest_key(key_scr, DSA_TOPK, n_chunks, thr_scr)

    def ok_fn(c):
        return (key_scr[c] >= thr) & (_col_pos(c) <= row)

    o = _mqa_masked_attention(_stack_heads(q_ref[...]), k_ref, v_ref, slab_ref, t0, _ALL_CHUNKS, ok_fn, s_scr, p_scr)
    o_ref[...] = jnp.concatenate(o, axis=1).astype(o_ref.dtype)


def _zspec_rows(name, width, rows, index_fn):
    cb, rem = divmod(_OFF[name], width)
    assert rem == 0, name
    return pl.BlockSpec((rows, width), functools.partial(index_fn, cb))


def _dsa(z, slabs):
    nq = SEQ // Q_TILE

    def qidx(cb, b, i):
        return (b * nq + i, cb)

    def kidx(cb, b, i):
        return (b, cb)

    return pl.pallas_call(
        _dsa_kernel,
        out_shape=jax.ShapeDtypeStruct((TOKENS, MIX_WIDTH), BF16),
        grid=(BATCH, nq),
        in_specs=[_zspec_rows('dsa_q', 512, Q_TILE, qidx),
                  _zspec_rows('idx_q', 1024, Q_TILE, qidx),
                  _zspec_rows('misc', 128, Q_TILE, qidx),
                  _zspec_rows('dsa_k', 128, SEQ, kidx),
                  _zspec_rows('dsa_v', 128, SEQ, kidx),
                  _zspec_rows('idx_k_lo', 128, SEQ, kidx),
                  _zspec_rows('idx_k_hi', 128, SEQ, kidx),
                  _SLAB_SPEC],
        out_specs=pl.BlockSpec((Q_TILE, MIX_WIDTH), lambda b, i: (b * nq + i, 0)),
        scratch_shapes=_ATTN_SCRATCH,
        compiler_params=_cparams(2, VMEM_LIMIT_BYTES),
        name="dsa_attention",
    )(z, z, z, z, z, z, z, slabs)


def _nsa_compress_kernel(xk_ref, xv_ref, pek_ref, pev_ref, kw1_ref, kw2_ref, vw1_ref, vw2_ref, ok_ref, ov_ref):
    half = CMP_STRIDE * HEAD_DIM

    def compress(x_ref, pe_ref, w1_ref, w2_ref):
        x = x_ref[...].astype(F32)
        a = _dot((x + pe_ref[:, :half]).astype(BF16), w1_ref[:half, :].astype(BF16))
        b = _dot((x + pe_ref[:, half:]).astype(BF16), w1_ref[half:, :].astype(BF16))
        hid = a + pltpu.roll(b, b.shape[0] - 1, 0)
        return _dot(jax.nn.gelu(hid).astype(BF16), w2_ref[...].astype(BF16))

    ok_ref[...] = compress(xk_ref, pek_ref, kw1_ref, kw2_ref)
    ov_ref[...] = compress(xv_ref, pev_ref, vw1_ref, vw2_ref)


def _nsa_compress(xk, xv, pek, pev, kw1, kw2, vw1, vw2):
    rows = xk.shape[0]
    full = lambda a: pl.BlockSpec(a.shape, lambda i: (0,) * a.ndim)
    args = (xk, xv, pek, pev, kw1, kw2, vw1, vw2)
    return pl.pallas_call(
        _nsa_compress_kernel,
        out_shape=(jax.ShapeDtypeStruct((rows, HEAD_DIM), F32), jax.ShapeDtypeStruct((rows, HEAD_DIM), F32)),
        grid=(1,),
        in_specs=[full(a) for a in args],
        out_specs=(pl.BlockSpec((rows, HEAD_DIM), lambda i: (0, 0)), pl.BlockSpec((rows, HEAD_DIM), lambda i: (0, 0))),
        compiler_params=_cparams(1, VMEM_LIMIT_BYTES),
        name="nsa_compress",
    )(*args)


def _nsa_kernel(q_ref, misc_ref, kc_ref, vc_ref, ks_ref, vs_ref, kw_ref, vw_ref, cov_ref, slab_ref, o_ref,
                key_scr, thr_scr, s_scr, p_scr):
    qb = pl.program_id(1)
    t0 = qb * Q_TILE
    scale = HEAD_DIM ** -0.5
    qs = _stack_heads(q_ref[...])
    misc = misc_ref[...].astype(F32)

    n_idx = lax.broadcasted_iota(I32, (N_HEADS * Q_TILE, LANES), 1)
    t_idx = t0 + (lax.broadcasted_iota(I32, (N_HEADS * Q_TILE, LANES), 0) & (Q_TILE - 1))
    cmp_ok = (CMP_STRIDE * n_idx + CMP_LEN - 1) <= t_idx
    lc = jnp.where(cmp_ok, _dot_nt(qs, kc_ref[...].astype(BF16)) * scale, NEG_BIG)
    e = jnp.exp(lc - jnp.max(lc, axis=1, keepdims=True))
    p_cmp = jnp.where(cmp_ok, e / jnp.sum(e, axis=1, keepdims=True), 0.0)
    o_cmp = _dot(p_cmp.astype(BF16), vc_ref[...].astype(BF16))

    p_sum = p_cmp[0:Q_TILE] + p_cmp[Q_TILE:2 * Q_TILE] + p_cmp[2 * Q_TILE:3 * Q_TILE] + p_cmp[3 * Q_TILE:]
    p_hi = p_sum.astype(BF16)
    p_lo = (p_sum - p_hi.astype(F32)).astype(BF16)
    cur = jnp.right_shift(t0 + lax.broadcasted_iota(I32, (Q_TILE, K_CHUNK), 0), SLC_SHIFT)
    for c in range(N_CHUNKS):
        cov = cov_ref[:, c * K_CHUNK:(c + 1) * K_CHUNK]
        imp = _dot(p_hi, cov) + _dot(p_lo, cov)
        jb = jnp.right_shift(c * K_CHUNK + lax.broadcasted_iota(I32, (Q_TILE, K_CHUNK), 1), SLC_SHIFT)
        forced = (jb == 0) | (jb == cur) | (jb == cur - 1)
        imp = jnp.where(jb <= cur, imp + jnp.where(forced, FORCE_SCORE, 0.0), -jnp.inf)
        key_scr[c] = _sortable_key(imp)
    thr = _kth_largest_key(key_scr, N_SEL * SLC_LEN, qb // 2 + 1, thr_scr)
    row = _row_pos(t0)

    def sel_ok(c):
        return (key_scr[c] >= thr) & (_col_pos(c) <= row)

    o_slc = _mqa_masked_attention(qs, ks_ref, vs_ref, slab_ref, t0, _ALL_CHUNKS, sel_ok, s_scr, p_scr)

    def win_ok(c):
        col = _col_pos(c)
        dist = row - col
        return (dist >= 0) & (dist < WINDOW) & (col >= 0)

    win_chunks = [qb // 2 - 2 + n for n in range(3)]
    o_win = _mqa_masked_attention(qs, kw_ref, vw_ref, slab_ref, t0, win_chunks, win_ok, s_scr, p_scr)

    outs = []
    for h in range(N_HEADS):
        g = jax.nn.sigmoid(misc[:, MISC_NSAG + 3 * h:MISC_NSAG + 3 * h + 3])
        rows = slice(h * Q_TILE, (h + 1) * Q_TILE)
        outs.append(g[:, 0:1] * o_cmp[rows] + g[:, 1:2] * o_slc[h] + g[:, 2:3] * o_win[h])
    o_ref[...] = jnp.concatenate(outs, axis=1).astype(o_ref.dtype)


def _nsa(z, k_cmp, v_cmp, cov, slabs):
    nq = SEQ // Q_TILE

    def qidx(cb, b, i):
        return (b * nq + i, cb)

    def kidx(cb, b, i):
        return (b, cb)

    cmp_spec = pl.BlockSpec((LANES, HEAD_DIM), lambda b, i: (b, 0))
    return pl.pallas_call(
        _nsa_kernel,
        out_shape=jax.ShapeDtypeStruct((TOKENS, MIX_WIDTH), BF16),
        grid=(BATCH, nq),
        in_specs=[_zspec_rows('nsa_q', 512, Q_TILE, qidx),
                  _zspec_rows('misc', 128, Q_TILE, qidx),
                  cmp_spec, cmp_spec,
                  _zspec_rows('nsa_ks', 128, SEQ, kidx),
                  _zspec_rows('nsa_vs', 128, SEQ, kidx),
                  _zspec_rows('nsa_kw', 128, SEQ, kidx),
                  _zspec_rows('nsa_vw', 128, SEQ, kidx),
                  pl.BlockSpec((LANES, SEQ), lambda b, i: (0, 0)),
                  _SLAB_SPEC],
        out_specs=pl.BlockSpec((Q_TILE, MIX_WIDTH), lambda b, i: (b * nq + i, 0)),
        scratch_shapes=_ATTN_SCRATCH,
        compiler_params=_cparams(2, VMEM_LIMIT_BYTES),
        name="nsa_attention",
    )(z, z, k_cmp, v_cmp, z, z, z, z, cov, slabs)


SB_TILE = 256
SB_HEADS_PER_STEP = 2


def _sb_kernel(q_ref, k_ref, v_ref, u_ref, o_ref):
    qb = pl.program_id(2)
    scale = HEAD_DIM ** -0.5
    upper = u_ref[...]
    row = qb * SB_TILE + lax.broadcasted_iota(I32, (SB_TILE, SB_TILE), 0)
    lane = lax.broadcasted_iota(I32, (SB_TILE, SB_TILE), 1)

    def body(i, carry):
        c = qb - i
        start = pl.multiple_of(c * SB_TILE, SB_TILE)
        strict = (c * SB_TILE + lane) < row
        out = []
        for g in range(SB_HEADS_PER_STEP):
            later, acc = carry[g]
            cols = slice(g * HEAD_DIM, (g + 1) * HEAD_DIM)
            zl = _dot_nt(q_ref[:, cols], k_ref[pl.ds(start, SB_TILE), cols]) * scale
            log_beta = jnp.minimum(zl, 0.0) - jnp.log1p(jnp.exp(-jnp.abs(zl)))
            log_keep = jnp.where(strict, log_beta - zl, 0.0)
            keep_hi = log_keep.astype(BF16)
            keep_lo = (log_keep - keep_hi.astype(F32)).astype(BF16)
            within = _dot(keep_hi, upper) + _dot(keep_lo, upper)
            a = jnp.where(strict, jnp.exp(log_beta + within + later), 0.0)
            acc = acc + _dot(a.astype(BF16), v_ref[pl.ds(start, SB_TILE), cols])
            later = later + jnp.sum(log_keep, axis=1, keepdims=True)
            out.append((later, acc))
        return tuple(out)

    init = tuple((jnp.zeros((SB_TILE, 1), F32), jnp.zeros((SB_TILE, HEAD_DIM), F32))
                 for _ in range(SB_HEADS_PER_STEP))
    res = lax.fori_loop(0, qb + 1, body, init)
    o_ref[...] = jnp.concatenate([acc for _, acc in res], axis=1).astype(o_ref.dtype)


def _stick_breaking(z, upper):
    nq = SEQ // SB_TILE
    width = SB_HEADS_PER_STEP * HEAD_DIM
    qcb, kcb, vcb = _OFF['sb_q'] // width, _OFF['sb_k'] // width, _OFF['sb_v'] // width
    return pl.pallas_call(
        _sb_kernel,
        out_shape=jax.ShapeDtypeStruct((TOKENS, MIX_WIDTH), BF16),
        grid=(BATCH, N_HEADS // SB_HEADS_PER_STEP, nq),
        in_specs=[pl.BlockSpec((SB_TILE, width), lambda b, h, i: (b * nq + i, qcb + h)),
                  pl.BlockSpec((SEQ, width), lambda b, h, i: (b, kcb + h)),
                  pl.BlockSpec((SEQ, width), lambda b, h, i: (b, vcb + h)),
                  pl.BlockSpec((SB_TILE, SB_TILE), lambda b, h, i: (0, 0))],
        out_specs=pl.BlockSpec((SB_TILE, width), lambda b, h, i: (b * nq + i, h)),
        compiler_params=_cparams(3, VMEM_LIMIT_BYTES),
        name="stick_breaking_attention",
    )(z, z, z, upper)


MLA_QK = 2 * HEAD_DIM


def _rope128(x, cos, nsin, psin):
    return x * cos + pltpu.roll(x, 96, 1) * nsin + pltpu.roll(x, 32, 1) * psin


def _mla_prep_kernel(cq_ref, ckv_ref, kr_ref, qn_ref, kvn_ref, wqn_ref, wqp_ref, wuk_ref, wuv_ref,
                     cos_ref, nsin_ref, psin_ref, q_ref, k_ref, v_ref):
    def rms(x, g):
        return (x * lax.rsqrt(jnp.mean(x * x, axis=-1, keepdims=True) + RMS_EPS) * g).astype(BF16)

    cos, nsin, psin = cos_ref[...], nsin_ref[...], psin_ref[...]
    xq = rms(cq_ref[...].astype(F32), qn_ref[...])
    xc = rms(ckv_ref[...].astype(F32), kvn_ref[...])
    q_nope = _dot(xq, wqn_ref[...])
    q_rope = _dot(xq, wqp_ref[...])
    k_nope = _dot(xc, wuk_ref[...])
    k_rope = _rope128(kr_ref[...].astype(F32), cos, nsin, psin)
    q_parts, k_parts = [], []
    for h in range(N_HEADS):
        cols = slice(h * HEAD_DIM, (h + 1) * HEAD_DIM)
        q_parts += [q_nope[:, cols], _rope128(q_rope[:, cols], cos, nsin, psin)]
        k_parts += [k_nope[:, cols], k_rope]
    q_ref[...] = jnp.concatenate(q_parts, axis=1).astype(BF16)
    k_ref[...] = jnp.concatenate(k_parts, axis=1).astype(BF16)
    v_ref[...] = _dot(xc, wuv_ref[...]).astype(BF16)


def _mla_prep(z, qn, kvn, wqn, wqp, wuk, wuv, cos, nsin, psin, *, tm):
    nt = SEQ // tm

    def zidx(cb, i):
        return (i, cb)

    wspec = pl.BlockSpec((Q_LORA, 512), lambda i: (0, 0))
    nspec = pl.BlockSpec((1, 512), lambda i: (0, 0))
    tspec = pl.BlockSpec((tm, LANES), lambda i: (i % nt, 0))
    return pl.pallas_call(
        _mla_prep_kernel,
        out_shape=(jax.ShapeDtypeStruct((TOKENS, N_HEADS * MLA_QK), BF16),
                   jax.ShapeDtypeStruct((TOKENS, N_HEADS * MLA_QK), BF16),
                   jax.ShapeDtypeStruct((TOKENS, N_HEADS * HEAD_DIM), BF16)),
        grid=(TOKENS // tm,),
        in_specs=[_zspec_rows('mla_cq', 512, tm, zidx), _zspec_rows('mla_ckv', 512, tm, zidx),
                  _zspec_rows('mla_kr', 128, tm, zidx), nspec, nspec, wspec, wspec, wspec, wspec,
                  tspec, tspec, tspec],
        out_specs=(pl.BlockSpec((tm, N_HEADS * MLA_QK), lambda i: (i, 0)),
                   pl.BlockSpec((tm, N_HEADS * MLA_QK), lambda i: (i, 0)),
                   pl.BlockSpec((tm, N_HEADS * HEAD_DIM), lambda i: (i, 0))),
        compiler_params=_cparams(1, VMEM_LIMIT_BYTES),
        name="mla_prep",
    )(z, z, z, qn, kvn, wqn, wqp, wuk, wuv, cos, nsin, psin)


FLASH_TILE = 256


def _flash_kernel(q_ref, k_ref, v_ref, o_ref, s_scr, *, scale, causal, n_kv):
    qb = pl.program_id(2)
    q = q_ref[...]
    row = qb * FLASH_TILE + lax.broadcasted_iota(I32, (FLASH_TILE, FLASH_TILE), 0)
    lane = lax.broadcasted_iota(I32, (FLASH_TILE, FLASH_TILE), 1)
    for c in range(n_kv):
        cols = slice(c * FLASH_TILE, (c + 1) * FLASH_TILE)
        s = _dot_nt(q, k_ref[cols, :]) * scale
        if causal:
            s = jnp.where(c * FLASH_TILE + lane <= row, s, NEG_BIG)
        s_scr[:, cols] = s
    s = s_scr[...]
    p = jnp.exp(s - jnp.max(s, axis=1, keepdims=True))
    o = _dot(p.astype(BF16), v_ref[...]) / jnp.sum(p, axis=1, keepdims=True)
    o_ref[...] = o.astype(o_ref.dtype)


def _flash(q, k, v, *, dk, kv_len, scale, causal, name):
    nq = SEQ // FLASH_TILE
    return pl.pallas_call(
        functools.partial(_flash_kernel, scale=scale, causal=causal, n_kv=kv_len // FLASH_TILE),
        out_shape=jax.ShapeDtypeStruct((TOKENS, N_HEADS * HEAD_DIM), BF16),
        grid=(BATCH, N_HEADS, nq),
        in_specs=[pl.BlockSpec((FLASH_TILE, dk), lambda b, h, i: (b * nq + i, h)),
                  pl.BlockSpec((kv_len, dk), lambda b, h, i: (b, h)),
                  pl.BlockSpec((kv_len, HEAD_DIM), lambda b, h, i: (b, h))],
        out_specs=pl.BlockSpec((FLASH_TILE, HEAD_DIM), lambda b, h, i: (b * nq + i, h)),
        scratch_shapes=[pltpu.VMEM((FLASH_TILE, kv_len), F32)],
        compiler_params=_cparams(3, VMEM_LIMIT_BYTES),
        name=name,
    )(q, k, v)


FFN_TM = 1024
FFN_SUB = 256
FFN_SUB_SHIFT = 8


def _swiglu_step(load_x, nv, j, wg_ref, wu_ref, wd_ref, o_ref, wg_b, wu_b, wd_b):
    wg_b[...] = wg_ref[0].astype(BF16)
    wu_b[...] = wu_ref[0].astype(BF16)
    wd_b[...] = wd_ref[0].astype(BF16)

    @pl.when(j == 0)
    def _():
        o_ref[...] = jnp.zeros(o_ref.shape, F32)

    def accumulate(rows):
        x = load_x(rows)
        g = _dot(x, wg_b[...])
        u = _dot(x, wu_b[...])
        o_ref[rows, :] += _dot((g * jax.nn.sigmoid(g) * u).astype(BF16), wd_b[...])

    @pl.when(nv == FFN_TM)
    def _():
        for s in range(FFN_TM // FFN_SUB):
            accumulate(slice(s * FFN_SUB, (s + 1) * FFN_SUB))

    for s in range(FFN_TM // FFN_SUB):
        @pl.when((s * FFN_SUB < nv) & (nv < FFN_TM))
        def _():
            accumulate(slice(s * FFN_SUB, (s + 1) * FFN_SUB))


def _ffn_kernel(te_ref, nv_ref, x_ref, wg_ref, wu_ref, wd_ref, o_ref, wg_b, wu_b, wd_b):
    del te_ref
    nv = nv_ref[pl.program_id(0)]
    _swiglu_step(lambda rows: x_ref[rows, :], nv, pl.program_id(1), wg_ref, wu_ref, wd_ref, o_ref, wg_b, wu_b, wd_b)


def _moe_ffn_kernel(te_ref, nv_ref, tok_ref, h_hbm, wg_ref, wu_ref, wd_ref, o_ref, x_buf, wg_b, wu_b, wd_b, sem):
    del te_ref
    i = pl.program_id(0)
    j = pl.program_id(1)
    nv = nv_ref[i]

    @pl.when((j == 0) & (nv > 0))
    def _():
        n_rows = jnp.left_shift(jnp.right_shift(nv + (FFN_SUB - 1), FFN_SUB_SHIFT), FFN_SUB_SHIFT)

        def row_copy(tok, r):
            return pltpu.make_async_copy(h_hbm.at[pl.ds(tok, 1)], x_buf.at[pl.ds(r, 1)], sem)

        def start(r, carry):
            row_copy(tok_ref[i * FFN_TM + r], r).start()
            return carry

        lax.fori_loop(0, n_rows, start, 0)

        def wait(r, carry):
            row_copy(0, 0).wait()
            return carry

        lax.fori_loop(0, n_rows, wait, 0)

    _swiglu_step(lambda rows: x_buf[rows, :].astype(BF16), nv, j, wg_ref, wu_ref, wd_ref, o_ref, wg_b, wu_b, wd_b)


def _ffn_specs(tf, n_prefetch):
    nff = D_FF // tf

    def jj(i, j, nv):
        return jnp.where(nv[i] > 0, j, nff - 1)

    w_in = pl.BlockSpec((1, D_MODEL, tf), lambda i, j, te, nv, *_: (te[i], 0, jj(i, j, nv)))
    w_out = pl.BlockSpec((1, tf, D_MODEL), lambda i, j, te, nv, *_: (te[i], jj(i, j, nv), 0))
    out = pl.BlockSpec((FFN_TM, D_MODEL), lambda i, j, *_: (i, 0))
    scratch = [pltpu.VMEM((D_MODEL, tf), BF16), pltpu.VMEM((D_MODEL, tf), BF16), pltpu.VMEM((tf, D_MODEL), BF16)]
    return nff, w_in, w_out, out, scratch


def _grouped_swiglu(tile_expert, tile_valid, x, wg, wu, wd, *, tf, name):
    r = x.shape[0]
    nff, w_in, w_out, out, scratch = _ffn_specs(tf, 2)
    grid_spec = pltpu.PrefetchScalarGridSpec(
        num_scalar_prefetch=2,
        grid=(r // FFN_TM, nff),
        in_specs=[pl.BlockSpec((FFN_TM, D_MODEL), lambda i, j, *_: (i, 0)), w_in, w_in, w_out],
        out_specs=out,
        scratch_shapes=scratch,
    )
    return pl.pallas_call(
        _ffn_kernel,
        out_shape=jax.ShapeDtypeStruct((r, D_MODEL), F32),
        grid_spec=grid_spec,
        compiler_params=_cparams(2, VMEM_LIMIT_BYTES),
        name=name,
    )(tile_expert, tile_valid, x, wg, wu, wd)


def _gathered_swiglu(tile_expert, tile_valid, row_token, h, wg, wu, wd, *, tf, name):
    r = row_token.shape[0]
    nff, w_in, w_out, out, scratch = _ffn_specs(tf, 3)
    grid_spec = pltpu.PrefetchScalarGridSpec(
        num_scalar_prefetch=3,
        grid=(r // FFN_TM, nff),
        in_specs=[pl.BlockSpec(memory_space=pl.ANY), w_in, w_in, w_out],
        out_specs=out,
        scratch_shapes=[pltpu.VMEM((FFN_TM, D_MODEL), F32)] + scratch + [pltpu.SemaphoreType.DMA(())],
    )
    return pl.pallas_call(
        _moe_ffn_kernel,
        out_shape=jax.ShapeDtypeStruct((r, D_MODEL), F32),
        grid_spec=grid_spec,
        compiler_params=_cparams(2, VMEM_LIMIT_BYTES),
        name=name,
    )(tile_expert, tile_valid, row_token, h, wg, wu, wd)


ROUTER_TM = 256
META_E0, META_E1, META_W0, META_W1, META_R0, META_R1 = range(6)


def _router_kernel(h_ref, r_ref, lt_ref, meta_ref, cnt_ref, run_scr):
    i = pl.program_id(0)

    @pl.when(i == 0)
    def _():
        run_scr[...] = jnp.zeros(run_scr.shape, F32)

    def split(x):
        hi = x.astype(BF16)
        return hi, (x - hi.astype(F32)).astype(BF16)

    h_hi, h_lo = split(h_ref[...])
    r_hi, r_lo = split(r_ref[...])
    logits = _dot(h_hi, r_hi) + (_dot(h_hi, r_lo) + _dot(h_lo, r_hi))
    lane = lax.broadcasted_iota(I32, (ROUTER_TM, LANES), 1).astype(F32)
    logits = jnp.where(lane < N_EXPERTS, logits, -jnp.inf)

    def top1(x):
        m = jnp.max(x, axis=1, keepdims=True)
        idx = jnp.min(jnp.where(x == m, lane, float(LANES)), axis=1, keepdims=True)
        return m, idx

    m0, e0 = top1(logits)
    m1, e1 = top1(jnp.where(lane == e0, -jnp.inf, logits))
    ex = jnp.exp(m1 - m0)
    w0 = 1.0 / (1.0 + ex)
    w1 = ex / (1.0 + ex)

    hot0 = jnp.where(lane == e0, 1.0, 0.0)
    hot1 = jnp.where(lane == e1, 1.0, 0.0)
    before0 = _dot(lt_ref[...], hot0.astype(BF16)) + run_scr[0:1, :]
    tot0 = jnp.sum(hot0, axis=0, keepdims=True)
    before1 = _dot(lt_ref[...], hot1.astype(BF16)) + run_scr[0:1, :] + tot0
    r0 = jnp.sum(hot0 * before0, axis=1, keepdims=True)
    r1 = jnp.sum(hot1 * before1, axis=1, keepdims=True)
    run_new = run_scr[0:1, :] + tot0 + jnp.sum(hot1, axis=0, keepdims=True)
    run_scr[...] = jnp.broadcast_to(run_new, run_scr.shape)
    cnt_ref[...] = jnp.broadcast_to(run_new, cnt_ref.shape)

    meta = jnp.zeros((ROUTER_TM, LANES), F32)
    for ln, val in ((META_E0, e0), (META_E1, e1), (META_W0, w0), (META_W1, w1),
                    (META_R0, r0), (META_R1, r1)):
        meta = jnp.where(lane == ln, val, meta)
    meta_ref[...] = meta


def _router(h, router_padded, lower_tri):
    return pl.pallas_call(
        _router_kernel,
        out_shape=(jax.ShapeDtypeStruct((TOKENS, LANES), F32), jax.ShapeDtypeStruct((8, LANES), F32)),
        grid=(TOKENS // ROUTER_TM,),
        in_specs=[pl.BlockSpec((ROUTER_TM, D_MODEL), lambda i: (i, 0)),
                  pl.BlockSpec((D_MODEL, LANES), lambda i: (0, 0)),
                  pl.BlockSpec((ROUTER_TM, ROUTER_TM), lambda i: (0, 0))],
        out_specs=(pl.BlockSpec((ROUTER_TM, LANES), lambda i: (i, 0)), pl.BlockSpec((8, LANES), lambda i: (0, 0))),
        scratch_shapes=[pltpu.VMEM((8, LANES), F32)],
        compiler_params=_cparams(1, VMEM_LIMIT_BYTES),
        name="moe_router",
    )(h, router_padded, lower_tri)


COMBINE_TM = 256


def _combine_kernel(rows_ref, y_hbm, h_ref, meta_ref, g_ref, b_ref, o_ref, ob_ref, buf, sem):
    i = pl.program_id(0)

    def row_copy(src, slot, r):
        return pltpu.make_async_copy(y_hbm.at[pl.ds(src, 1)], buf.at[slot, pl.ds(r, 1)], sem)

    def start(r, carry):
        tok = i * COMBINE_TM + r
        row_copy(rows_ref[tok], 0, r).start()
        row_copy(rows_ref[TOKENS + tok], 1, r).start()
        return carry

    lax.fori_loop(0, COMBINE_TM, start, 0)

    def wait(r, carry):
        row_copy(0, 0, 0).wait()
        row_copy(0, 1, 0).wait()
        return carry

    lax.fori_loop(0, COMBINE_TM, wait, 0)
    meta = meta_ref[...]
    f = meta[:, META_W0:META_W0 + 1] * buf[0] + meta[:, META_W1:META_W1 + 1] * buf[1]
    out = _layer_norm_rows(DN_ALPHA * h_ref[...] + f, g_ref[...], b_ref[...])
    o_ref[...] = out
    ob_ref[...] = out.astype(BF16)


def _combine(rows, y, h, meta, g, b, layer):
    grid_spec = pltpu.PrefetchScalarGridSpec(
        num_scalar_prefetch=1,
        grid=(TOKENS // COMBINE_TM,),
        in_specs=[pl.BlockSpec(memory_space=pl.ANY),
                  pl.BlockSpec((COMBINE_TM, D_MODEL), lambda i, rows: (i, 0)),
                  pl.BlockSpec((COMBINE_TM, LANES), lambda i, rows: (i, 0)),
                  pl.BlockSpec((None, 1, D_MODEL), lambda i, rows: (layer, 0, 0)),
                  pl.BlockSpec((None, 1, D_MODEL), lambda i, rows: (layer, 0, 0))],
        out_specs=(pl.BlockSpec((COMBINE_TM, D_MODEL), lambda i, rows: (i, 0)),
                   pl.BlockSpec((COMBINE_TM, D_MODEL), lambda i, rows: (i, 0))),
        scratch_shapes=[pltpu.VMEM((2, COMBINE_TM, D_MODEL), F32), pltpu.SemaphoreType.DMA(())],
    )
    return pl.pallas_call(
        _combine_kernel,
        out_shape=(jax.ShapeDtypeStruct((TOKENS, D_MODEL), F32), jax.ShapeDtypeStruct((TOKENS, D_MODEL), BF16)),
        grid_spec=grid_spec,
        compiler_params=_cparams(1, VMEM_LIMIT_BYTES),
        name="moe_combine",
    )(rows, y, h, meta, g, b)


def _pack_w_in(w_in):
    src = {}
    off = 0
    for name, width in IN_SPLITS:
        src[name] = (off, off + width)
        off += width
    lead = w_in.shape[:-1]

    def cols(name):
        a, b = src[name]
        return w_in[..., a:b]

    def zeros(width):
        return jnp.zeros(lead + (width,), w_in.dtype)

    pieces = []
    for name, width in _PACKED:
        if name == 'idx_k_lo':
            pieces += [cols('idx_k'), zeros(64)]
        elif name == 'idx_k_hi':
            pieces += [zeros(64), cols('idx_k')]
        elif name == 'misc':
            pieces += [cols('idx_w'), cols('nsa_g'), zeros(128 - 28)]
        elif name == 'mla_kr':
            pieces += [cols('mla_kr'), zeros(64)]
        else:
            pieces.append(cols(name))
    return jnp.concatenate(pieces, axis=-1).astype(BF16)


def _t5_bucket_np(dist):
    exact = REL_BUCKETS // 2
    d = np.maximum(dist, 0)
    log_ratio = np.log(np.maximum(d, 1).astype(np.float32) / exact) / math.log(REL_MAX_DIST / exact)
    far = np.minimum(exact + (log_ratio * (REL_BUCKETS - exact)).astype(np.int32), REL_BUCKETS - 1)
    return np.where(d < exact, d, far).astype(np.int32)


def _bias_slabs(table4):
    i = np.arange(Q_TILE)[:, None]
    j = np.arange(K_CHUNK)[None, :]
    bucket = jnp.asarray(np.stack([_t5_bucket_np(i - j - rel) for rel in (-256, -128, 0, -2 * SEQ)]))
    tab = table4.astype(F32)
    out = jnp.zeros((4, N_HEADS, Q_TILE, K_CHUNK), F32)
    for b in range(REL_BUCKETS):
        out = jnp.where((bucket == b)[:, None], tab[b][None, :, None, None], out)
    return out


def _rope_tables():
    half = QK_ROPE // 2
    inv = ROPE_BASE ** (-jnp.arange(half, dtype=F32) / half)
    ang = jnp.arange(SEQ, dtype=F32)[:, None] * inv[None, :]
    cos, sin = jnp.cos(ang), jnp.sin(ang)
    zero = jnp.zeros_like(cos)
    pad = jnp.zeros((SEQ, LANES - QK_ROPE), F32)
    cos_t = jnp.concatenate([cos, cos, pad], axis=1)
    nsin_t = jnp.concatenate([-sin, zero, pad], axis=1)
    psin_t = jnp.concatenate([zero, sin, pad], axis=1)
    return cos_t, nsin_t, psin_t


def _cover_expanded():
    n = np.arange(LANES)[:, None]
    s = np.arange(SEQ)[None, :]
    j = s // SLC_LEN
    cover = (CMP_STRIDE * n < SLC_LEN * j + SLC_LEN) & (CMP_STRIDE * n + CMP_LEN > SLC_LEN * j) & (n < N_CMP)
    return jnp.asarray(cover, BF16)


def _strict_upper(n):
    return jnp.asarray(np.arange(n)[:, None] > np.arange(n)[None, :], BF16)


def _strict_lower(n):
    return jnp.asarray(np.arange(n)[None, :] < np.arange(n)[:, None], BF16)


def _moe_layout(meta, counts):
    n_tiles = 2 * TOKENS // FFN_TM + N_EXPERTS
    cnt = counts[0, :N_EXPERTS].astype(I32)
    padded = ((cnt + FFN_TM - 1) // FFN_TM) * FFN_TM
    ends = jnp.cumsum(padded)
    offs = ends - padded
    e = meta[:, META_E0:META_E1 + 1].astype(I32)
    rank = meta[:, META_R0:META_R1 + 1].astype(I32)
    rows = (offs[e] + rank).T.reshape(-1)
    token = jnp.tile(jnp.arange(TOKENS, dtype=I32), 2)
    row_token = jnp.zeros((n_tiles * FFN_TM,), I32).at[rows].set(token)
    tile_start = jnp.arange(n_tiles, dtype=I32) * FFN_TM
    te = jnp.minimum(jnp.sum(tile_start[:, None] >= ends[None, :], axis=1), N_EXPERTS - 1).astype(I32)
    nv = jnp.clip(cnt[te] - (tile_start - offs[te]), 0, FFN_TM).astype(I32)
    return rows, row_token, te, nv


def kernel(x, mem, rel_table, w_in, mla_q_norm, mla_kv_norm, mla_w_uq, mla_w_uk, mla_w_uv, nsa_pe_k, nsa_pe_v,
           nsa_ck_w1, nsa_ck_w2, nsa_cv_w1, nsa_cv_w2, w_branch, w_out, ln1_g, ln1_b, xa_wq, xa_wk, xa_wv, xa_wo,
           ln2_g, ln2_b, ffn_w_gate, ffn_w_up, ffn_w_down, moe_router, moe_w_gate, moe_w_up, moe_w_down,
           ln3_g, ln3_b):
    h = x.reshape(TOKENS, D_MODEL)
    hb = h.astype(BF16)
    mem_b = mem.reshape(BATCH * MEM_LEN, D_MODEL).astype(BF16)

    w_in_p = _pack_w_in(w_in)
    dsa_slabs = _bias_slabs(rel_table[:, :N_HEADS])
    nsa_slabs = _bias_slabs(rel_table[:, N_HEADS:])
    cos_t, nsin_t, psin_t = _rope_tables()
    cover = _cover_expanded()
    upper = _strict_upper(SB_TILE)
    lower = _strict_lower(ROUTER_TM)

    uq = mla_w_uq.reshape(DEPTH, Q_LORA, N_HEADS, QK_NOPE + QK_ROPE)
    w_qn = uq[..., :QK_NOPE].reshape(DEPTH, Q_LORA, N_HEADS * QK_NOPE).astype(BF16)
    w_qp = jnp.concatenate([uq[..., QK_NOPE:], jnp.zeros((DEPTH, Q_LORA, N_HEADS, LANES - QK_ROPE), F32)],
                           axis=-1).reshape(DEPTH, Q_LORA, N_HEADS * LANES).astype(BF16)
    w_uk = mla_w_uk.astype(BF16)
    w_uv = mla_w_uv.astype(BF16)
    w_br = w_branch.astype(BF16)
    w_o = w_out.astype(BF16)
    xq, xk, xv, xo = (w.astype(BF16) for w in (xa_wq, xa_wk, xa_wv, xa_wo))
    router_p = jnp.pad(moe_router, ((0, 0), (0, 0), (0, LANES - N_EXPERTS)))
    row2 = lambda a: a.reshape(DEPTH, 1, -1)
    g1, b1, g2, b2, g3, b3 = (row2(a) for a in (ln1_g, ln1_b, ln2_g, ln2_b, ln3_g, ln3_b))
    qn, kvn = row2(mla_q_norm), row2(mla_kv_norm)
    pek = nsa_pe_k.reshape(DEPTH, 1, CMP_LEN * HEAD_DIM)
    pev = nsa_pe_v.reshape(DEPTH, 1, CMP_LEN * HEAD_DIM)
    dense_nv = jnp.full((TOKENS // FFN_TM,), FFN_TM, I32)
    moe_wg = moe_w_gate.reshape((-1,) + moe_w_gate.shape[2:])
    moe_wu = moe_w_up.reshape((-1,) + moe_w_up.shape[2:])
    moe_wd = moe_w_down.reshape((-1,) + moe_w_down.shape[2:])

    for layer in range(DEPTH):
        z = _matmul(hb, w_in_p, layer, tm=1024, tn=1024, out_dtype=BF16, name="in_proj")
        o_dsa = _dsa(z, dsa_slabs)
        o_sb = _stick_breaking(z, upper)
        group = lambda name: z[:, _OFF[name]:_OFF[name] + HEAD_DIM].reshape(BATCH * LANES, CMP_STRIDE * HEAD_DIM)
        k_cmp, v_cmp = _nsa_compress(group('nsa_kc'), group('nsa_vc'), pek[layer], pev[layer],
                                     nsa_ck_w1[layer], nsa_ck_w2[layer], nsa_cv_w1[layer], nsa_cv_w2[layer])
        o_nsa = _nsa(z, k_cmp, v_cmp, cover, nsa_slabs)
        q_cat, k_cat, v_mla = _mla_prep(z, qn[layer], kvn[layer], w_qn[layer], w_qp[layer], w_uk[layer],
                                        w_uv[layer], cos_t, nsin_t, psin_t, tm=512)
        o_mla = _flash(q_cat, k_cat, v_mla, dk=MLA_QK, kv_len=SEQ, scale=(QK_NOPE + QK_ROPE) ** -0.5,
                       causal=True, name="mla_attention")
        mixed = _merge((o_dsa, o_sb, o_nsa, o_mla), z, w_br, layer, tm=512, tn=512)
        h, hb = _matmul_res_ln(mixed, w_o, h, g1, b1, layer, tm=256, name="out_proj_ln")

        q_x = _matmul(hb, xq, layer, tm=1024, tn=512, out_dtype=BF16, name="xa_q_proj")
        k_x = _matmul(mem_b, xk, layer, tm=1024, tn=512, out_dtype=BF16, name="xa_k_proj")
        v_x = _matmul(mem_b, xv, layer, tm=1024, tn=512, out_dtype=BF16, name="xa_v_proj")
        o_x = _flash(q_x, k_x, v_x, dk=HEAD_DIM, kv_len=MEM_LEN, scale=HEAD_DIM ** -0.5, causal=False,
                     name="cross_attention")
        h, hb = _matmul_res_ln(o_x, xo, h, g2, b2, layer, tm=256, name="xa_out_proj_ln")

        i = layer // 2
        if layer % 2 == 0:
            dense_te = jnp.full((TOKENS // FFN_TM,), i, I32)
            y = _grouped_swiglu(dense_te, dense_nv, hb, ffn_w_gate, ffn_w_up, ffn_w_down, tf=256,
                                name="dense_swiglu")
            h, hb = _res_ln(y, h, g3, b3, layer, tm=256, name="ffn_res_ln")
        else:
            meta, counts = _router(h, router_p[i], lower)
            rows, row_token, te, nv = _moe_layout(meta, counts)
            y = _gathered_swiglu(te + i * N_EXPERTS, nv, row_token, h, moe_wg, moe_wu, moe_wd, tf=256,
                                 name="moe_swiglu")
            h, hb = _combine(rows, y, h, meta, g3, b3, layer)
    return h.reshape(BATCH, SEQ, D_MODEL)
```

```python
import functools
import math

import jax
import jax.numpy as jnp
import numpy as np
from jax import lax
from jax.experimental import pallas as pl
from jax.experimental.pallas import tpu as pltpu

F32 = jnp.float32
BF16 = jnp.bfloat16
I32 = jnp.int32

D_MODEL = 2048
BATCH = 4
SEQ = 2048
DEPTH = 4
TOKENS = BATCH * SEQ
MEM_LEN = 256
HEAD_DIM = 128
N_HEADS = 4
DSA_TOPK = min(256, SEQ // 4)
IDX_HEADS = 16
IDX_DIM = 64
CMP_LEN = 32
CMP_STRIDE = 16
CMP_HIDDEN = 256
N_CMP = (SEQ - CMP_LEN) // CMP_STRIDE + 1
SLC_LEN = 64
SLC_SHIFT = 6
N_SLC = SEQ // SLC_LEN
N_SEL = min(16, N_SLC)
WINDOW = 512
FORCE_SCORE = 1.0e4
Q_LORA = 512
KV_LORA = 512
QK_NOPE = 128
QK_ROPE = 64
ROPE_BASE = 10000.0
N_MIXERS = 4
MIX_WIDTH = 512
REL_BUCKETS = 32
REL_MAX_DIST = 128
D_FF = 5632
N_EXPERTS = 8
DN_ALPHA = (2 * DEPTH) ** 0.25
LN_EPS = 1e-5
RMS_EPS = 1e-6
NEG_BIG = -1.0e30

IN_SPLITS = (
    ('dsa_q', 512), ('dsa_k', 128), ('dsa_v', 128),
    ('idx_q', IDX_HEADS * IDX_DIM), ('idx_k', IDX_DIM), ('idx_w', IDX_HEADS),
    ('sb_q', 512), ('sb_k', 512), ('sb_v', 512),
    ('nsa_q', 512),
    ('nsa_kc', 128), ('nsa_vc', 128), ('nsa_ks', 128), ('nsa_vs', 128),
    ('nsa_kw', 128), ('nsa_vw', 128), ('nsa_g', 12),
    ('mla_cq', Q_LORA), ('mla_ckv', KV_LORA), ('mla_kr', QK_ROPE),
    ('gates', N_MIXERS * D_MODEL),
)

LANES = 128
Q_TILE = 128
K_CHUNK = 256
N_CHUNKS = SEQ // K_CHUNK
VMEM_LIMIT_BYTES = 56 * 1024 * 1024
INT_MIN = -2147483648
KEY_NEG_INF = -2139095041

_PACKED = (
    ('idx_q', 1024), ('dsa_q', 512), ('sb_q', 512), ('sb_k', 512), ('sb_v', 512), ('nsa_q', 512),
    ('mla_cq', 512), ('mla_ckv', 512),
    ('dsa_k', 128), ('dsa_v', 128), ('idx_k_lo', 128), ('idx_k_hi', 128), ('misc', 128),
    ('nsa_kc', 128), ('nsa_vc', 128), ('nsa_ks', 128), ('nsa_vs', 128), ('nsa_kw', 128), ('nsa_vw', 128),
    ('mla_kr', 128), ('gates', 8192),
)
_OFF = {}
_o = 0
for _n, _w in _PACKED:
    _OFF[_n] = _o
    _o += _w
Z_WIDTH = _o
MISC_IDXW = 0
MISC_NSAG = 16


def _cparams(n_axes, vmem=None):
    return pltpu.CompilerParams(dimension_semantics=("arbitrary",) * n_axes, vmem_limit_bytes=vmem)


def _dot(a, b):
    return jnp.dot(a, b, preferred_element_type=F32)


def _dot_nt(a, b):
    return lax.dot_general(a, b, (((1,), (1,)), ((), ())), preferred_element_type=F32)


def _layer_norm_rows(v, g, b):
    mu = jnp.mean(v, axis=-1, keepdims=True)
    d = v - mu
    var = jnp.mean(d * d, axis=-1, keepdims=True)
    return d * lax.rsqrt(var + LN_EPS) * g + b


def _matmul_kernel(x_ref, w_ref, o_ref):
    o_ref[...] = _dot(x_ref[...].astype(BF16), w_ref[...].astype(BF16)).astype(o_ref.dtype)


def _matmul(x, w, layer, *, tm, tn, out_dtype, name):
    m, k = x.shape
    n = w.shape[2]
    return pl.pallas_call(
        _matmul_kernel,
        out_shape=jax.ShapeDtypeStruct((m, n), out_dtype),
        grid=(n // tn, m // tm),
        in_specs=[pl.BlockSpec((tm, k), lambda j, i: (i, 0)),
                  pl.BlockSpec((None, k, tn), lambda j, i: (layer, 0, j))],
        out_specs=pl.BlockSpec((tm, tn), lambda j, i: (i, j)),
        compiler_params=_cparams(2, VMEM_LIMIT_BYTES),
        name=name,
    )(x, w)


def _matmul_res_ln_kernel(x_ref, w_ref, h_ref, g_ref, b_ref, o_ref, ob_ref):
    y = _dot(x_ref[...], w_ref[...])
    out = _layer_norm_rows(DN_ALPHA * h_ref[...] + y, g_ref[...], b_ref[...])
    o_ref[...] = out
    ob_ref[...] = out.astype(BF16)


def _matmul_res_ln(x, w, h, g, b, layer, *, tm, name):
    m, k = x.shape
    d = w.shape[2]
    return pl.pallas_call(
        _matmul_res_ln_kernel,
        out_shape=(jax.ShapeDtypeStruct((m, d), F32), jax.ShapeDtypeStruct((m, d), BF16)),
        grid=(m // tm,),
        in_specs=[pl.BlockSpec((tm, k), lambda i: (i, 0)),
                  pl.BlockSpec((None, k, d), lambda i: (layer, 0, 0)),
                  pl.BlockSpec((tm, d), lambda i: (i, 0)),
                  pl.BlockSpec((None, 1, d), lambda i: (layer, 0, 0)),
                  pl.BlockSpec((None, 1, d), lambda i: (layer, 0, 0))],
        out_specs=(pl.BlockSpec((tm, d), lambda i: (i, 0)), pl.BlockSpec((tm, d), lambda i: (i, 0))),
        compiler_params=_cparams(1, VMEM_LIMIT_BYTES),
        name=name,
    )(x, w, h, g, b)


def _res_ln_kernel(y_ref, h_ref, g_ref, b_ref, o_ref, ob_ref):
    out = _layer_norm_rows(DN_ALPHA * h_ref[...] + y_ref[...], g_ref[...], b_ref[...])
    o_ref[...] = out
    ob_ref[...] = out.astype(BF16)


def _res_ln(y, h, g, b, layer, *, tm, name):
    m, d = h.shape
    return pl.pallas_call(
        _res_ln_kernel,
        out_shape=(jax.ShapeDtypeStruct((m, d), F32), jax.ShapeDtypeStruct((m, d), BF16)),
        grid=(m // tm,),
        in_specs=[pl.BlockSpec((tm, d), lambda i: (i, 0)),
                  pl.BlockSpec((tm, d), lambda i: (i, 0)),
                  pl.BlockSpec((None, 1, d), lambda i: (layer, 0, 0)),
                  pl.BlockSpec((None, 1, d), lambda i: (layer, 0, 0))],
        out_specs=(pl.BlockSpec((tm, d), lambda i: (i, 0)), pl.BlockSpec((tm, d), lambda i: (i, 0))),
        compiler_params=_cparams(1, VMEM_LIMIT_BYTES),
        name=name,
    )(y, h, g, b)


def _merge_kernel(b0, b1, b2, b3, g0, g1, g2, g3, wb_ref, o_ref):
    acc = None
    for n, (br, gr) in enumerate(((b0, g0), (b1, g1), (b2, g2), (b3, g3))):
        y = _dot(br[...], wb_ref[n])
        gy = jax.nn.sigmoid(gr[...].astype(F32)) * y
        acc = gy if acc is None else acc + gy
    o_ref[...] = acc.astype(o_ref.dtype)


def _merge(branches, z, wb, layer, *, tm, tn):
    m = z.shape[0]
    gate_specs = []
    for n in range(N_MIXERS):
        base = (_OFF['gates'] + n * D_MODEL) // tn
        gate_specs.append(pl.BlockSpec((tm, tn), lambda j, i, base=base: (i, base + j)))
    return pl.pallas_call(
        _merge_kernel,
        out_shape=jax.ShapeDtypeStruct((m, D_MODEL), BF16),
        grid=(D_MODEL // tn, m // tm),
        in_specs=[pl.BlockSpec((tm, MIX_WIDTH), lambda j, i: (i, 0))] * N_MIXERS + gate_specs
        + [pl.BlockSpec((None, N_MIXERS, MIX_WIDTH, tn), lambda j, i: (layer, 0, 0, j))],
        out_specs=pl.BlockSpec((tm, tn), lambda j, i: (i, j)),
        compiler_params=_cparams(2, VMEM_LIMIT_BYTES),
        name="branch_merge",
    )(*branches, z, z, z, z, wb)


def _sortable_key(score):
    score = jnp.where(score == 0.0, 0.0, score)
    bits = pltpu.bitcast(score, I32)
    return bits ^ (jnp.right_shift(bits, 31) & 0x7FFFFFFF)


def _kth_largest_key(key_scr, k, n_chunks, thr_scr):
    half = Q_TILE // 2

    def search(n):
        def body(i, lo):
            bit = jnp.left_shift(jnp.int32(1), 31 - i)
            out = []
            for g in range(2):
                rows = slice(g * half, (g + 1) * half)
                cand_u = lo[g] | bit
                cand_s = cand_u ^ INT_MIN
                cnt = jnp.zeros((half, K_CHUNK), F32)
                for c in range(n):
                    cnt = cnt + jnp.where(key_scr[c, rows, :] >= cand_s, 1.0, 0.0)
                tot = jnp.sum(cnt, axis=1, keepdims=True)
                out.append(jnp.where(tot >= float(k), cand_u, lo[g]))
            return tuple(out)

        zero = jnp.zeros((half, 1), I32)
        lo = lax.fori_loop(0, 32, body, (zero, zero))
        return jnp.concatenate(lo, axis=0) ^ INT_MIN

    n_even = jnp.left_shift(jnp.right_shift(n_chunks + 1, 1), 1)
    for n in range(2, N_CHUNKS + 1, 2):
        @pl.when(n_even == n)
        def _():
            thr_scr[...] = search(n)

    return thr_scr[...]


def _break_ties(key_scr, thr, k, tri_ref):
    ge = jnp.zeros((Q_TILE, K_CHUNK), F32)
    gt = jnp.zeros((Q_TILE, K_CHUNK), F32)
    for c in range(N_CHUNKS):
        kc = key_scr[c]
        ge = ge + jnp.where(kc >= thr, 1.0, 0.0)
        gt = gt + jnp.where(kc > thr, 1.0, 0.0)
    n_ge = jnp.sum(ge, axis=1, keepdims=True)
    need = float(k) - jnp.sum(gt, axis=1, keepdims=True)
    tied = (n_ge > float(k)) & (thr > KEY_NEG_INF)

    @pl.when(jnp.max(jnp.where(tied, 1.0, 0.0)) > 0.0)
    def _():
        before = jnp.zeros((Q_TILE, 1), F32)
        for c in range(N_CHUNKS):
            kc = key_scr[c]
            eq = kc == thr
            eq_f = jnp.where(eq, 1.0, 0.0)
            rank = _dot(eq_f.astype(BF16), tri_ref[...]) + before
            drop = tied & eq & (rank > need)
            key_scr[c] = jnp.where(drop, INT_MIN, kc)
            before = before + jnp.sum(eq_f, axis=1, keepdims=True)


def _slab_index(rel):
    return jnp.where(rel == 0, 2, jnp.where(rel == -128, 1, jnp.where(rel == -256, 0, 3)))


def _stack_heads(q):
    return jnp.concatenate([q[:, h * HEAD_DIM:(h + 1) * HEAD_DIM] for h in range(N_HEADS)], axis=0)


def _mqa_masked_attention(qs, k_ref, v_ref, slab_ref, t0, chunks, ok_fn, s_scr, p_scr):
    scale = HEAD_DIM ** -0.5
    width = len(chunks) * K_CHUNK
    values = []
    for n, c in enumerate(chunks):
        cols = slice(n * K_CHUNK, (n + 1) * K_CHUNK)
        start = c * K_CHUNK if isinstance(c, int) else pl.multiple_of(jnp.maximum(c, 0) * K_CHUNK, K_CHUNK)
        madd = jnp.where(ok_fn(c), 0.0, NEG_BIG)
        s4 = _dot_nt(qs, k_ref[pl.ds(start, K_CHUNK), :]) * scale
        values.append(v_ref[pl.ds(start, K_CHUNK), :])
        slab = _slab_index(c * K_CHUNK - t0)
        for h in range(N_HEADS):
            s_scr[h, :, cols] = s4[h * Q_TILE:(h + 1) * Q_TILE] + slab_ref[slab, h] + madd
    norms = []
    for h in range(N_HEADS):
        s = s_scr[h, :, :width]
        p = jnp.exp(s - jnp.max(s, axis=1, keepdims=True))
        norms.append(jnp.sum(p, axis=1, keepdims=True))
        p_scr[h * Q_TILE:(h + 1) * Q_TILE, :width] = p.astype(BF16)
    o4 = _dot(p_scr[:, :width], jnp.concatenate(values, axis=0))
    return [o4[h * Q_TILE:(h + 1) * Q_TILE] / norms[h] for h in range(N_HEADS)]


def _row_pos(t0):
    return t0 + lax.broadcasted_iota(I32, (Q_TILE, K_CHUNK), 0)


def _col_pos(c):
    return c * K_CHUNK + lax.broadcasted_iota(I32, (Q_TILE, K_CHUNK), 1)


_ALL_CHUNKS = list(range(N_CHUNKS))
_ATTN_SCRATCH = [pltpu.VMEM((N_CHUNKS, Q_TILE, K_CHUNK), I32),
                 pltpu.VMEM((Q_TILE, 1), I32),
                 pltpu.VMEM((N_HEADS, Q_TILE, SEQ), F32),
                 pltpu.VMEM((N_HEADS * Q_TILE, SEQ), BF16)]
_SLAB_SPEC = pl.BlockSpec((4, N_HEADS, Q_TILE, K_CHUNK), lambda b, i: (0, 0, 0, 0))
_TRI_SPEC = pl.BlockSpec((K_CHUNK, K_CHUNK), lambda b, i: (0, 0))


def _dsa_kernel(q_ref, iq_ref, misc_ref, k_ref, v_ref, iklo_ref, ikhi_ref, slab_ref, tri_ref, o_ref,
                key_scr, thr_scr, s_scr, p_scr):
    qb = pl.program_id(1)
    t0 = qb * Q_TILE
    n_chunks = qb // 2 + 1
    misc = misc_ref[...].astype(F32)
    row = t0 + lax.broadcasted_iota(I32, (Q_TILE, K_CHUNK), 0)

    key_scr[...] = jnp.full(key_scr.shape, KEY_NEG_INF, I32)

    pairs = jnp.concatenate([iq_ref[:, p * LANES:(p + 1) * LANES] for p in range(IDX_HEADS // 2)], axis=0)

    def score_body(c, carry):
        start = pl.multiple_of(c * K_CHUNK, K_CHUNK)
        s_lo = _dot_nt(pairs, iklo_ref[pl.ds(start, K_CHUNK), :])
        s_hi = _dot_nt(pairs, ikhi_ref[pl.ds(start, K_CHUNK), :])
        acc = jnp.zeros((Q_TILE, K_CHUNK), F32)
        for p in range(IDX_HEADS // 2):
            rows = slice(p * Q_TILE, (p + 1) * Q_TILE)
            w0 = misc[:, MISC_IDXW + 2 * p:MISC_IDXW + 2 * p + 1]
            w1 = misc[:, MISC_IDXW + 2 * p + 1:MISC_IDXW + 2 * p + 2]
            acc = acc + jnp.maximum(s_lo[rows], 0.0) * w0
            acc = acc + jnp.maximum(s_hi[rows], 0.0) * w1
        col = c * K_CHUNK + lax.broadcasted_iota(I32, (Q_TILE, K_CHUNK), 1)
        score = jnp.where(col <= row, acc, -jnp.inf)
        key_scr[c] = _sortable_key(score)
        return carry

    lax.fori_loop(0, n_chunks, score_body, 0)
    thr = _kth_largest_key(key_scr, DSA_TOPK, n_chunks, thr_scr)
    _break_ties(key_scr, thr, DSA_TOPK, tri_ref)

    def ok_fn(c):
        return (key_scr[c] >= thr) & (_col_pos(c) <= row)

    o = _mqa_masked_attention(_stack_heads(q_ref[...]), k_ref, v_ref, slab_ref, t0, _ALL_CHUNKS, ok_fn, s_scr, p_scr)
    o_ref[...] = jnp.concatenate(o, axis=1).astype(o_ref.dtype)


def _zspec_rows(name, width, rows, index_fn):
    cb, rem = divmod(_OFF[name], width)
    assert rem == 0, name
    return pl.BlockSpec((rows, width), functools.partial(index_fn, cb))


def _dsa(z, slabs, tri):
    nq = SEQ // Q_TILE

    def qidx(cb, b, i):
        return (b * nq + i, cb)

    def kidx(cb, b, i):
        return (b, cb)

    return pl.pallas_call(
        _dsa_kernel,
        out_shape=jax.ShapeDtypeStruct((TOKENS, MIX_WIDTH), BF16),
        grid=(BATCH, nq),
        in_specs=[_zspec_rows('dsa_q', 512, Q_TILE, qidx),
                  _zspec_rows('idx_q', 1024, Q_TILE, qidx),
                  _zspec_rows('misc', 128, Q_TILE, qidx),
                  _zspec_rows('dsa_k', 128, SEQ, kidx),
                  _zspec_rows('dsa_v', 128, SEQ, kidx),
                  _zspec_rows('idx_k_lo', 128, SEQ, kidx),
                  _zspec_rows('idx_k_hi', 128, SEQ, kidx),
                  _SLAB_SPEC, _TRI_SPEC],
        out_specs=pl.BlockSpec((Q_TILE, MIX_WIDTH), lambda b, i: (b * nq + i, 0)),
        scratch_shapes=_ATTN_SCRATCH,
        compiler_params=_cparams(2, VMEM_LIMIT_BYTES),
        name="dsa_attention",
    )(z, z, z, z, z, z, z, slabs, tri)


def _nsa_compress_kernel(xk_ref, xv_ref, pek_ref, pev_ref, kw1_ref, kw2_ref, vw1_ref, vw2_ref, ok_ref, ov_ref):
    half = CMP_STRIDE * HEAD_DIM

    def compress(x_ref, pe_ref, w1_ref, w2_ref):
        x = x_ref[...].astype(F32)
        a = _dot((x + pe_ref[:, :half]).astype(BF16), w1_ref[:half, :].astype(BF16))
        b = _dot((x + pe_ref[:, half:]).astype(BF16), w1_ref[half:, :].astype(BF16))
        hid = a + pltpu.roll(b, b.shape[0] - 1, 0)
        return _dot(jax.nn.gelu(hid).astype(BF16), w2_ref[...].astype(BF16))

    ok_ref[...] = compress(xk_ref, pek_ref, kw1_ref, kw2_ref)
    ov_ref[...] = compress(xv_ref, pev_ref, vw1_ref, vw2_ref)


def _nsa_compress(xk, xv, pek, pev, kw1, kw2, vw1, vw2):
    rows = xk.shape[0]
    full = lambda a: pl.BlockSpec(a.shape, lambda i: (0,) * a.ndim)
    args = (xk, xv, pek, pev, kw1, kw2, vw1, vw2)
    return pl.pallas_call(
        _nsa_compress_kernel,
        out_shape=(jax.ShapeDtypeStruct((rows, HEAD_DIM), F32), jax.ShapeDtypeStruct((rows, HEAD_DIM), F32)),
        grid=(1,),
        in_specs=[full(a) for a in args],
        out_specs=(pl.BlockSpec((rows, HEAD_DIM), lambda i: (0, 0)), pl.BlockSpec((rows, HEAD_DIM), lambda i: (0, 0))),
        compiler_params=_cparams(1, VMEM_LIMIT_BYTES),
        name="nsa_compress",
    )(*args)


def _nsa_kernel(q_ref, misc_ref, kc_ref, vc_ref, ks_ref, vs_ref, kw_ref, vw_ref, cov_ref, slab_ref, tri_ref, o_ref,
                key_scr, thr_scr, s_scr, p_scr):
    qb = pl.program_id(1)
    t0 = qb * Q_TILE
    scale = HEAD_DIM ** -0.5
    qs = _stack_heads(q_ref[...])
    misc = misc_ref[...].astype(F32)

    n_idx = lax.broadcasted_iota(I32, (N_HEADS * Q_TILE, LANES), 1)
    t_idx = t0 + (lax.broadcasted_iota(I32, (N_HEADS * Q_TILE, LANES), 0) & (Q_TILE - 1))
    cmp_ok = (CMP_STRIDE * n_idx + CMP_LEN - 1) <= t_idx
    lc = jnp.where(cmp_ok, _dot_nt(qs, kc_ref[...].astype(BF16)) * scale, NEG_BIG)
    e = jnp.exp(lc - jnp.max(lc, axis=1, keepdims=True))
    p_cmp = jnp.where(cmp_ok, e / jnp.sum(e, axis=1, keepdims=True), 0.0)
    o_cmp = _dot(p_cmp.astype(BF16), vc_ref[...].astype(BF16))

    p_sum = p_cmp[0:Q_TILE] + p_cmp[Q_TILE:2 * Q_TILE] + p_cmp[2 * Q_TILE:3 * Q_TILE] + p_cmp[3 * Q_TILE:]
    p_hi = p_sum.astype(BF16)
    p_lo = (p_sum - p_hi.astype(F32)).astype(BF16)
    cur = jnp.right_shift(t0 + lax.broadcasted_iota(I32, (Q_TILE, K_CHUNK), 0), SLC_SHIFT)
    for c in range(N_CHUNKS):
        cov = cov_ref[:, c * K_CHUNK:(c + 1) * K_CHUNK]
        imp = _dot(p_hi, cov) + _dot(p_lo, cov)
        jb = jnp.right_shift(c * K_CHUNK + lax.broadcasted_iota(I32, (Q_TILE, K_CHUNK), 1), SLC_SHIFT)
        forced = (jb == 0) | (jb == cur) | (jb == cur - 1)
        imp = jnp.where(jb <= cur, imp + jnp.where(forced, FORCE_SCORE, 0.0), -jnp.inf)
        key_scr[c] = _sortable_key(imp)
    thr = _kth_largest_key(key_scr, N_SEL * SLC_LEN, qb // 2 + 1, thr_scr)
    _break_ties(key_scr, thr, N_SEL * SLC_LEN, tri_ref)
    row = _row_pos(t0)

    def sel_ok(c):
        return (key_scr[c] >= thr) & (_col_pos(c) <= row)

    o_slc = _mqa_masked_attention(qs, ks_ref, vs_ref, slab_ref, t0, _ALL_CHUNKS, sel_ok, s_scr, p_scr)

    def win_ok(c):
        col = _col_pos(c)
        dist = row - col
        return (dist >= 0) & (dist < WINDOW) & (col >= 0)

    win_chunks = [qb // 2 - 2 + n for n in range(3)]
    o_win = _mqa_masked_attention(qs, kw_ref, vw_ref, slab_ref, t0, win_chunks, win_ok, s_scr, p_scr)

    outs = []
    for h in range(N_HEADS):
        g = jax.nn.sigmoid(misc[:, MISC_NSAG + 3 * h:MISC_NSAG + 3 * h + 3])
        rows = slice(h * Q_TILE, (h + 1) * Q_TILE)
        outs.append(g[:, 0:1] * o_cmp[rows] + g[:, 1:2] * o_slc[h] + g[:, 2:3] * o_win[h])
    o_ref[...] = jnp.concatenate(outs, axis=1).astype(o_ref.dtype)


def _nsa(z, k_cmp, v_cmp, cov, slabs, tri):
    nq = SEQ // Q_TILE

    def qidx(cb, b, i):
        return (b * nq + i, cb)

    def kidx(cb, b, i):
        return (b, cb)

    cmp_spec = pl.BlockSpec((LANES, HEAD_DIM), lambda b, i: (b, 0))
    return pl.pallas_call(
        _nsa_kernel,
        out_shape=jax.ShapeDtypeStruct((TOKENS, MIX_WIDTH), BF16),
        grid=(BATCH, nq),
        in_specs=[_zspec_rows('nsa_q', 512, Q_TILE, qidx),
                  _zspec_rows('misc', 128, Q_TILE, qidx),
                  cmp_spec, cmp_spec,
                  _zspec_rows('nsa_ks', 128, SEQ, kidx),
                  _zspec_rows('nsa_vs', 128, SEQ, kidx),
                  _zspec_rows('nsa_kw', 128, SEQ, kidx),
                  _zspec_rows('nsa_vw', 128, SEQ, kidx),
                  pl.BlockSpec((LANES, SEQ), lambda b, i: (0, 0)),
                  _SLAB_SPEC, _TRI_SPEC],
        out_specs=pl.BlockSpec((Q_TILE, MIX_WIDTH), lambda b, i: (b * nq + i, 0)),
        scratch_shapes=_ATTN_SCRATCH,
        compiler_params=_cparams(2, VMEM_LIMIT_BYTES),
        name="nsa_attention",
    )(z, z, k_cmp, v_cmp, z, z, z, z, cov, slabs, tri)


SB_TILE = 256
SB_HEADS_PER_STEP = 2


def _sb_kernel(q_ref, k_ref, v_ref, u_ref, o_ref):
    qb = pl.program_id(2)
    scale = HEAD_DIM ** -0.5
    upper = u_ref[...]
    row = qb * SB_TILE + lax.broadcasted_iota(I32, (SB_TILE, SB_TILE), 0)
    lane = lax.broadcasted_iota(I32, (SB_TILE, SB_TILE), 1)

    def body(i, carry):
        c = qb - i
        start = pl.multiple_of(c * SB_TILE, SB_TILE)
        strict = (c * SB_TILE + lane) < row
        out = []
        for g in range(SB_HEADS_PER_STEP):
            later, acc = carry[g]
            cols = slice(g * HEAD_DIM, (g + 1) * HEAD_DIM)
            zl = _dot_nt(q_ref[:, cols], k_ref[pl.ds(start, SB_TILE), cols]) * scale
            log_beta = jnp.minimum(zl, 0.0) - jnp.log1p(jnp.exp(-jnp.abs(zl)))
            log_keep = jnp.where(strict, log_beta - zl, 0.0)
            keep_hi = log_keep.astype(BF16)
            keep_lo = (log_keep - keep_hi.astype(F32)).astype(BF16)
            within = _dot(keep_hi, upper) + _dot(keep_lo, upper)
            a = jnp.where(strict, jnp.exp(log_beta + within + later), 0.0)
            acc = acc + _dot(a.astype(BF16), v_ref[pl.ds(start, SB_TILE), cols])
            later = later + jnp.sum(log_keep, axis=1, keepdims=True)
            out.append((later, acc))
        return tuple(out)

    init = tuple((jnp.zeros((SB_TILE, 1), F32), jnp.zeros((SB_TILE, HEAD_DIM), F32))
                 for _ in range(SB_HEADS_PER_STEP))
    res = lax.fori_loop(0, qb + 1, body, init)
    o_ref[...] = jnp.concatenate([acc for _, acc in res], axis=1).astype(o_ref.dtype)


def _stick_breaking(z, upper):
    nq = SEQ // SB_TILE
    width = SB_HEADS_PER_STEP * HEAD_DIM
    qcb, kcb, vcb = _OFF['sb_q'] // width, _OFF['sb_k'] // width, _OFF['sb_v'] // width
    return pl.pallas_call(
        _sb_kernel,
        out_shape=jax.ShapeDtypeStruct((TOKENS, MIX_WIDTH), BF16),
        grid=(BATCH, N_HEADS // SB_HEADS_PER_STEP, nq),
        in_specs=[pl.BlockSpec((SB_TILE, width), lambda b, h, i: (b * nq + i, qcb + h)),
                  pl.BlockSpec((SEQ, width), lambda b, h, i: (b, kcb + h)),
                  pl.BlockSpec((SEQ, width), lambda b, h, i: (b, vcb + h)),
                  pl.BlockSpec((SB_TILE, SB_TILE), lambda b, h, i: (0, 0))],
        out_specs=pl.BlockSpec((SB_TILE, width), lambda b, h, i: (b * nq + i, h)),
        compiler_params=_cparams(3, VMEM_LIMIT_BYTES),
        name="stick_breaking_attention",
    )(z, z, z, upper)


MLA_QK = 2 * HEAD_DIM


def _rope128(x, cos, nsin, psin):
    return x * cos + pltpu.roll(x, 96, 1) * nsin + pltpu.roll(x, 32, 1) * psin


def _mla_prep_kernel(cq_ref, ckv_ref, kr_ref, qn_ref, kvn_ref, wqn_ref, wqp_ref, wuk_ref, wuv_ref,
                     cos_ref, nsin_ref, psin_ref, q_ref, k_ref, v_ref):
    def rms(x, g):
        return (x * lax.rsqrt(jnp.mean(x * x, axis=-1, keepdims=True) + RMS_EPS) * g).astype(BF16)

    cos, nsin, psin = cos_ref[...], nsin_ref[...], psin_ref[...]
    xq = rms(cq_ref[...].astype(F32), qn_ref[...])
    xc = rms(ckv_ref[...].astype(F32), kvn_ref[...])
    q_nope = _dot(xq, wqn_ref[...])
    q_rope = _dot(xq, wqp_ref[...])
    k_nope = _dot(xc, wuk_ref[...])
    k_rope = _rope128(kr_ref[...].astype(F32), cos, nsin, psin)
    q_parts, k_parts = [], []
    for h in range(N_HEADS):
        cols = slice(h * HEAD_DIM, (h + 1) * HEAD_DIM)
        q_parts += [q_nope[:, cols], _rope128(q_rope[:, cols], cos, nsin, psin)]
        k_parts += [k_nope[:, cols], k_rope]
    q_ref[...] = jnp.concatenate(q_parts, axis=1).astype(BF16)
    k_ref[...] = jnp.concatenate(k_parts, axis=1).astype(BF16)
    v_ref[...] = _dot(xc, wuv_ref[...]).astype(BF16)


def _mla_prep(z, qn, kvn, wqn, wqp, wuk, wuv, cos, nsin, psin, *, tm):
    nt = SEQ // tm

    def zidx(cb, i):
        return (i, cb)

    wspec = pl.BlockSpec((Q_LORA, 512), lambda i: (0, 0))
    nspec = pl.BlockSpec((1, 512), lambda i: (0, 0))
    tspec = pl.BlockSpec((tm, LANES), lambda i: (i % nt, 0))
    return pl.pallas_call(
        _mla_prep_kernel,
        out_shape=(jax.ShapeDtypeStruct((TOKENS, N_HEADS * MLA_QK), BF16),
                   jax.ShapeDtypeStruct((TOKENS, N_HEADS * MLA_QK), BF16),
                   jax.ShapeDtypeStruct((TOKENS, N_HEADS * HEAD_DIM), BF16)),
        grid=(TOKENS // tm,),
        in_specs=[_zspec_rows('mla_cq', 512, tm, zidx), _zspec_rows('mla_ckv', 512, tm, zidx),
                  _zspec_rows('mla_kr', 128, tm, zidx), nspec, nspec, wspec, wspec, wspec, wspec,
                  tspec, tspec, tspec],
        out_specs=(pl.BlockSpec((tm, N_HEADS * MLA_QK), lambda i: (i, 0)),
                   pl.BlockSpec((tm, N_HEADS * MLA_QK), lambda i: (i, 0)),
                   pl.BlockSpec((tm, N_HEADS * HEAD_DIM), lambda i: (i, 0))),
        compiler_params=_cparams(1, VMEM_LIMIT_BYTES),
        name="mla_prep",
    )(z, z, z, qn, kvn, wqn, wqp, wuk, wuv, cos, nsin, psin)


FLASH_TILE = 256


def _flash_kernel(q_ref, k_ref, v_ref, o_ref, s_scr, *, scale, causal, n_kv):
    qb = pl.program_id(2)
    q = q_ref[...]
    row = qb * FLASH_TILE + lax.broadcasted_iota(I32, (FLASH_TILE, FLASH_TILE), 0)
    lane = lax.broadcasted_iota(I32, (FLASH_TILE, FLASH_TILE), 1)
    for c in range(n_kv):
        cols = slice(c * FLASH_TILE, (c + 1) * FLASH_TILE)
        s = _dot_nt(q, k_ref[cols, :]) * scale
        if causal:
            s = jnp.where(c * FLASH_TILE + lane <= row, s, NEG_BIG)
        s_scr[:, cols] = s
    s = s_scr[...]
    p = jnp.exp(s - jnp.max(s, axis=1, keepdims=True))
    o = _dot(p.astype(BF16), v_ref[...]) / jnp.sum(p, axis=1, keepdims=True)
    o_ref[...] = o.astype(o_ref.dtype)


def _flash(q, k, v, *, dk, kv_len, scale, causal, name):
    nq = SEQ // FLASH_TILE
    return pl.pallas_call(
        functools.partial(_flash_kernel, scale=scale, causal=causal, n_kv=kv_len // FLASH_TILE),
        out_shape=jax.ShapeDtypeStruct((TOKENS, N_HEADS * HEAD_DIM), BF16),
        grid=(BATCH, N_HEADS, nq),
        in_specs=[pl.BlockSpec((FLASH_TILE, dk), lambda b, h, i: (b * nq + i, h)),
                  pl.BlockSpec((kv_len, dk), lambda b, h, i: (b, h)),
                  pl.BlockSpec((kv_len, HEAD_DIM), lambda b, h, i: (b, h))],
        out_specs=pl.BlockSpec((FLASH_TILE, HEAD_DIM), lambda b, h, i: (b * nq + i, h)),
        scratch_shapes=[pltpu.VMEM((FLASH_TILE, kv_len), F32)],
        compiler_params=_cparams(3, VMEM_LIMIT_BYTES),
        name=name,
    )(q, k, v)


FFN_TM = 1024
FFN_SUB = 256
FFN_SUB_SHIFT = 8


def _swiglu_step(load_x, nv, j, wg_ref, wu_ref, wd_ref, o_ref, wg_b, wu_b, wd_b):
    wg_b[...] = wg_ref[0].astype(BF16)
    wu_b[...] = wu_ref[0].astype(BF16)
    wd_b[...] = wd_ref[0].astype(BF16)

    @pl.when(j == 0)
    def _():
        o_ref[...] = jnp.zeros(o_ref.shape, F32)

    def accumulate(rows):
        x = load_x(rows)
        g = _dot(x, wg_b[...])
        u = _dot(x, wu_b[...])
        o_ref[rows, :] += _dot((g * jax.nn.sigmoid(g) * u).astype(BF16), wd_b[...])

    n_sub = jnp.right_shift(nv + (FFN_SUB - 1), FFN_SUB_SHIFT)
    for n in range(1, FFN_TM // FFN_SUB + 1):
        @pl.when(n_sub == n)
        def _():
            for s in range(n):
                accumulate(slice(s * FFN_SUB, (s + 1) * FFN_SUB))


def _ffn_kernel(te_ref, nv_ref, x_ref, wg_ref, wu_ref, wd_ref, o_ref, wg_b, wu_b, wd_b):
    del te_ref
    nv = nv_ref[pl.program_id(0)]
    _swiglu_step(lambda rows: x_ref[rows, :], nv, pl.program_id(1), wg_ref, wu_ref, wd_ref, o_ref, wg_b, wu_b, wd_b)


def _moe_ffn_kernel(te_ref, nv_ref, tok_ref, h_hbm, wg_ref, wu_ref, wd_ref, o_ref, x_buf, wg_b, wu_b, wd_b, sem):
    del te_ref
    i = pl.program_id(0)
    j = pl.program_id(1)
    nv = nv_ref[i]

    @pl.when((j == 0) & (nv > 0))
    def _():
        n_rows = jnp.left_shift(jnp.right_shift(nv + (FFN_SUB - 1), FFN_SUB_SHIFT), FFN_SUB_SHIFT)

        def row_copy(tok, r):
            return pltpu.make_async_copy(h_hbm.at[pl.ds(tok, 1)], x_buf.at[pl.ds(r, 1)], sem)

        def start(r, carry):
            row_copy(tok_ref[i * FFN_TM + r], r).start()
            return carry

        lax.fori_loop(0, n_rows, start, 0)

        def wait(r, carry):
            row_copy(0, 0).wait()
            return carry

        lax.fori_loop(0, n_rows, wait, 0)

    _swiglu_step(lambda rows: x_buf[rows, :].astype(BF16), nv, j, wg_ref, wu_ref, wd_ref, o_ref, wg_b, wu_b, wd_b)


def _ffn_specs(tf, n_prefetch):
    nff = D_FF // tf

    def jj(i, j, nv):
        return jnp.where(nv[i] > 0, j, nff - 1)

    w_in = pl.BlockSpec((1, D_MODEL, tf), lambda i, j, te, nv, *_: (te[i], 0, jj(i, j, nv)))
    w_out = pl.BlockSpec((1, tf, D_MODEL), lambda i, j, te, nv, *_: (te[i], jj(i, j, nv), 0))
    out = pl.BlockSpec((FFN_TM, D_MODEL), lambda i, j, *_: (i, 0))
    scratch = [pltpu.VMEM((D_MODEL, tf), BF16), pltpu.VMEM((D_MODEL, tf), BF16), pltpu.VMEM((tf, D_MODEL), BF16)]
    return nff, w_in, w_out, out, scratch


def _grouped_swiglu(tile_expert, tile_valid, x, wg, wu, wd, *, tf, name):
    r = x.shape[0]
    nff, w_in, w_out, out, scratch = _ffn_specs(tf, 2)
    grid_spec = pltpu.PrefetchScalarGridSpec(
        num_scalar_prefetch=2,
        grid=(r // FFN_TM, nff),
        in_specs=[pl.BlockSpec((FFN_TM, D_MODEL), lambda i, j, *_: (i, 0)), w_in, w_in, w_out],
        out_specs=out,
        scratch_shapes=scratch,
    )
    return pl.pallas_call(
        _ffn_kernel,
        out_shape=jax.ShapeDtypeStruct((r, D_MODEL), F32),
        grid_spec=grid_spec,
        compiler_params=_cparams(2, VMEM_LIMIT_BYTES),
        name=name,
    )(tile_expert, tile_valid, x, wg, wu, wd)


def _gathered_swiglu(tile_expert, tile_valid, row_token, h, wg, wu, wd, *, tf, name):
    r = row_token.shape[0]
    nff, w_in, w_out, out, scratch = _ffn_specs(tf, 3)
    grid_spec = pltpu.PrefetchScalarGridSpec(
        num_scalar_prefetch=3,
        grid=(r // FFN_TM, nff),
        in_specs=[pl.BlockSpec(memory_space=pl.ANY), w_in, w_in, w_out],
        out_specs=out,
        scratch_shapes=[pltpu.VMEM((FFN_TM, D_MODEL), F32)] + scratch + [pltpu.SemaphoreType.DMA(())],
    )
    return pl.pallas_call(
        _moe_ffn_kernel,
        out_shape=jax.ShapeDtypeStruct((r, D_MODEL), F32),
        grid_spec=grid_spec,
        compiler_params=_cparams(2, VMEM_LIMIT_BYTES),
        name=name,
    )(tile_expert, tile_valid, row_token, h, wg, wu, wd)


ROUTER_TM = 256
META_E0, META_E1, META_W0, META_W1, META_R0, META_R1 = range(6)


def _router_kernel(h_ref, r_ref, lt_ref, meta_ref, cnt_ref, run_scr):
    i = pl.program_id(0)

    @pl.when(i == 0)
    def _():
        run_scr[...] = jnp.zeros(run_scr.shape, F32)

    def split(x):
        hi = x.astype(BF16)
        return hi, (x - hi.astype(F32)).astype(BF16)

    h_hi, h_lo = split(h_ref[...])
    r_hi, r_lo = split(r_ref[...])
    logits = _dot(h_hi, r_hi) + (_dot(h_hi, r_lo) + _dot(h_lo, r_hi))
    lane = lax.broadcasted_iota(I32, (ROUTER_TM, LANES), 1).astype(F32)
    logits = jnp.where(lane < N_EXPERTS, logits, -jnp.inf)

    def top1(x):
        m = jnp.max(x, axis=1, keepdims=True)
        idx = jnp.min(jnp.where(x == m, lane, float(LANES)), axis=1, keepdims=True)
        return m, idx

    m0, e0 = top1(logits)
    m1, e1 = top1(jnp.where(lane == e0, -jnp.inf, logits))
    ex = jnp.exp(m1 - m0)
    w0 = 1.0 / (1.0 + ex)
    w1 = ex / (1.0 + ex)

    hot0 = jnp.where(lane == e0, 1.0, 0.0)
    hot1 = jnp.where(lane == e1, 1.0, 0.0)
    before0 = _dot(lt_ref[...], hot0.astype(BF16)) + run_scr[0:1, :]
    tot0 = jnp.sum(hot0, axis=0, keepdims=True)
    before1 = _dot(lt_ref[...], hot1.astype(BF16)) + run_scr[0:1, :] + tot0
    r0 = jnp.sum(hot0 * before0, axis=1, keepdims=True)
    r1 = jnp.sum(hot1 * before1, axis=1, keepdims=True)
    run_new = run_scr[0:1, :] + tot0 + jnp.sum(hot1, axis=0, keepdims=True)
    run_scr[...] = jnp.broadcast_to(run_new, run_scr.shape)
    cnt_ref[...] = jnp.broadcast_to(run_new, cnt_ref.shape)

    meta = jnp.zeros((ROUTER_TM, LANES), F32)
    for ln, val in ((META_E0, e0), (META_E1, e1), (META_W0, w0), (META_W1, w1),
                    (META_R0, r0), (META_R1, r1)):
        meta = jnp.where(lane == ln, val, meta)
    meta_ref[...] = meta


def _router(h, router_padded, lower_tri):
    return pl.pallas_call(
        _router_kernel,
        out_shape=(jax.ShapeDtypeStruct((TOKENS, LANES), F32), jax.ShapeDtypeStruct((8, LANES), F32)),
        grid=(TOKENS // ROUTER_TM,),
        in_specs=[pl.BlockSpec((ROUTER_TM, D_MODEL), lambda i: (i, 0)),
                  pl.BlockSpec((D_MODEL, LANES), lambda i: (0, 0)),
                  pl.BlockSpec((ROUTER_TM, ROUTER_TM), lambda i: (0, 0))],
        out_specs=(pl.BlockSpec((ROUTER_TM, LANES), lambda i: (i, 0)), pl.BlockSpec((8, LANES), lambda i: (0, 0))),
        scratch_shapes=[pltpu.VMEM((8, LANES), F32)],
        compiler_params=_cparams(1, VMEM_LIMIT_BYTES),
        name="moe_router",
    )(h, router_padded, lower_tri)


COMBINE_TM = 256


def _combine_kernel(rows_ref, y_hbm, h_ref, meta_ref, g_ref, b_ref, o_ref, ob_ref, buf, sem):
    i = pl.program_id(0)

    def row_copy(src, slot, r):
        return pltpu.make_async_copy(y_hbm.at[pl.ds(src, 1)], buf.at[slot, pl.ds(r, 1)], sem)

    def start(r, carry):
        tok = i * COMBINE_TM + r
        row_copy(rows_ref[tok], 0, r).start()
        row_copy(rows_ref[TOKENS + tok], 1, r).start()
        return carry

    lax.fori_loop(0, COMBINE_TM, start, 0)

    def wait(r, carry):
        row_copy(0, 0, 0).wait()
        row_copy(0, 1, 0).wait()
        return carry

    lax.fori_loop(0, COMBINE_TM, wait, 0)
    meta = meta_ref[...]
    f = meta[:, META_W0:META_W0 + 1] * buf[0] + meta[:, META_W1:META_W1 + 1] * buf[1]
    out = _layer_norm_rows(DN_ALPHA * h_ref[...] + f, g_ref[...], b_ref[...])
    o_ref[...] = out
    ob_ref[...] = out.astype(BF16)


def _combine(rows, y, h, meta, g, b, layer):
    grid_spec = pltpu.PrefetchScalarGridSpec(
        num_scalar_prefetch=1,
        grid=(TOKENS // COMBINE_TM,),
        in_specs=[pl.BlockSpec(memory_space=pl.ANY),
                  pl.BlockSpec((COMBINE_TM, D_MODEL), lambda i, rows: (i, 0)),
                  pl.BlockSpec((COMBINE_TM, LANES), lambda i, rows: (i, 0)),
                  pl.BlockSpec((None, 1, D_MODEL), lambda i, rows: (layer, 0, 0)),
                  pl.BlockSpec((None, 1, D_MODEL), lambda i, rows: (layer, 0, 0))],
        out_specs=(pl.BlockSpec((COMBINE_TM, D_MODEL), lambda i, rows: (i, 0)),
                   pl.BlockSpec((COMBINE_TM, D_MODEL), lambda i, rows: (i, 0))),
        scratch_shapes=[pltpu.VMEM((2, COMBINE_TM, D_MODEL), F32), pltpu.SemaphoreType.DMA(())],
    )
    return pl.pallas_call(
        _combine_kernel,
        out_shape=(jax.ShapeDtypeStruct((TOKENS, D_MODEL), F32), jax.ShapeDtypeStruct((TOKENS, D_MODEL), BF16)),
        grid_spec=grid_spec,
        compiler_params=_cparams(1, VMEM_LIMIT_BYTES),
        name="moe_combine",
    )(rows, y, h, meta, g, b)


def _pack_w_in(w_in):
    src = {}
    off = 0
    for name, width in IN_SPLITS:
        src[name] = (off, off + width)
        off += width
    lead = w_in.shape[:-1]

    def cols(name):
        a, b = src[name]
        return w_in[..., a:b]

    def zeros(width):
        return jnp.zeros(lead + (width,), w_in.dtype)

    pieces = []
    for name, width in _PACKED:
        if name == 'idx_k_lo':
            pieces += [cols('idx_k'), zeros(64)]
        elif name == 'idx_k_hi':
            pieces += [zeros(64), cols('idx_k')]
        elif name == 'misc':
            pieces += [cols('idx_w'), cols('nsa_g'), zeros(128 - 28)]
        elif name == 'mla_kr':
            pieces += [cols('mla_kr'), zeros(64)]
        else:
            pieces.append(cols(name))
    return jnp.concatenate(pieces, axis=-1).astype(BF16)


def _t5_bucket_np(dist):
    exact = REL_BUCKETS // 2
    d = np.maximum(dist, 0)
    log_ratio = np.log(np.maximum(d, 1).astype(np.float32) / exact) / math.log(REL_MAX_DIST / exact)
    far = np.minimum(exact + (log_ratio * (REL_BUCKETS - exact)).astype(np.int32), REL_BUCKETS - 1)
    return np.where(d < exact, d, far).astype(np.int32)


def _bias_slabs(table4):
    i = np.arange(Q_TILE)[:, None]
    j = np.arange(K_CHUNK)[None, :]
    bucket = jnp.asarray(np.stack([_t5_bucket_np(i - j - rel) for rel in (-256, -128, 0, -2 * SEQ)]))
    tab = table4.astype(F32)
    out = jnp.zeros((4, N_HEADS, Q_TILE, K_CHUNK), F32)
    for b in range(REL_BUCKETS):
        out = jnp.where((bucket == b)[:, None], tab[b][None, :, None, None], out)
    return out


def _rope_tables():
    half = QK_ROPE // 2
    inv = ROPE_BASE ** (-jnp.arange(half, dtype=F32) / half)
    ang = jnp.arange(SEQ, dtype=F32)[:, None] * inv[None, :]
    cos, sin = jnp.cos(ang), jnp.sin(ang)
    zero = jnp.zeros_like(cos)
    pad = jnp.zeros((SEQ, LANES - QK_ROPE), F32)
    cos_t = jnp.concatenate([cos, cos, pad], axis=1)
    nsin_t = jnp.concatenate([-sin, zero, pad], axis=1)
    psin_t = jnp.concatenate([zero, sin, pad], axis=1)
    return cos_t, nsin_t, psin_t


def _cover_expanded():
    n = np.arange(LANES)[:, None]
    s = np.arange(SEQ)[None, :]
    j = s // SLC_LEN
    cover = (CMP_STRIDE * n < SLC_LEN * j + SLC_LEN) & (CMP_STRIDE * n + CMP_LEN > SLC_LEN * j) & (n < N_CMP)
    return jnp.asarray(cover, BF16)


def _strict_upper(n):
    return jnp.asarray(np.arange(n)[:, None] > np.arange(n)[None, :], BF16)


def _strict_lower(n):
    return jnp.asarray(np.arange(n)[None, :] < np.arange(n)[:, None], BF16)


def _moe_layout(meta, counts):
    n_tiles = 2 * TOKENS // FFN_TM + N_EXPERTS
    cnt = counts[0, :N_EXPERTS].astype(I32)
    padded = ((cnt + FFN_TM - 1) // FFN_TM) * FFN_TM
    ends = jnp.cumsum(padded)
    offs = ends - padded
    e = meta[:, META_E0:META_E1 + 1].astype(I32)
    rank = meta[:, META_R0:META_R1 + 1].astype(I32)
    rows = (offs[e] + rank).T.reshape(-1)
    token = jnp.tile(jnp.arange(TOKENS, dtype=I32), 2)
    row_token = jnp.zeros((n_tiles * FFN_TM,), I32).at[rows].set(token)
    tile_start = jnp.arange(n_tiles, dtype=I32) * FFN_TM
    te = jnp.minimum(jnp.sum(tile_start[:, None] >= ends[None, :], axis=1), N_EXPERTS - 1).astype(I32)
    nv = jnp.clip(cnt[te] - (tile_start - offs[te]), 0, FFN_TM).astype(I32)
    return rows, row_token, te, nv


def kernel(x, mem, rel_table, w_in, mla_q_norm, mla_kv_norm, mla_w_uq, mla_w_uk, mla_w_uv, nsa_pe_k, nsa_pe_v,
           nsa_ck_w1, nsa_ck_w2, nsa_cv_w1, nsa_cv_w2, w_branch, w_out, ln1_g, ln1_b, xa_wq, xa_wk, xa_wv, xa_wo,
           ln2_g, ln2_b, ffn_w_gate, ffn_w_up, ffn_w_down, moe_router, moe_w_gate, moe_w_up, moe_w_down,
           ln3_g, ln3_b):
    h = x.reshape(TOKENS, D_MODEL)
    hb = h.astype(BF16)
    mem_b = mem.reshape(BATCH * MEM_LEN, D_MODEL).astype(BF16)

    w_in_p = _pack_w_in(w_in)
    dsa_slabs = _bias_slabs(rel_table[:, :N_HEADS])
    nsa_slabs = _bias_slabs(rel_table[:, N_HEADS:])
    cos_t, nsin_t, psin_t = _rope_tables()
    cover = _cover_expanded()
    upper = _strict_upper(SB_TILE)
    incl_upper = jnp.asarray(np.arange(K_CHUNK)[:, None] <= np.arange(K_CHUNK)[None, :], BF16)
    lower = _strict_lower(ROUTER_TM)

    uq = mla_w_uq.reshape(DEPTH, Q_LORA, N_HEADS, QK_NOPE + QK_ROPE)
    w_qn = uq[..., :QK_NOPE].reshape(DEPTH, Q_LORA, N_HEADS * QK_NOPE).astype(BF16)
    w_qp = jnp.concatenate([uq[..., QK_NOPE:], jnp.zeros((DEPTH, Q_LORA, N_HEADS, LANES - QK_ROPE), F32)],
                           axis=-1).reshape(DEPTH, Q_LORA, N_HEADS * LANES).astype(BF16)
    w_uk = mla_w_uk.astype(BF16)
    w_uv = mla_w_uv.astype(BF16)
    w_br = w_branch.astype(BF16)
    w_o = w_out.astype(BF16)
    xq, xk, xv, xo = (w.astype(BF16) for w in (xa_wq, xa_wk, xa_wv, xa_wo))
    router_p = jnp.pad(moe_router, ((0, 0), (0, 0), (0, LANES - N_EXPERTS)))
    row2 = lambda a: a.reshape(DEPTH, 1, -1)
    g1, b1, g2, b2, g3, b3 = (row2(a) for a in (ln1_g, ln1_b, ln2_g, ln2_b, ln3_g, ln3_b))
    qn, kvn = row2(mla_q_norm), row2(mla_kv_norm)
    pek = nsa_pe_k.reshape(DEPTH, 1, CMP_LEN * HEAD_DIM)
    pev = nsa_pe_v.reshape(DEPTH, 1, CMP_LEN * HEAD_DIM)
    dense_nv = jnp.full((TOKENS // FFN_TM,), FFN_TM, I32)
    moe_wg = moe_w_gate.reshape((-1,) + moe_w_gate.shape[2:])
    moe_wu = moe_w_up.reshape((-1,) + moe_w_up.shape[2:])
    moe_wd = moe_w_down.reshape((-1,) + moe_w_down.shape[2:])

    for layer in range(DEPTH):
        z = _matmul(hb, w_in_p, layer, tm=1024, tn=1024, out_dtype=BF16, name="in_proj")
        o_dsa = _dsa(z, dsa_slabs, incl_upper)
        o_sb = _stick_breaking(z, upper)
        group = lambda name: z[:, _OFF[name]:_OFF[name] + HEAD_DIM].reshape(BATCH * LANES, CMP_STRIDE * HEAD_DIM)
        k_cmp, v_cmp = _nsa_compress(group('nsa_kc'), group('nsa_vc'), pek[layer], pev[layer],
                                     nsa_ck_w1[layer], nsa_ck_w2[layer], nsa_cv_w1[layer], nsa_cv_w2[layer])
        o_nsa = _nsa(z, k_cmp, v_cmp, cover, nsa_slabs, incl_upper)
        q_cat, k_cat, v_mla = _mla_prep(z, qn[layer], kvn[layer], w_qn[layer], w_qp[layer], w_uk[layer],
                                        w_uv[layer], cos_t, nsin_t, psin_t, tm=512)
        o_mla = _flash(q_cat, k_cat, v_mla, dk=MLA_QK, kv_len=SEQ, scale=(QK_NOPE + QK_ROPE) ** -0.5,
                       causal=True, name="mla_attention")
        mixed = _merge((o_dsa, o_sb, o_nsa, o_mla), z, w_br, layer, tm=512, tn=512)
        h, hb = _matmul_res_ln(mixed, w_o, h, g1, b1, layer, tm=256, name="out_proj_ln")

        q_x = _matmul(hb, xq, layer, tm=1024, tn=512, out_dtype=BF16, name="xa_q_proj")
        k_x = _matmul(mem_b, xk, layer, tm=1024, tn=512, out_dtype=BF16, name="xa_k_proj")
        v_x = _matmul(mem_b, xv, layer, tm=1024, tn=512, out_dtype=BF16, name="xa_v_proj")
        o_x = _flash(q_x, k_x, v_x, dk=HEAD_DIM, kv_len=MEM_LEN, scale=HEAD_DIM ** -0.5, causal=False,
                     name="cross_attention")
        h, hb = _matmul_res_ln(o_x, xo, h, g2, b2, layer, tm=256, name="xa_out_proj_ln")

        i = layer // 2
        if layer % 2 == 0:
            dense_te = jnp.full((TOKENS // FFN_TM,), i, I32)
            y = _grouped_swiglu(dense_te, dense_nv, hb, ffn_w_gate, ffn_w_up, ffn_w_down, tf=256,
                                name="dense_swiglu")
            h, hb = _res_ln(y, h, g3, b3, layer, tm=256, name="ffn_res_ln")
        else:
            meta, counts = _router(h, router_p[i], lower)
            rows, row_token, te, nv = _moe_layout(meta, counts)
            y = _gathered_swiglu(te + i * N_EXPERTS, nv, row_token, h, moe_wg, moe_wu, moe_wd, tf=256,
                                 name="moe_swiglu")
            h, hb = _combine(rows, y, h, meta, g3, b3, layer)
    return h.reshape(BATCH, SEQ, D_MODEL)
```

```python
import functools
import math

import jax
import jax.numpy as jnp
import numpy as np
from jax import lax
from jax.experimental import pallas as pl
from jax.experimental.pallas import tpu as pltpu

F32 = jnp.float32
BF16 = jnp.bfloat16
I32 = jnp.int32

D_MODEL = 2048
BATCH = 4
SEQ = 2048
DEPTH = 4
TOKENS = BATCH * SEQ
MEM_LEN = 256
HEAD_DIM = 128
N_HEADS = 4
DSA_TOPK = min(256, SEQ // 4)
IDX_HEADS = 16
IDX_DIM = 64
CMP_LEN = 32
CMP_STRIDE = 16
CMP_HIDDEN = 256
N_CMP = (SEQ - CMP_LEN) // CMP_STRIDE + 1
SLC_LEN = 64
SLC_SHIFT = 6
N_SLC = SEQ // SLC_LEN
N_SEL = min(16, N_SLC)
WINDOW = 512
FORCE_SCORE = 1.0e4
Q_LORA = 512
KV_LORA = 512
QK_NOPE = 128
QK_ROPE = 64
ROPE_BASE = 10000.0
N_MIXERS = 4
MIX_WIDTH = 512
REL_BUCKETS = 32
REL_MAX_DIST = 128
D_FF = 5632
N_EXPERTS = 8
DN_ALPHA = (2 * DEPTH) ** 0.25
LN_EPS = 1e-5
RMS_EPS = 1e-6
NEG_BIG = -1.0e30

IN_SPLITS = (
    ('dsa_q', 512), ('dsa_k', 128), ('dsa_v', 128),
    ('idx_q', IDX_HEADS * IDX_DIM), ('idx_k', IDX_DIM), ('idx_w', IDX_HEADS),
    ('sb_q', 512), ('sb_k', 512), ('sb_v', 512),
    ('nsa_q', 512),
    ('nsa_kc', 128), ('nsa_vc', 128), ('nsa_ks', 128), ('nsa_vs', 128),
    ('nsa_kw', 128), ('nsa_vw', 128), ('nsa_g', 12),
    ('mla_cq', Q_LORA), ('mla_ckv', KV_LORA), ('mla_kr', QK_ROPE),
    ('gates', N_MIXERS * D_MODEL),
)

LANES = 128
Q_TILE = 128
K_CHUNK = 256
N_CHUNKS = SEQ // K_CHUNK
VMEM_LIMIT_BYTES = 56 * 1024 * 1024
INT_MIN = -2147483648
KEY_NEG_INF = -2139095041

_PACKED = (
    ('idx_q', 1024), ('dsa_q', 512), ('sb_q', 512), ('sb_k', 512), ('sb_v', 512), ('nsa_q', 512),
    ('mla_cq', 512), ('mla_ckv', 512),
    ('dsa_k', 128), ('dsa_v', 128), ('idx_k_lo', 128), ('idx_k_hi', 128), ('misc', 128),
    ('nsa_kc', 128), ('nsa_vc', 128), ('nsa_ks', 128), ('nsa_vs', 128), ('nsa_kw', 128), ('nsa_vw', 128),
    ('mla_kr', 128), ('gates', 8192),
)
_OFF = {}
_o = 0
for _n, _w in _PACKED:
    _OFF[_n] = _o
    _o += _w
Z_WIDTH = _o
MISC_IDXW = 0
MISC_NSAG = 16


def _cparams(n_axes, vmem=None):
    return pltpu.CompilerParams(dimension_semantics=("arbitrary",) * n_axes, vmem_limit_bytes=vmem)


def _dot(a, b):
    return jnp.dot(a, b, preferred_element_type=F32)


def _dot_nt(a, b):
    return lax.dot_general(a, b, (((1,), (1,)), ((), ())), preferred_element_type=F32)


def _layer_norm_rows(v, g, b):
    mu = jnp.mean(v, axis=-1, keepdims=True)
    d = v - mu
    var = jnp.mean(d * d, axis=-1, keepdims=True)
    return d * lax.rsqrt(var + LN_EPS) * g + b


def _matmul_kernel(x_ref, w_ref, o_ref):
    o_ref[...] = _dot(x_ref[...].astype(BF16), w_ref[...].astype(BF16)).astype(o_ref.dtype)


def _matmul(x, w, layer, *, tm, tn, out_dtype, name):
    m, k = x.shape
    n = w.shape[2]
    return pl.pallas_call(
        _matmul_kernel,
        out_shape=jax.ShapeDtypeStruct((m, n), out_dtype),
        grid=(n // tn, m // tm),
        in_specs=[pl.BlockSpec((tm, k), lambda j, i: (i, 0)),
                  pl.BlockSpec((None, k, tn), lambda j, i: (layer, 0, j))],
        out_specs=pl.BlockSpec((tm, tn), lambda j, i: (i, j)),
        compiler_params=_cparams(2, VMEM_LIMIT_BYTES),
        name=name,
    )(x, w)


def _matmul_nt_kernel(x_ref, wt_ref, o_ref):
    o_ref[...] = _dot_nt(x_ref[...], wt_ref[...]).astype(o_ref.dtype)


def _matmul_nt(x, wt, layer, *, tm, tn, out_dtype, name):
    m, k = x.shape
    n = wt.shape[1]
    return pl.pallas_call(
        _matmul_nt_kernel,
        out_shape=jax.ShapeDtypeStruct((m, n), out_dtype),
        grid=(n // tn, m // tm),
        in_specs=[pl.BlockSpec((tm, k), lambda j, i: (i, 0)),
                  pl.BlockSpec((None, tn, k), lambda j, i: (layer, j, 0))],
        out_specs=pl.BlockSpec((tm, tn), lambda j, i: (i, j)),
        compiler_params=_cparams(2, VMEM_LIMIT_BYTES),
        name=name,
    )(x, wt)


def _matmul_res_ln_kernel(x_ref, w_ref, h_ref, g_ref, b_ref, o_ref, ob_ref):
    y = _dot(x_ref[...], w_ref[...])
    out = _layer_norm_rows(DN_ALPHA * h_ref[...] + y, g_ref[...], b_ref[...])
    o_ref[...] = out
    ob_ref[...] = out.astype(BF16)


def _matmul_res_ln(x, w, h, g, b, layer, *, tm, name):
    m, k = x.shape
    d = w.shape[2]
    return pl.pallas_call(
        _matmul_res_ln_kernel,
        out_shape=(jax.ShapeDtypeStruct((m, d), F32), jax.ShapeDtypeStruct((m, d), BF16)),
        grid=(m // tm,),
        in_specs=[pl.BlockSpec((tm, k), lambda i: (i, 0)),
                  pl.BlockSpec((None, k, d), lambda i: (layer, 0, 0)),
                  pl.BlockSpec((tm, d), lambda i: (i, 0)),
                  pl.BlockSpec((None, 1, d), lambda i: (layer, 0, 0)),
                  pl.BlockSpec((None, 1, d), lambda i: (layer, 0, 0))],
        out_specs=(pl.BlockSpec((tm, d), lambda i: (i, 0)), pl.BlockSpec((tm, d), lambda i: (i, 0))),
        compiler_params=_cparams(1, VMEM_LIMIT_BYTES),
        name=name,
    )(x, w, h, g, b)


def _res_ln_kernel(y_ref, h_ref, g_ref, b_ref, o_ref, ob_ref):
    out = _layer_norm_rows(DN_ALPHA * h_ref[...] + y_ref[...], g_ref[...], b_ref[...])
    o_ref[...] = out
    ob_ref[...] = out.astype(BF16)


def _res_ln(y, h, g, b, layer, *, tm, name):
    m, d = h.shape
    return pl.pallas_call(
        _res_ln_kernel,
        out_shape=(jax.ShapeDtypeStruct((m, d), F32), jax.ShapeDtypeStruct((m, d), BF16)),
        grid=(m // tm,),
        in_specs=[pl.BlockSpec((tm, d), lambda i: (i, 0)),
                  pl.BlockSpec((tm, d), lambda i: (i, 0)),
                  pl.BlockSpec((None, 1, d), lambda i: (layer, 0, 0)),
                  pl.BlockSpec((None, 1, d), lambda i: (layer, 0, 0))],
        out_specs=(pl.BlockSpec((tm, d), lambda i: (i, 0)), pl.BlockSpec((tm, d), lambda i: (i, 0))),
        compiler_params=_cparams(1, VMEM_LIMIT_BYTES),
        name=name,
    )(y, h, g, b)


def _merge_kernel(b0, b1, b2, b3, g0, g1, g2, g3, wb_ref, o_ref):
    acc = None
    for n, (br, gr) in enumerate(((b0, g0), (b1, g1), (b2, g2), (b3, g3))):
        y = _dot(br[...], wb_ref[n])
        gy = jax.nn.sigmoid(gr[...].astype(F32)) * y
        acc = gy if acc is None else acc + gy
    o_ref[...] = acc.astype(o_ref.dtype)


def _merge(branches, z, wb, layer, *, tm, tn):
    m = z.shape[0]
    gate_specs = []
    for n in range(N_MIXERS):
        base = (_OFF['gates'] + n * D_MODEL) // tn
        gate_specs.append(pl.BlockSpec((tm, tn), lambda j, i, base=base: (i, base + j)))
    return pl.pallas_call(
        _merge_kernel,
        out_shape=jax.ShapeDtypeStruct((m, D_MODEL), BF16),
        grid=(D_MODEL // tn, m // tm),
        in_specs=[pl.BlockSpec((tm, MIX_WIDTH), lambda j, i: (i, 0))] * N_MIXERS + gate_specs
        + [pl.BlockSpec((None, N_MIXERS, MIX_WIDTH, tn), lambda j, i: (layer, 0, 0, j))],
        out_specs=pl.BlockSpec((tm, tn), lambda j, i: (i, j)),
        compiler_params=_cparams(2, VMEM_LIMIT_BYTES),
        name="branch_merge",
    )(*branches, z, z, z, z, wb)


def _sortable_key(score):
    score = jnp.where(score == 0.0, 0.0, score)
    bits = pltpu.bitcast(score, I32)
    return bits ^ (jnp.right_shift(bits, 31) & 0x7FFFFFFF)


def _kth_largest_key(key_scr, k, n_chunks, thr_scr):
    half = Q_TILE // 2

    def search(n):
        def body(i, lo):
            bit = jnp.left_shift(jnp.int32(1), 31 - i)
            out = []
            for g in range(2):
                rows = slice(g * half, (g + 1) * half)
                cand_u = lo[g] | bit
                cand_s = cand_u ^ INT_MIN
                cnt = jnp.zeros((half, K_CHUNK), F32)
                for c in range(n):
                    cnt = cnt + jnp.where(key_scr[c, rows, :] >= cand_s, 1.0, 0.0)
                tot = jnp.sum(cnt, axis=1, keepdims=True)
                out.append(jnp.where(tot >= float(k), cand_u, lo[g]))
            return tuple(out)

        zero = jnp.zeros((half, 1), I32)
        lo = lax.fori_loop(0, 32, body, (zero, zero))
        return jnp.concatenate(lo, axis=0) ^ INT_MIN

    n_even = jnp.left_shift(jnp.right_shift(n_chunks + 1, 1), 1)
    for n in range(2, N_CHUNKS + 1, 2):
        @pl.when(n_even == n)
        def _():
            thr_scr[...] = search(n)

    return thr_scr[...]


def _break_ties(key_scr, thr, k, tri_ref):
    ge = jnp.zeros((Q_TILE, K_CHUNK), F32)
    for c in range(N_CHUNKS):
        ge = ge + jnp.where(key_scr[c] >= thr, 1.0, 0.0)
    tied = (jnp.sum(ge, axis=1, keepdims=True) > float(k)) & (thr > KEY_NEG_INF)

    @pl.when(jnp.max(jnp.where(tied, 1.0, 0.0)) > 0.0)
    def _():
        gt = jnp.zeros((Q_TILE, K_CHUNK), F32)
        for c in range(N_CHUNKS):
            gt = gt + jnp.where(key_scr[c] > thr, 1.0, 0.0)
        need = float(k) - jnp.sum(gt, axis=1, keepdims=True)
        before = jnp.zeros((Q_TILE, 1), F32)
        for c in range(N_CHUNKS):
            kc = key_scr[c]
            eq = kc == thr
            eq_f = jnp.where(eq, 1.0, 0.0)
            rank = _dot(eq_f.astype(BF16), tri_ref[...]) + before
            drop = tied & eq & (rank > need)
            key_scr[c] = jnp.where(drop, INT_MIN, kc)
            before = before + jnp.sum(eq_f, axis=1, keepdims=True)


def _slab_index(rel):
    return jnp.where(rel == 0, 2, jnp.where(rel == -128, 1, jnp.where(rel == -256, 0, 3)))


def _stack_heads(q):
    return jnp.concatenate([q[:, h * HEAD_DIM:(h + 1) * HEAD_DIM] for h in range(N_HEADS)], axis=0)


def _mqa_masked_attention(qs, k_ref, v_ref, slab_ref, t0, chunks, ok_fn, s_scr, p_scr):
    scale = HEAD_DIM ** -0.5
    width = len(chunks) * K_CHUNK
    values = []
    for n, c in enumerate(chunks):
        cols = slice(n * K_CHUNK, (n + 1) * K_CHUNK)
        start = c * K_CHUNK if isinstance(c, int) else pl.multiple_of(jnp.maximum(c, 0) * K_CHUNK, K_CHUNK)
        madd = jnp.where(ok_fn(c), 0.0, NEG_BIG)
        s4 = _dot_nt(qs, k_ref[pl.ds(start, K_CHUNK), :]) * scale
        values.append(v_ref[pl.ds(start, K_CHUNK), :])
        slab = _slab_index(c * K_CHUNK - t0)
        for h in range(N_HEADS):
            s_scr[h, :, cols] = s4[h * Q_TILE:(h + 1) * Q_TILE] + slab_ref[slab, h] + madd
    norms = []
    for h in range(N_HEADS):
        s = s_scr[h, :, :width]
        p = jnp.exp(s - jnp.max(s, axis=1, keepdims=True))
        norms.append(jnp.sum(p, axis=1, keepdims=True))
        p_scr[h * Q_TILE:(h + 1) * Q_TILE, :width] = p.astype(BF16)
    o4 = _dot(p_scr[:, :width], jnp.concatenate(values, axis=0))
    return [o4[h * Q_TILE:(h + 1) * Q_TILE] / norms[h] for h in range(N_HEADS)]


def _row_pos(t0):
    return t0 + lax.broadcasted_iota(I32, (Q_TILE, K_CHUNK), 0)


def _col_pos(c):
    return c * K_CHUNK + lax.broadcasted_iota(I32, (Q_TILE, K_CHUNK), 1)


_ALL_CHUNKS = list(range(N_CHUNKS))
_ATTN_SCRATCH = [pltpu.VMEM((N_CHUNKS, Q_TILE, K_CHUNK), I32),
                 pltpu.VMEM((Q_TILE, 1), I32),
                 pltpu.VMEM((N_HEADS, Q_TILE, SEQ), F32),
                 pltpu.VMEM((N_HEADS * Q_TILE, SEQ), BF16)]
_SLAB_SPEC = pl.BlockSpec((4, N_HEADS, Q_TILE, K_CHUNK), lambda b, i: (0, 0, 0, 0))
_TRI_SPEC = pl.BlockSpec((K_CHUNK, K_CHUNK), lambda b, i: (0, 0))


def _dsa_kernel(q_ref, iq_ref, misc_ref, k_ref, v_ref, iklo_ref, ikhi_ref, slab_ref, tri_ref, o_ref,
                key_scr, thr_scr, s_scr, p_scr):
    qb = pl.program_id(1)
    t0 = qb * Q_TILE
    n_chunks = qb // 2 + 1
    misc = misc_ref[...].astype(F32)
    row = t0 + lax.broadcasted_iota(I32, (Q_TILE, K_CHUNK), 0)

    key_scr[...] = jnp.full(key_scr.shape, KEY_NEG_INF, I32)

    pairs = jnp.concatenate([iq_ref[:, p * LANES:(p + 1) * LANES] for p in range(IDX_HEADS // 2)], axis=0)

    def score_body(c, carry):
        start = pl.multiple_of(c * K_CHUNK, K_CHUNK)
        s_lo = _dot_nt(pairs, iklo_ref[pl.ds(start, K_CHUNK), :])
        s_hi = _dot_nt(pairs, ikhi_ref[pl.ds(start, K_CHUNK), :])
        acc = jnp.zeros((Q_TILE, K_CHUNK), F32)
        for p in range(IDX_HEADS // 2):
            rows = slice(p * Q_TILE, (p + 1) * Q_TILE)
            w0 = misc[:, MISC_IDXW + 2 * p:MISC_IDXW + 2 * p + 1]
            w1 = misc[:, MISC_IDXW + 2 * p + 1:MISC_IDXW + 2 * p + 2]
            acc = acc + jnp.maximum(s_lo[rows], 0.0) * w0
            acc = acc + jnp.maximum(s_hi[rows], 0.0) * w1
        col = c * K_CHUNK + lax.broadcasted_iota(I32, (Q_TILE, K_CHUNK), 1)
        score = jnp.where(col <= row, acc, -jnp.inf)
        key_scr[c] = _sortable_key(score)
        return carry

    lax.fori_loop(0, n_chunks, score_body, 0)
    thr = _kth_largest_key(key_scr, DSA_TOPK, n_chunks, thr_scr)
    _break_ties(key_scr, thr, DSA_TOPK, tri_ref)

    def ok_fn(c):
        return (key_scr[c] >= thr) & (_col_pos(c) <= row)

    o = _mqa_masked_attention(_stack_heads(q_ref[...]), k_ref, v_ref, slab_ref, t0, _ALL_CHUNKS, ok_fn, s_scr, p_scr)
    o_ref[...] = jnp.concatenate(o, axis=1).astype(o_ref.dtype)


def _zspec_rows(name, width, rows, index_fn):
    cb, rem = divmod(_OFF[name], width)
    assert rem == 0, name
    return pl.BlockSpec((rows, width), functools.partial(index_fn, cb))


def _dsa(z, slabs, tri):
    nq = SEQ // Q_TILE

    def qidx(cb, b, i):
        return (b * nq + i, cb)

    def kidx(cb, b, i):
        return (b, cb)

    return pl.pallas_call(
        _dsa_kernel,
        out_shape=jax.ShapeDtypeStruct((TOKENS, MIX_WIDTH), BF16),
        grid=(BATCH, nq),
        in_specs=[_zspec_rows('dsa_q', 512, Q_TILE, qidx),
                  _zspec_rows('idx_q', 1024, Q_TILE, qidx),
                  _zspec_rows('misc', 128, Q_TILE, qidx),
                  _zspec_rows('dsa_k', 128, SEQ, kidx),
                  _zspec_rows('dsa_v', 128, SEQ, kidx),
                  _zspec_rows('idx_k_lo', 128, SEQ, kidx),
                  _zspec_rows('idx_k_hi', 128, SEQ, kidx),
                  _SLAB_SPEC, _TRI_SPEC],
        out_specs=pl.BlockSpec((Q_TILE, MIX_WIDTH), lambda b, i: (b * nq + i, 0)),
        scratch_shapes=_ATTN_SCRATCH,
        compiler_params=_cparams(2, VMEM_LIMIT_BYTES),
        name="dsa_attention",
    )(z, z, z, z, z, z, z, slabs, tri)


def _nsa_compress_kernel(xk_ref, xv_ref, pek_ref, pev_ref, kw1_ref, kw2_ref, vw1_ref, vw2_ref, ok_ref, ov_ref):
    half = CMP_STRIDE * HEAD_DIM

    def compress(x_ref, pe_ref, w1_ref, w2_ref):
        x = x_ref[...].astype(F32)
        a = _dot((x + pe_ref[:, :half]).astype(BF16), w1_ref[:half, :].astype(BF16))
        b = _dot((x + pe_ref[:, half:]).astype(BF16), w1_ref[half:, :].astype(BF16))
        hid = a + pltpu.roll(b, b.shape[0] - 1, 0)
        return _dot(jax.nn.gelu(hid).astype(BF16), w2_ref[...].astype(BF16))

    ok_ref[...] = compress(xk_ref, pek_ref, kw1_ref, kw2_ref)
    ov_ref[...] = compress(xv_ref, pev_ref, vw1_ref, vw2_ref)


def _nsa_compress(xk, xv, pek, pev, kw1, kw2, vw1, vw2):
    rows = xk.shape[0]
    full = lambda a: pl.BlockSpec(a.shape, lambda i: (0,) * a.ndim)
    args = (xk, xv, pek, pev, kw1, kw2, vw1, vw2)
    return pl.pallas_call(
        _nsa_compress_kernel,
        out_shape=(jax.ShapeDtypeStruct((rows, HEAD_DIM), F32), jax.ShapeDtypeStruct((rows, HEAD_DIM), F32)),
        grid=(1,),
        in_specs=[full(a) for a in args],
        out_specs=(pl.BlockSpec((rows, HEAD_DIM), lambda i: (0, 0)), pl.BlockSpec((rows, HEAD_DIM), lambda i: (0, 0))),
        compiler_params=_cparams(1, VMEM_LIMIT_BYTES),
        name="nsa_compress",
    )(*args)


def _nsa_kernel(q_ref, misc_ref, kc_ref, vc_ref, ks_ref, vs_ref, kw_ref, vw_ref, cov_ref, slab_ref, tri_ref, o_ref,
                key_scr, thr_scr, s_scr, p_scr):
    qb = pl.program_id(1)
    t0 = qb * Q_TILE
    scale = HEAD_DIM ** -0.5
    qs = _stack_heads(q_ref[...])
    misc = misc_ref[...].astype(F32)

    n_idx = lax.broadcasted_iota(I32, (N_HEADS * Q_TILE, LANES), 1)
    t_idx = t0 + (lax.broadcasted_iota(I32, (N_HEADS * Q_TILE, LANES), 0) & (Q_TILE - 1))
    cmp_ok = (CMP_STRIDE * n_idx + CMP_LEN - 1) <= t_idx
    lc = jnp.where(cmp_ok, _dot_nt(qs, kc_ref[...].astype(BF16)) * scale, NEG_BIG)
    e = jnp.exp(lc - jnp.max(lc, axis=1, keepdims=True))
    p_cmp = jnp.where(cmp_ok, e / jnp.sum(e, axis=1, keepdims=True), 0.0)
    o_cmp = _dot(p_cmp.astype(BF16), vc_ref[...].astype(BF16))

    p_sum = p_cmp[0:Q_TILE] + p_cmp[Q_TILE:2 * Q_TILE] + p_cmp[2 * Q_TILE:3 * Q_TILE] + p_cmp[3 * Q_TILE:]
    p_hi = p_sum.astype(BF16)
    p_lo = (p_sum - p_hi.astype(F32)).astype(BF16)
    cur = jnp.right_shift(t0 + lax.broadcasted_iota(I32, (Q_TILE, K_CHUNK), 0), SLC_SHIFT)
    for c in range(N_CHUNKS):
        cov = cov_ref[:, c * K_CHUNK:(c + 1) * K_CHUNK]
        imp = _dot(p_hi, cov) + _dot(p_lo, cov)
        jb = jnp.right_shift(c * K_CHUNK + lax.broadcasted_iota(I32, (Q_TILE, K_CHUNK), 1), SLC_SHIFT)
        forced = (jb == 0) | (jb == cur) | (jb == cur - 1)
        imp = jnp.where(jb <= cur, imp + jnp.where(forced, FORCE_SCORE, 0.0), -jnp.inf)
        key_scr[c] = _sortable_key(imp)
    thr = _kth_largest_key(key_scr, N_SEL * SLC_LEN, qb // 2 + 1, thr_scr)
    _break_ties(key_scr, thr, N_SEL * SLC_LEN, tri_ref)
    row = _row_pos(t0)

    def sel_ok(c):
        return (key_scr[c] >= thr) & (_col_pos(c) <= row)

    o_slc = _mqa_masked_attention(qs, ks_ref, vs_ref, slab_ref, t0, _ALL_CHUNKS, sel_ok, s_scr, p_scr)

    def win_ok(c):
        col = _col_pos(c)
        dist = row - col
        return (dist >= 0) & (dist < WINDOW) & (col >= 0)

    win_chunks = [qb // 2 - 2 + n for n in range(3)]
    o_win = _mqa_masked_attention(qs, kw_ref, vw_ref, slab_ref, t0, win_chunks, win_ok, s_scr, p_scr)

    outs = []
    for h in range(N_HEADS):
        g = jax.nn.sigmoid(misc[:, MISC_NSAG + 3 * h:MISC_NSAG + 3 * h + 3])
        rows = slice(h * Q_TILE, (h + 1) * Q_TILE)
        outs.append(g[:, 0:1] * o_cmp[rows] + g[:, 1:2] * o_slc[h] + g[:, 2:3] * o_win[h])
    o_ref[...] = jnp.concatenate(outs, axis=1).astype(o_ref.dtype)


def _nsa(z, k_cmp, v_cmp, cov, slabs, tri):
    nq = SEQ // Q_TILE

    def qidx(cb, b, i):
        return (b * nq + i, cb)

    def kidx(cb, b, i):
        return (b, cb)

    cmp_spec = pl.BlockSpec((LANES, HEAD_DIM), lambda b, i: (b, 0))
    return pl.pallas_call(
        _nsa_kernel,
        out_shape=jax.ShapeDtypeStruct((TOKENS, MIX_WIDTH), BF16),
        grid=(BATCH, nq),
        in_specs=[_zspec_rows('nsa_q', 512, Q_TILE, qidx),
                  _zspec_rows('misc', 128, Q_TILE, qidx),
                  cmp_spec, cmp_spec,
                  _zspec_rows('nsa_ks', 128, SEQ, kidx),
                  _zspec_rows('nsa_vs', 128, SEQ, kidx),
                  _zspec_rows('nsa_kw', 128, SEQ, kidx),
                  _zspec_rows('nsa_vw', 128, SEQ, kidx),
                  pl.BlockSpec((LANES, SEQ), lambda b, i: (0, 0)),
                  _SLAB_SPEC, _TRI_SPEC],
        out_specs=pl.BlockSpec((Q_TILE, MIX_WIDTH), lambda b, i: (b * nq + i, 0)),
        scratch_shapes=_ATTN_SCRATCH,
        compiler_params=_cparams(2, VMEM_LIMIT_BYTES),
        name="nsa_attention",
    )(z, z, k_cmp, v_cmp, z, z, z, z, cov, slabs, tri)


SB_TILE = 256
SB_HEADS_PER_STEP = 2


def _sb_kernel(q_ref, k_ref, v_ref, u_ref, o_ref):
    qb = pl.program_id(2)
    scale = HEAD_DIM ** -0.5
    upper = u_ref[...]
    row = qb * SB_TILE + lax.broadcasted_iota(I32, (SB_TILE, SB_TILE), 0)
    lane = lax.broadcasted_iota(I32, (SB_TILE, SB_TILE), 1)

    def body(i, carry):
        c = qb - i
        start = pl.multiple_of(c * SB_TILE, SB_TILE)
        strict = (c * SB_TILE + lane) < row
        out = []
        for g in range(SB_HEADS_PER_STEP):
            later, acc = carry[g]
            cols = slice(g * HEAD_DIM, (g + 1) * HEAD_DIM)
            zl = _dot_nt(q_ref[:, cols], k_ref[pl.ds(start, SB_TILE), cols]) * scale
            log_beta = jnp.minimum(zl, 0.0) - jnp.log1p(jnp.exp(-jnp.abs(zl)))
            log_keep = jnp.where(strict, log_beta - zl, 0.0)
            keep_hi = log_keep.astype(BF16)
            keep_lo = (log_keep - keep_hi.astype(F32)).astype(BF16)
            within = _dot(keep_hi, upper) + _dot(keep_lo, upper)
            a = jnp.where(strict, jnp.exp(log_beta + within + later), 0.0)
            acc = acc + _dot(a.astype(BF16), v_ref[pl.ds(start, SB_TILE), cols])
            later = later + jnp.sum(log_keep, axis=1, keepdims=True)
            out.append((later, acc))
        return tuple(out)

    init = tuple((jnp.zeros((SB_TILE, 1), F32), jnp.zeros((SB_TILE, HEAD_DIM), F32))
                 for _ in range(SB_HEADS_PER_STEP))
    res = lax.fori_loop(0, qb + 1, body, init)
    o_ref[...] = jnp.concatenate([acc for _, acc in res], axis=1).astype(o_ref.dtype)


def _stick_breaking(z, upper):
    nq = SEQ // SB_TILE
    width = SB_HEADS_PER_STEP * HEAD_DIM
    qcb, kcb, vcb = _OFF['sb_q'] // width, _OFF['sb_k'] // width, _OFF['sb_v'] // width
    return pl.pallas_call(
        _sb_kernel,
        out_shape=jax.ShapeDtypeStruct((TOKENS, MIX_WIDTH), BF16),
        grid=(BATCH, N_HEADS // SB_HEADS_PER_STEP, nq),
        in_specs=[pl.BlockSpec((SB_TILE, width), lambda b, h, i: (b * nq + i, qcb + h)),
                  pl.BlockSpec((SEQ, width), lambda b, h, i: (b, kcb + h)),
                  pl.BlockSpec((SEQ, width), lambda b, h, i: (b, vcb + h)),
                  pl.BlockSpec((SB_TILE, SB_TILE), lambda b, h, i: (0, 0))],
        out_specs=pl.BlockSpec((SB_TILE, width), lambda b, h, i: (b * nq + i, h)),
        compiler_params=_cparams(3, VMEM_LIMIT_BYTES),
        name="stick_breaking_attention",
    )(z, z, z, upper)


MLA_QK = 2 * HEAD_DIM


def _rope128(x, cos, nsin, psin):
    return x * cos + pltpu.roll(x, 96, 1) * nsin + pltpu.roll(x, 32, 1) * psin


def _mla_prep_kernel(cq_ref, ckv_ref, kr_ref, qn_ref, kvn_ref, wqn_ref, wqp_ref, wuk_ref, wuv_ref,
                     cos_ref, nsin_ref, psin_ref, q_ref, k_ref, v_ref):
    def rms(x, g):
        return (x * lax.rsqrt(jnp.mean(x * x, axis=-1, keepdims=True) + RMS_EPS) * g).astype(BF16)

    cos, nsin, psin = cos_ref[...], nsin_ref[...], psin_ref[...]
    xq = rms(cq_ref[...].astype(F32), qn_ref[...])
    xc = rms(ckv_ref[...].astype(F32), kvn_ref[...])
    q_nope = _dot(xq, wqn_ref[...])
    q_rope = _dot(xq, wqp_ref[...])
    k_nope = _dot(xc, wuk_ref[...])
    k_rope = _rope128(kr_ref[...].astype(F32), cos, nsin, psin)
    q_parts, k_parts = [], []
    for h in range(N_HEADS):
        cols = slice(h * HEAD_DIM, (h + 1) * HEAD_DIM)
        q_parts += [q_nope[:, cols], _rope128(q_rope[:, cols], cos, nsin, psin)]
        k_parts += [k_nope[:, cols], k_rope]
    q_ref[...] = jnp.concatenate(q_parts, axis=1).astype(BF16)
    k_ref[...] = jnp.concatenate(k_parts, axis=1).astype(BF16)
    v_ref[...] = _dot(xc, wuv_ref[...]).astype(BF16)


def _mla_prep(z, qn, kvn, wqn, wqp, wuk, wuv, cos, nsin, psin, *, tm):
    nt = SEQ // tm

    def zidx(cb, i):
        return (i, cb)

    wspec = pl.BlockSpec((Q_LORA, 512), lambda i: (0, 0))
    nspec = pl.BlockSpec((1, 512), lambda i: (0, 0))
    tspec = pl.BlockSpec((tm, LANES), lambda i: (i % nt, 0))
    return pl.pallas_call(
        _mla_prep_kernel,
        out_shape=(jax.ShapeDtypeStruct((TOKENS, N_HEADS * MLA_QK), BF16),
                   jax.ShapeDtypeStruct((TOKENS, N_HEADS * MLA_QK), BF16),
                   jax.ShapeDtypeStruct((TOKENS, N_HEADS * HEAD_DIM), BF16)),
        grid=(TOKENS // tm,),
        in_specs=[_zspec_rows('mla_cq', 512, tm, zidx), _zspec_rows('mla_ckv', 512, tm, zidx),
                  _zspec_rows('mla_kr', 128, tm, zidx), nspec, nspec, wspec, wspec, wspec, wspec,
                  tspec, tspec, tspec],
        out_specs=(pl.BlockSpec((tm, N_HEADS * MLA_QK), lambda i: (i, 0)),
                   pl.BlockSpec((tm, N_HEADS * MLA_QK), lambda i: (i, 0)),
                   pl.BlockSpec((tm, N_HEADS * HEAD_DIM), lambda i: (i, 0))),
        compiler_params=_cparams(1, VMEM_LIMIT_BYTES),
        name="mla_prep",
    )(z, z, z, qn, kvn, wqn, wqp, wuk, wuv, cos, nsin, psin)


FLASH_TILE = 256


def _flash_kernel(q_ref, k_ref, v_ref, o_ref, s_scr, *, scale, causal, n_kv):
    qb = pl.program_id(2)
    q = q_ref[...]
    row = qb * FLASH_TILE + lax.broadcasted_iota(I32, (FLASH_TILE, FLASH_TILE), 0)
    lane = lax.broadcasted_iota(I32, (FLASH_TILE, FLASH_TILE), 1)
    for c in range(n_kv):
        cols = slice(c * FLASH_TILE, (c + 1) * FLASH_TILE)
        s = _dot_nt(q, k_ref[cols, :]) * scale
        if causal:
            s = jnp.where(c * FLASH_TILE + lane <= row, s, NEG_BIG)
        s_scr[:, cols] = s
    s = s_scr[...]
    p = jnp.exp(s - jnp.max(s, axis=1, keepdims=True))
    o = _dot(p.astype(BF16), v_ref[...]) / jnp.sum(p, axis=1, keepdims=True)
    o_ref[...] = o.astype(o_ref.dtype)


def _flash(q, k, v, *, dk, kv_len, scale, causal, name):
    nq = SEQ // FLASH_TILE
    return pl.pallas_call(
        functools.partial(_flash_kernel, scale=scale, causal=causal, n_kv=kv_len // FLASH_TILE),
        out_shape=jax.ShapeDtypeStruct((TOKENS, N_HEADS * HEAD_DIM), BF16),
        grid=(BATCH, N_HEADS, nq),
        in_specs=[pl.BlockSpec((FLASH_TILE, dk), lambda b, h, i: (b * nq + i, h)),
                  pl.BlockSpec((kv_len, dk), lambda b, h, i: (b, h)),
                  pl.BlockSpec((kv_len, HEAD_DIM), lambda b, h, i: (b, h))],
        out_specs=pl.BlockSpec((FLASH_TILE, HEAD_DIM), lambda b, h, i: (b * nq + i, h)),
        scratch_shapes=[pltpu.VMEM((FLASH_TILE, kv_len), F32)],
        compiler_params=_cparams(3, VMEM_LIMIT_BYTES),
        name=name,
    )(q, k, v)


FFN_TM = 1024
FFN_SUB = 256
FFN_SUB_SHIFT = 8


def _swiglu_step(load_x, nv, j, wg_ref, wu_ref, wd_ref, o_ref, wg_b, wu_b, wd_b):
    wg_b[...] = wg_ref[0].astype(BF16)
    wu_b[...] = wu_ref[0].astype(BF16)
    wd_b[...] = wd_ref[0].astype(BF16)

    @pl.when(j == 0)
    def _():
        o_ref[...] = jnp.zeros(o_ref.shape, F32)

    def accumulate(rows):
        x = load_x(rows)
        g = _dot(x, wg_b[...])
        u = _dot(x, wu_b[...])
        o_ref[rows, :] += _dot((g * jax.nn.sigmoid(g) * u).astype(BF16), wd_b[...])

    n_sub = jnp.right_shift(nv + (FFN_SUB - 1), FFN_SUB_SHIFT)
    for n in range(1, FFN_TM // FFN_SUB + 1):
        @pl.when(n_sub == n)
        def _():
            for s in range(n):
                accumulate(slice(s * FFN_SUB, (s + 1) * FFN_SUB))


def _ffn_kernel(te_ref, nv_ref, x_ref, wg_ref, wu_ref, wd_ref, o_ref, wg_b, wu_b, wd_b):
    del te_ref
    nv = nv_ref[pl.program_id(0)]
    _swiglu_step(lambda rows: x_ref[rows, :], nv, pl.program_id(1), wg_ref, wu_ref, wd_ref, o_ref, wg_b, wu_b, wd_b)


def _moe_ffn_kernel(te_ref, nv_ref, tok_ref, h_hbm, wg_ref, wu_ref, wd_ref, o_ref, x_buf, wg_b, wu_b, wd_b, sem):
    del te_ref
    i = pl.program_id(0)
    j = pl.program_id(1)
    nv = nv_ref[i]

    @pl.when((j == 0) & (nv > 0))
    def _():
        n_rows = jnp.left_shift(jnp.right_shift(nv + (FFN_SUB - 1), FFN_SUB_SHIFT), FFN_SUB_SHIFT)

        def row_copy(tok, r):
            return pltpu.make_async_copy(h_hbm.at[pl.ds(tok, 1)], x_buf.at[pl.ds(r, 1)], sem)

        def start(r, carry):
            row_copy(tok_ref[i * FFN_TM + r], r).start()
            return carry

        lax.fori_loop(0, n_rows, start, 0)

        def wait(r, carry):
            row_copy(0, 0).wait()
            return carry

        lax.fori_loop(0, n_rows, wait, 0)

    _swiglu_step(lambda rows: x_buf[rows, :].astype(BF16), nv, j, wg_ref, wu_ref, wd_ref, o_ref, wg_b, wu_b, wd_b)


def _ffn_specs(tf, n_prefetch):
    nff = D_FF // tf

    def jj(i, j, nv):
        return jnp.where(nv[i] > 0, j, nff - 1)

    w_in = pl.BlockSpec((1, D_MODEL, tf), lambda i, j, te, nv, *_: (te[i], 0, jj(i, j, nv)))
    w_out = pl.BlockSpec((1, tf, D_MODEL), lambda i, j, te, nv, *_: (te[i], jj(i, j, nv), 0))
    out = pl.BlockSpec((FFN_TM, D_MODEL), lambda i, j, *_: (i, 0))
    scratch = [pltpu.VMEM((D_MODEL, tf), BF16), pltpu.VMEM((D_MODEL, tf), BF16), pltpu.VMEM((tf, D_MODEL), BF16)]
    return nff, w_in, w_out, out, scratch


def _grouped_swiglu(tile_expert, tile_valid, x, wg, wu, wd, *, tf, name):
    r = x.shape[0]
    nff, w_in, w_out, out, scratch = _ffn_specs(tf, 2)
    grid_spec = pltpu.PrefetchScalarGridSpec(
        num_scalar_prefetch=2,
        grid=(r // FFN_TM, nff),
        in_specs=[pl.BlockSpec((FFN_TM, D_MODEL), lambda i, j, *_: (i, 0)), w_in, w_in, w_out],
        out_specs=out,
        scratch_shapes=scratch,
    )
    return pl.pallas_call(
        _ffn_kernel,
        out_shape=jax.ShapeDtypeStruct((r, D_MODEL), F32),
        grid_spec=grid_spec,
        compiler_params=_cparams(2, VMEM_LIMIT_BYTES),
        name=name,
    )(tile_expert, tile_valid, x, wg, wu, wd)


def _gathered_swiglu(tile_expert, tile_valid, row_token, h, wg, wu, wd, *, tf, name):
    r = row_token.shape[0]
    nff, w_in, w_out, out, scratch = _ffn_specs(tf, 3)
    grid_spec = pltpu.PrefetchScalarGridSpec(
        num_scalar_prefetch=3,
        grid=(r // FFN_TM, nff),
        in_specs=[pl.BlockSpec(memory_space=pl.ANY), w_in, w_in, w_out],
        out_specs=out,
        scratch_shapes=[pltpu.VMEM((FFN_TM, D_MODEL), F32)] + scratch + [pltpu.SemaphoreType.DMA(())],
    )
    return pl.pallas_call(
        _moe_ffn_kernel,
        out_shape=jax.ShapeDtypeStruct((r, D_MODEL), F32),
        grid_spec=grid_spec,
        compiler_params=_cparams(2, VMEM_LIMIT_BYTES),
        name=name,
    )(tile_expert, tile_valid, row_token, h, wg, wu, wd)


ROUTER_TM = 256
META_E0, META_E1, META_W0, META_W1, META_R0, META_R1 = range(6)


def _router_kernel(h_ref, r_ref, lt_ref, meta_ref, cnt_ref, run_scr):
    i = pl.program_id(0)

    @pl.when(i == 0)
    def _():
        run_scr[...] = jnp.zeros(run_scr.shape, F32)

    def split(x):
        hi = x.astype(BF16)
        return hi, (x - hi.astype(F32)).astype(BF16)

    h_hi, h_lo = split(h_ref[...])
    r_hi, r_lo = split(r_ref[...])
    logits = _dot(h_hi, r_hi) + (_dot(h_hi, r_lo) + _dot(h_lo, r_hi))
    lane = lax.broadcasted_iota(I32, (ROUTER_TM, LANES), 1).astype(F32)
    logits = jnp.where(lane < N_EXPERTS, logits, -jnp.inf)

    def top1(x):
        m = jnp.max(x, axis=1, keepdims=True)
        idx = jnp.min(jnp.where(x == m, lane, float(LANES)), axis=1, keepdims=True)
        return m, idx

    m0, e0 = top1(logits)
    m1, e1 = top1(jnp.where(lane == e0, -jnp.inf, logits))
    ex = jnp.exp(m1 - m0)
    w0 = 1.0 / (1.0 + ex)
    w1 = ex / (1.0 + ex)

    hot0 = jnp.where(lane == e0, 1.0, 0.0)
    hot1 = jnp.where(lane == e1, 1.0, 0.0)
    before0 = _dot(lt_ref[...], hot0.astype(BF16)) + run_scr[0:1, :]
    tot0 = jnp.sum(hot0, axis=0, keepdims=True)
    before1 = _dot(lt_ref[...], hot1.astype(BF16)) + run_scr[0:1, :] + tot0
    r0 = jnp.sum(hot0 * before0, axis=1, keepdims=True)
    r1 = jnp.sum(hot1 * before1, axis=1, keepdims=True)
    run_new = run_scr[0:1, :] + tot0 + jnp.sum(hot1, axis=0, keepdims=True)
    run_scr[...] = jnp.broadcast_to(run_new, run_scr.shape)
    cnt_ref[...] = jnp.broadcast_to(run_new, cnt_ref.shape)

    meta = jnp.zeros((ROUTER_TM, LANES), F32)
    for ln, val in ((META_E0, e0), (META_E1, e1), (META_W0, w0), (META_W1, w1),
                    (META_R0, r0), (META_R1, r1)):
        meta = jnp.where(lane == ln, val, meta)
    meta_ref[...] = meta


def _router(h, router_padded, lower_tri):
    return pl.pallas_call(
        _router_kernel,
        out_shape=(jax.ShapeDtypeStruct((TOKENS, LANES), F32), jax.ShapeDtypeStruct((8, LANES), F32)),
        grid=(TOKENS // ROUTER_TM,),
        in_specs=[pl.BlockSpec((ROUTER_TM, D_MODEL), lambda i: (i, 0)),
                  pl.BlockSpec((D_MODEL, LANES), lambda i: (0, 0)),
                  pl.BlockSpec((ROUTER_TM, ROUTER_TM), lambda i: (0, 0))],
        out_specs=(pl.BlockSpec((ROUTER_TM, LANES), lambda i: (i, 0)), pl.BlockSpec((8, LANES), lambda i: (0, 0))),
        scratch_shapes=[pltpu.VMEM((8, LANES), F32)],
        compiler_params=_cparams(1, VMEM_LIMIT_BYTES),
        name="moe_router",
    )(h, router_padded, lower_tri)


COMBINE_TM = 256


def _combine_kernel(rows_ref, y_hbm, h_ref, meta_ref, g_ref, b_ref, o_ref, ob_ref, buf, sem):
    i = pl.program_id(0)

    def row_copy(src, slot, r):
        return pltpu.make_async_copy(y_hbm.at[pl.ds(src, 1)], buf.at[slot, pl.ds(r, 1)], sem)

    def start(r, carry):
        tok = i * COMBINE_TM + r
        row_copy(rows_ref[tok], 0, r).start()
        row_copy(rows_ref[TOKENS + tok], 1, r).start()
        return carry

    lax.fori_loop(0, COMBINE_TM, start, 0)

    def wait(r, carry):
        row_copy(0, 0, 0).wait()
        row_copy(0, 1, 0).wait()
        return carry

    lax.fori_loop(0, COMBINE_TM, wait, 0)
    meta = meta_ref[...]
    f = meta[:, META_W0:META_W0 + 1] * buf[0] + meta[:, META_W1:META_W1 + 1] * buf[1]
    out = _layer_norm_rows(DN_ALPHA * h_ref[...] + f, g_ref[...], b_ref[...])
    o_ref[...] = out
    ob_ref[...] = out.astype(BF16)


def _combine(rows, y, h, meta, g, b, layer):
    grid_spec = pltpu.PrefetchScalarGridSpec(
        num_scalar_prefetch=1,
        grid=(TOKENS // COMBINE_TM,),
        in_specs=[pl.BlockSpec(memory_space=pl.ANY),
                  pl.BlockSpec((COMBINE_TM, D_MODEL), lambda i, rows: (i, 0)),
                  pl.BlockSpec((COMBINE_TM, LANES), lambda i, rows: (i, 0)),
                  pl.BlockSpec((None, 1, D_MODEL), lambda i, rows: (layer, 0, 0)),
                  pl.BlockSpec((None, 1, D_MODEL), lambda i, rows: (layer, 0, 0))],
        out_specs=(pl.BlockSpec((COMBINE_TM, D_MODEL), lambda i, rows: (i, 0)),
                   pl.BlockSpec((COMBINE_TM, D_MODEL), lambda i, rows: (i, 0))),
        scratch_shapes=[pltpu.VMEM((2, COMBINE_TM, D_MODEL), F32), pltpu.SemaphoreType.DMA(())],
    )
    return pl.pallas_call(
        _combine_kernel,
        out_shape=(jax.ShapeDtypeStruct((TOKENS, D_MODEL), F32), jax.ShapeDtypeStruct((TOKENS, D_MODEL), BF16)),
        grid_spec=grid_spec,
        compiler_params=_cparams(1, VMEM_LIMIT_BYTES),
        name="moe_combine",
    )(rows, y, h, meta, g, b)


def _packed_pieces():
    src = {}
    off = 0
    for name, width in IN_SPLITS:
        src[name] = off
        off += width
    pieces = []
    for name, width in _PACKED:
        if name == 'idx_k_lo':
            pieces += [(src['idx_k'], 64), (None, 64)]
        elif name == 'idx_k_hi':
            pieces += [(None, 64), (src['idx_k'], 64)]
        elif name == 'misc':
            pieces += [(src['idx_w'], 16), (src['nsa_g'], 12), (None, 128 - 28)]
        elif name == 'mla_kr':
            pieces += [(src['mla_kr'], 64), (None, 64)]
        else:
            pieces.append((src[name], width))
    return pieces


def _pack_w_in_t(w_in):
    wt = jnp.transpose(w_in, (0, 2, 1))
    zero = lambda width: jnp.zeros((wt.shape[0], width, wt.shape[2]), wt.dtype)
    pieces = [zero(width) if start is None else wt[:, start:start + width] for start, width in _packed_pieces()]
    return jnp.concatenate(pieces, axis=1).astype(BF16)


def _t5_bucket_np(dist):
    exact = REL_BUCKETS // 2
    d = np.maximum(dist, 0)
    log_ratio = np.log(np.maximum(d, 1).astype(np.float32) / exact) / math.log(REL_MAX_DIST / exact)
    far = np.minimum(exact + (log_ratio * (REL_BUCKETS - exact)).astype(np.int32), REL_BUCKETS - 1)
    return np.where(d < exact, d, far).astype(np.int32)


def _bias_slabs(table4):
    i = np.arange(Q_TILE)[:, None]
    j = np.arange(K_CHUNK)[None, :]
    bucket = jnp.asarray(np.stack([_t5_bucket_np(i - j - rel) for rel in (-256, -128, 0, -2 * SEQ)]))
    tab = table4.astype(F32)
    out = jnp.zeros((4, N_HEADS, Q_TILE, K_CHUNK), F32)
    for b in range(REL_BUCKETS):
        out = jnp.where((bucket == b)[:, None], tab[b][None, :, None, None], out)
    return out


def _rope_tables():
    half = QK_ROPE // 2
    inv = ROPE_BASE ** (-jnp.arange(half, dtype=F32) / half)
    ang = jnp.arange(SEQ, dtype=F32)[:, None] * inv[None, :]
    cos, sin = jnp.cos(ang), jnp.sin(ang)
    zero = jnp.zeros_like(cos)
    pad = jnp.zeros((SEQ, LANES - QK_ROPE), F32)
    cos_t = jnp.concatenate([cos, cos, pad], axis=1)
    nsin_t = jnp.concatenate([-sin, zero, pad], axis=1)
    psin_t = jnp.concatenate([zero, sin, pad], axis=1)
    return cos_t, nsin_t, psin_t


def _cover_expanded():
    n = np.arange(LANES)[:, None]
    s = np.arange(SEQ)[None, :]
    j = s // SLC_LEN
    cover = (CMP_STRIDE * n < SLC_LEN * j + SLC_LEN) & (CMP_STRIDE * n + CMP_LEN > SLC_LEN * j) & (n < N_CMP)
    return jnp.asarray(cover, BF16)


def _strict_upper(n):
    return jnp.asarray(np.arange(n)[:, None] > np.arange(n)[None, :], BF16)


def _strict_lower(n):
    return jnp.asarray(np.arange(n)[None, :] < np.arange(n)[:, None], BF16)


def _moe_layout(meta, counts):
    n_tiles = 2 * TOKENS // FFN_TM + N_EXPERTS
    cnt = counts[0, :N_EXPERTS].astype(I32)
    padded = ((cnt + FFN_TM - 1) // FFN_TM) * FFN_TM
    ends = jnp.cumsum(padded)
    offs = ends - padded
    e = meta[:, META_E0:META_E1 + 1].astype(I32)
    rank = meta[:, META_R0:META_R1 + 1].astype(I32)
    rows = (offs[e] + rank).T.reshape(-1)
    token = jnp.tile(jnp.arange(TOKENS, dtype=I32), 2)
    row_token = jnp.zeros((n_tiles * FFN_TM,), I32).at[rows].set(token)
    tile_start = jnp.arange(n_tiles, dtype=I32) * FFN_TM
    te = jnp.minimum(jnp.sum(tile_start[:, None] >= ends[None, :], axis=1), N_EXPERTS - 1).astype(I32)
    nv = jnp.clip(cnt[te] - (tile_start - offs[te]), 0, FFN_TM).astype(I32)
    return rows, row_token, te, nv


def kernel(x, mem, rel_table, w_in, mla_q_norm, mla_kv_norm, mla_w_uq, mla_w_uk, mla_w_uv, nsa_pe_k, nsa_pe_v,
           nsa_ck_w1, nsa_ck_w2, nsa_cv_w1, nsa_cv_w2, w_branch, w_out, ln1_g, ln1_b, xa_wq, xa_wk, xa_wv, xa_wo,
           ln2_g, ln2_b, ffn_w_gate, ffn_w_up, ffn_w_down, moe_router, moe_w_gate, moe_w_up, moe_w_down,
           ln3_g, ln3_b):
    h = x.reshape(TOKENS, D_MODEL)
    hb = h.astype(BF16)
    mem_b = mem.reshape(BATCH * MEM_LEN, D_MODEL).astype(BF16)

    w_in_t = _pack_w_in_t(w_in)
    dsa_slabs = _bias_slabs(rel_table[:, :N_HEADS])
    nsa_slabs = _bias_slabs(rel_table[:, N_HEADS:])
    cos_t, nsin_t, psin_t = _rope_tables()
    cover = _cover_expanded()
    upper = _strict_upper(SB_TILE)
    incl_upper = jnp.asarray(np.arange(K_CHUNK)[:, None] <= np.arange(K_CHUNK)[None, :], BF16)
    lower = _strict_lower(ROUTER_TM)

    uq = mla_w_uq.reshape(DEPTH, Q_LORA, N_HEADS, QK_NOPE + QK_ROPE)
    w_qn = uq[..., :QK_NOPE].reshape(DEPTH, Q_LORA, N_HEADS * QK_NOPE).astype(BF16)
    w_qp = jnp.concatenate([uq[..., QK_NOPE:], jnp.zeros((DEPTH, Q_LORA, N_HEADS, LANES - QK_ROPE), F32)],
                           axis=-1).reshape(DEPTH, Q_LORA, N_HEADS * LANES).astype(BF16)
    w_uk = mla_w_uk.astype(BF16)
    w_uv = mla_w_uv.astype(BF16)
    w_br = w_branch.astype(BF16)
    w_o = w_out.astype(BF16)
    xq, xk, xv, xo = (w.astype(BF16) for w in (xa_wq, xa_wk, xa_wv, xa_wo))
    router_p = jnp.pad(moe_router, ((0, 0), (0, 0), (0, LANES - N_EXPERTS)))
    row2 = lambda a: a.reshape(DEPTH, 1, -1)
    g1, b1, g2, b2, g3, b3 = (row2(a) for a in (ln1_g, ln1_b, ln2_g, ln2_b, ln3_g, ln3_b))
    qn, kvn = row2(mla_q_norm), row2(mla_kv_norm)
    pek = nsa_pe_k.reshape(DEPTH, 1, CMP_LEN * HEAD_DIM)
    pev = nsa_pe_v.reshape(DEPTH, 1, CMP_LEN * HEAD_DIM)
    dense_nv = jnp.full((TOKENS // FFN_TM,), FFN_TM, I32)
    moe_wg = moe_w_gate.reshape((-1,) + moe_w_gate.shape[2:])
    moe_wu = moe_w_up.reshape((-1,) + moe_w_up.shape[2:])
    moe_wd = moe_w_down.reshape((-1,) + moe_w_down.shape[2:])

    for layer in range(DEPTH):
        z = _matmul_nt(hb, w_in_t, layer, tm=1024, tn=1024, out_dtype=BF16, name="in_proj")
        o_dsa = _dsa(z, dsa_slabs, incl_upper)
        o_sb = _stick_breaking(z, upper)
        group = lambda name: z[:, _OFF[name]:_OFF[name] + HEAD_DIM].reshape(BATCH * LANES, CMP_STRIDE * HEAD_DIM)
        k_cmp, v_cmp = _nsa_compress(group('nsa_kc'), group('nsa_vc'), pek[layer], pev[layer],
                                     nsa_ck_w1[layer], nsa_ck_w2[layer], nsa_cv_w1[layer], nsa_cv_w2[layer])
        o_nsa = _nsa(z, k_cmp, v_cmp, cover, nsa_slabs, incl_upper)
        q_cat, k_cat, v_mla = _mla_prep(z, qn[layer], kvn[layer], w_qn[layer], w_qp[layer], w_uk[layer],
                                        w_uv[layer], cos_t, nsin_t, psin_t, tm=512)
        o_mla = _flash(q_cat, k_cat, v_mla, dk=MLA_QK, kv_len=SEQ, scale=(QK_NOPE + QK_ROPE) ** -0.5,
                       causal=True, name="mla_attention")
        mixed = _merge((o_dsa, o_sb, o_nsa, o_mla), z, w_br, layer, tm=512, tn=512)
        h, hb = _matmul_res_ln(mixed, w_o, h, g1, b1, layer, tm=256, name="out_proj_ln")

        q_x = _matmul(hb, xq, layer, tm=1024, tn=512, out_dtype=BF16, name="xa_q_proj")
        k_x = _matmul(mem_b, xk, layer, tm=1024, tn=512, out_dtype=BF16, name="xa_k_proj")
        v_x = _matmul(mem_b, xv, layer, tm=1024, tn=512, out_dtype=BF16, name="xa_v_proj")
        o_x = _flash(q_x, k_x, v_x, dk=HEAD_DIM, kv_len=MEM_LEN, scale=HEAD_DIM ** -0.5, causal=False,
                     name="cross_attention")
        h, hb = _matmul_res_ln(o_x, xo, h, g2, b2, layer, tm=256, name="xa_out_proj_ln")

        i = layer // 2
        if layer % 2 == 0:
            dense_te = jnp.full((TOKENS // FFN_TM,), i, I32)
            y = _grouped_swiglu(dense_te, dense_nv, hb, ffn_w_gate, ffn_w_up, ffn_w_down, tf=256,
                                name="dense_swiglu")
            h, hb = _res_ln(y, h, g3, b3, layer, tm=256, name="ffn_res_ln")
        else:
            meta, counts = _router(h, router_p[i], lower)
            rows, row_token, te, nv = _moe_layout(meta, counts)
            y = _gathered_swiglu(te + i * N_EXPERTS, nv, row_token, h, moe_wg, moe_wu, moe_wd, tf=256,
                                 name="moe_swiglu")
            h, hb = _combine(rows, y, h, meta, g3, b3, layer)
    return h.reshape(BATCH, SEQ, D_MODEL)
```

```python
import functools
import math

import jax
import jax.numpy as jnp
import numpy as np
from jax import lax
from jax.experimental import pallas as pl
from jax.experimental.pallas import tpu as pltpu

F32 = jnp.float32
BF16 = jnp.bfloat16
I32 = jnp.int32

D_MODEL = 2048
BATCH = 4
SEQ = 2048
DEPTH = 4
TOKENS = BATCH * SEQ
MEM_LEN = 256
HEAD_DIM = 128
N_HEADS = 4
DSA_TOPK = min(256, SEQ // 4)
IDX_HEADS = 16
IDX_DIM = 64
CMP_LEN = 32
CMP_STRIDE = 16
CMP_HIDDEN = 256
N_CMP = (SEQ - CMP_LEN) // CMP_STRIDE + 1
SLC_LEN = 64
SLC_SHIFT = 6
N_SLC = SEQ // SLC_LEN
N_SEL = min(16, N_SLC)
WINDOW = 512
FORCE_SCORE = 1.0e4
Q_LORA = 512
KV_LORA = 512
QK_NOPE = 128
QK_ROPE = 64
ROPE_BASE = 10000.0
N_MIXERS = 4
MIX_WIDTH = 512
REL_BUCKETS = 32
REL_MAX_DIST = 128
D_FF = 5632
N_EXPERTS = 8
DN_ALPHA = (2 * DEPTH) ** 0.25
LN_EPS = 1e-5
RMS_EPS = 1e-6
NEG_BIG = -1.0e30

IN_SPLITS = (
    ('dsa_q', 512), ('dsa_k', 128), ('dsa_v', 128),
    ('idx_q', IDX_HEADS * IDX_DIM), ('idx_k', IDX_DIM), ('idx_w', IDX_HEADS),
    ('sb_q', 512), ('sb_k', 512), ('sb_v', 512),
    ('nsa_q', 512),
    ('nsa_kc', 128), ('nsa_vc', 128), ('nsa_ks', 128), ('nsa_vs', 128),
    ('nsa_kw', 128), ('nsa_vw', 128), ('nsa_g', 12),
    ('mla_cq', Q_LORA), ('mla_ckv', KV_LORA), ('mla_kr', QK_ROPE),
    ('gates', N_MIXERS * D_MODEL),
)

LANES = 128
Q_TILE = 128
K_CHUNK = 256
N_CHUNKS = SEQ // K_CHUNK
VMEM_LIMIT_BYTES = 56 * 1024 * 1024
INT_MIN = -2147483648
KEY_NEG_INF = -2139095041

_PACKED = (
    ('idx_q', 1024), ('dsa_q', 512), ('sb_q', 512), ('sb_k', 512), ('sb_v', 512), ('nsa_q', 512),
    ('mla_cq', 512), ('mla_ckv', 512),
    ('dsa_k', 128), ('dsa_v', 128), ('idx_k_lo', 128), ('idx_k_hi', 128), ('misc', 128),
    ('nsa_kc', 128), ('nsa_vc', 128), ('nsa_ks', 128), ('nsa_vs', 128), ('nsa_kw', 128), ('nsa_vw', 128),
    ('mla_kr', 128), ('gates', 8192),
)
_OFF = {}
_o = 0
for _n, _w in _PACKED:
    _OFF[_n] = _o
    _o += _w
Z_WIDTH = _o
MISC_IDXW = 0
MISC_NSAG = 16


def _cparams(n_axes, vmem=None):
    return pltpu.CompilerParams(dimension_semantics=("arbitrary",) * n_axes, vmem_limit_bytes=vmem)


def _dot(a, b):
    return jnp.dot(a, b, preferred_element_type=F32)


def _dot_nt(a, b):
    return lax.dot_general(a, b, (((1,), (1,)), ((), ())), preferred_element_type=F32)


def _layer_norm_rows(v, g, b):
    mu = jnp.mean(v, axis=-1, keepdims=True)
    d = v - mu
    var = jnp.mean(d * d, axis=-1, keepdims=True)
    return d * lax.rsqrt(var + LN_EPS) * g + b


def _matmul_kernel(x_ref, w_ref, o_ref):
    o_ref[...] = _dot(x_ref[...].astype(BF16), w_ref[...].astype(BF16)).astype(o_ref.dtype)


def _matmul(x, w, layer, *, tm, tn, out_dtype, name):
    m, k = x.shape
    n = w.shape[2]
    return pl.pallas_call(
        _matmul_kernel,
        out_shape=jax.ShapeDtypeStruct((m, n), out_dtype),
        grid=(n // tn, m // tm),
        in_specs=[pl.BlockSpec((tm, k), lambda j, i: (i, 0)),
                  pl.BlockSpec((None, k, tn), lambda j, i: (layer, 0, j))],
        out_specs=pl.BlockSpec((tm, tn), lambda j, i: (i, j)),
        compiler_params=_cparams(2, VMEM_LIMIT_BYTES),
        name=name,
    )(x, w)


def _matmul_res_ln_kernel(x_ref, w_ref, h_ref, g_ref, b_ref, o_ref, ob_ref):
    y = _dot(x_ref[...], w_ref[...])
    out = _layer_norm_rows(DN_ALPHA * h_ref[...] + y, g_ref[...], b_ref[...])
    o_ref[...] = out
    ob_ref[...] = out.astype(BF16)


def _matmul_res_ln(x, w, h, g, b, layer, *, tm, name):
    m, k = x.shape
    d = w.shape[2]
    return pl.pallas_call(
        _matmul_res_ln_kernel,
        out_shape=(jax.ShapeDtypeStruct((m, d), F32), jax.ShapeDtypeStruct((m, d), BF16)),
        grid=(m // tm,),
        in_specs=[pl.BlockSpec((tm, k), lambda i: (i, 0)),
                  pl.BlockSpec((None, k, d), lambda i: (layer, 0, 0)),
                  pl.BlockSpec((tm, d), lambda i: (i, 0)),
                  pl.BlockSpec((None, 1, d), lambda i: (layer, 0, 0)),
                  pl.BlockSpec((None, 1, d), lambda i: (layer, 0, 0))],
        out_specs=(pl.BlockSpec((tm, d), lambda i: (i, 0)), pl.BlockSpec((tm, d), lambda i: (i, 0))),
        compiler_params=_cparams(1, VMEM_LIMIT_BYTES),
        name=name,
    )(x, w, h, g, b)


def _res_ln_kernel(y_ref, h_ref, g_ref, b_ref, o_ref, ob_ref):
    out = _layer_norm_rows(DN_ALPHA * h_ref[...] + y_ref[...], g_ref[...], b_ref[...])
    o_ref[...] = out
    ob_ref[...] = out.astype(BF16)


def _res_ln(y, h, g, b, layer, *, tm, name):
    m, d = h.shape
    return pl.pallas_call(
        _res_ln_kernel,
        out_shape=(jax.ShapeDtypeStruct((m, d), F32), jax.ShapeDtypeStruct((m, d), BF16)),
        grid=(m // tm,),
        in_specs=[pl.BlockSpec((tm, d), lambda i: (i, 0)),
                  pl.BlockSpec((tm, d), lambda i: (i, 0)),
                  pl.BlockSpec((None, 1, d), lambda i: (layer, 0, 0)),
                  pl.BlockSpec((None, 1, d), lambda i: (layer, 0, 0))],
        out_specs=(pl.BlockSpec((tm, d), lambda i: (i, 0)), pl.BlockSpec((tm, d), lambda i: (i, 0))),
        compiler_params=_cparams(1, VMEM_LIMIT_BYTES),
        name=name,
    )(y, h, g, b)


def _merge_kernel(b0, b1, b2, b3, g0, g1, g2, g3, wb_ref, o_ref):
    acc = None
    for n, (br, gr) in enumerate(((b0, g0), (b1, g1), (b2, g2), (b3, g3))):
        y = _dot(br[...], wb_ref[n])
        gy = jax.nn.sigmoid(gr[...].astype(F32)) * y
        acc = gy if acc is None else acc + gy
    o_ref[...] = acc.astype(o_ref.dtype)


def _merge(branches, z, wb, layer, *, tm, tn):
    m = z.shape[0]
    gate_specs = []
    for n in range(N_MIXERS):
        base = (_OFF['gates'] + n * D_MODEL) // tn
        gate_specs.append(pl.BlockSpec((tm, tn), lambda j, i, base=base: (i, base + j)))
    return pl.pallas_call(
        _merge_kernel,
        out_shape=jax.ShapeDtypeStruct((m, D_MODEL), BF16),
        grid=(D_MODEL // tn, m // tm),
        in_specs=[pl.BlockSpec((tm, MIX_WIDTH), lambda j, i: (i, 0))] * N_MIXERS + gate_specs
        + [pl.BlockSpec((None, N_MIXERS, MIX_WIDTH, tn), lambda j, i: (layer, 0, 0, j))],
        out_specs=pl.BlockSpec((tm, tn), lambda j, i: (i, j)),
        compiler_params=_cparams(2, VMEM_LIMIT_BYTES),
        name="branch_merge",
    )(*branches, z, z, z, z, wb)


def _sortable_key(score):
    score = jnp.where(score == 0.0, 0.0, score)
    bits = pltpu.bitcast(score, I32)
    return bits ^ (jnp.right_shift(bits, 31) & 0x7FFFFFFF)


def _kth_largest_key(key_scr, k, n_chunks, thr_scr):
    half = Q_TILE // 2

    def search(n):
        def body(i, lo):
            bit = jnp.left_shift(jnp.int32(1), 31 - i)
            out = []
            for g in range(2):
                rows = slice(g * half, (g + 1) * half)
                cand_u = lo[g] | bit
                cand_s = cand_u ^ INT_MIN
                cnt = jnp.zeros((half, K_CHUNK), F32)
                for c in range(n):
                    cnt = cnt + jnp.where(key_scr[c, rows, :] >= cand_s, 1.0, 0.0)
                tot = jnp.sum(cnt, axis=1, keepdims=True)
                out.append(jnp.where(tot >= float(k), cand_u, lo[g]))
            return tuple(out)

        zero = jnp.zeros((half, 1), I32)
        lo = lax.fori_loop(0, 32, body, (zero, zero))
        return jnp.concatenate(lo, axis=0) ^ INT_MIN

    n_even = jnp.left_shift(jnp.right_shift(n_chunks + 1, 1), 1)
    for n in range(2, N_CHUNKS + 1, 2):
        @pl.when(n_even == n)
        def _():
            thr_scr[...] = search(n)

    return thr_scr[...]


def _break_ties(key_scr, thr, k, tri_ref):
    ge = jnp.zeros((Q_TILE, K_CHUNK), F32)
    gt = jnp.zeros((Q_TILE, K_CHUNK), F32)
    for c in range(N_CHUNKS):
        kc = key_scr[c]
        ge = ge + jnp.where(kc >= thr, 1.0, 0.0)
        gt = gt + jnp.where(kc > thr, 1.0, 0.0)
    n_ge = jnp.sum(ge, axis=1, keepdims=True)
    need = float(k) - jnp.sum(gt, axis=1, keepdims=True)
    tied = (n_ge > float(k)) & (thr > KEY_NEG_INF)

    @pl.when(jnp.max(jnp.where(tied, 1.0, 0.0)) > 0.0)
    def _():
        before = jnp.zeros((Q_TILE, 1), F32)
        for c in range(N_CHUNKS):
            kc = key_scr[c]
            eq = kc == thr
            eq_f = jnp.where(eq, 1.0, 0.0)
            rank = _dot(eq_f.astype(BF16), tri_ref[...]) + before
            drop = tied & eq & (rank > need)
            key_scr[c] = jnp.where(drop, INT_MIN, kc)
            before = before + jnp.sum(eq_f, axis=1, keepdims=True)


def _slab_index(rel):
    return jnp.where(rel == 0, 2, jnp.where(rel == -128, 1, jnp.where(rel == -256, 0, 3)))


def _stack_heads(q):
    return jnp.concatenate([q[:, h * HEAD_DIM:(h + 1) * HEAD_DIM] for h in range(N_HEADS)], axis=0)


def _mqa_masked_attention(qs, k_ref, v_ref, slab_ref, t0, chunks, ok_fn, s_scr, p_scr):
    scale = HEAD_DIM ** -0.5
    width = len(chunks) * K_CHUNK
    values = []
    for n, c in enumerate(chunks):
        cols = slice(n * K_CHUNK, (n + 1) * K_CHUNK)
        start = c * K_CHUNK if isinstance(c, int) else pl.multiple_of(jnp.maximum(c, 0) * K_CHUNK, K_CHUNK)
        madd = jnp.where(ok_fn(c), 0.0, NEG_BIG)
        s4 = _dot_nt(qs, k_ref[pl.ds(start, K_CHUNK), :]) * scale
        values.append(v_ref[pl.ds(start, K_CHUNK), :])
        slab = _slab_index(c * K_CHUNK - t0)
        for h in range(N_HEADS):
            s_scr[h, :, cols] = s4[h * Q_TILE:(h + 1) * Q_TILE] + slab_ref[slab, h] + madd
    norms = []
    for h in range(N_HEADS):
        s = s_scr[h, :, :width]
        p = jnp.exp(s - jnp.max(s, axis=1, keepdims=True))
        norms.append(jnp.sum(p, axis=1, keepdims=True))
        p_scr[h * Q_TILE:(h + 1) * Q_TILE, :width] = p.astype(BF16)
    o4 = _dot(p_scr[:, :width], jnp.concatenate(values, axis=0))
    return [o4[h * Q_TILE:(h + 1) * Q_TILE] / norms[h] for h in range(N_HEADS)]


def _row_pos(t0):
    return t0 + lax.broadcasted_iota(I32, (Q_TILE, K_CHUNK), 0)


def _col_pos(c):
    return c * K_CHUNK + lax.broadcasted_iota(I32, (Q_TILE, K_CHUNK), 1)


_ALL_CHUNKS = list(range(N_CHUNKS))
_ATTN_SCRATCH = [pltpu.VMEM((N_CHUNKS, Q_TILE, K_CHUNK), I32),
                 pltpu.VMEM((Q_TILE, 1), I32),
                 pltpu.VMEM((N_HEADS, Q_TILE, SEQ), F32),
                 pltpu.VMEM((N_HEADS * Q_TILE, SEQ), BF16)]
_SLAB_SPEC = pl.BlockSpec((4, N_HEADS, Q_TILE, K_CHUNK), lambda b, i: (0, 0, 0, 0))
_TRI_SPEC = pl.BlockSpec((K_CHUNK, K_CHUNK), lambda b, i: (0, 0))


def _dsa_kernel(q_ref, iq_ref, misc_ref, k_ref, v_ref, iklo_ref, ikhi_ref, slab_ref, tri_ref, o_ref,
                key_scr, thr_scr, s_scr, p_scr):
    qb = pl.program_id(1)
    t0 = qb * Q_TILE
    n_chunks = qb // 2 + 1
    misc = misc_ref[...].astype(F32)
    row = t0 + lax.broadcasted_iota(I32, (Q_TILE, K_CHUNK), 0)

    key_scr[...] = jnp.full(key_scr.shape, KEY_NEG_INF, I32)

    pairs = jnp.concatenate([iq_ref[:, p * LANES:(p + 1) * LANES] for p in range(IDX_HEADS // 2)], axis=0)

    def score_body(c, carry):
        start = pl.multiple_of(c * K_CHUNK, K_CHUNK)
        s_lo = _dot_nt(pairs, iklo_ref[pl.ds(start, K_CHUNK), :])
        s_hi = _dot_nt(pairs, ikhi_ref[pl.ds(start, K_CHUNK), :])
        acc = jnp.zeros((Q_TILE, K_CHUNK), F32)
        for p in range(IDX_HEADS // 2):
            rows = slice(p * Q_TILE, (p + 1) * Q_TILE)
            w0 = misc[:, MISC_IDXW + 2 * p:MISC_IDXW + 2 * p + 1]
            w1 = misc[:, MISC_IDXW + 2 * p + 1:MISC_IDXW + 2 * p + 2]
            acc = acc + jnp.maximum(s_lo[rows], 0.0) * w0
            acc = acc + jnp.maximum(s_hi[rows], 0.0) * w1
        col = c * K_CHUNK + lax.broadcasted_iota(I32, (Q_TILE, K_CHUNK), 1)
        score = jnp.where(col <= row, acc, -jnp.inf)
        key_scr[c] = _sortable_key(score)
        return carry

    lax.fori_loop(0, n_chunks, score_body, 0)
    thr = _kth_largest_key(key_scr, DSA_TOPK, n_chunks, thr_scr)
    _break_ties(key_scr, thr, DSA_TOPK, tri_ref)

    def ok_fn(c):
        return (key_scr[c] >= thr) & (_col_pos(c) <= row)

    o = _mqa_masked_attention(_stack_heads(q_ref[...]), k_ref, v_ref, slab_ref, t0, _ALL_CHUNKS, ok_fn, s_scr, p_scr)
    o_ref[...] = jnp.concatenate(o, axis=1).astype(o_ref.dtype)


def _zspec_rows(name, width, rows, index_fn):
    cb, rem = divmod(_OFF[name], width)
    assert rem == 0, name
    return pl.BlockSpec((rows, width), functools.partial(index_fn, cb))


def _dsa(z, slabs, tri):
    nq = SEQ // Q_TILE

    def qidx(cb, b, i):
        return (b * nq + i, cb)

    def kidx(cb, b, i):
        return (b, cb)

    return pl.pallas_call(
        _dsa_kernel,
        out_shape=jax.ShapeDtypeStruct((TOKENS, MIX_WIDTH), BF16),
        grid=(BATCH, nq),
        in_specs=[_zspec_rows('dsa_q', 512, Q_TILE, qidx),
                  _zspec_rows('idx_q', 1024, Q_TILE, qidx),
                  _zspec_rows('misc', 128, Q_TILE, qidx),
                  _zspec_rows('dsa_k', 128, SEQ, kidx),
                  _zspec_rows('dsa_v', 128, SEQ, kidx),
                  _zspec_rows('idx_k_lo', 128, SEQ, kidx),
                  _zspec_rows('idx_k_hi', 128, SEQ, kidx),
                  _SLAB_SPEC, _TRI_SPEC],
        out_specs=pl.BlockSpec((Q_TILE, MIX_WIDTH), lambda b, i: (b * nq + i, 0)),
        scratch_shapes=_ATTN_SCRATCH,
        compiler_params=_cparams(2, VMEM_LIMIT_BYTES),
        name="dsa_attention",
    )(z, z, z, z, z, z, z, slabs, tri)


def _nsa_compress_kernel(xk_ref, xv_ref, pek_ref, pev_ref, kw1_ref, kw2_ref, vw1_ref, vw2_ref, ok_ref, ov_ref):
    half = CMP_STRIDE * HEAD_DIM

    def compress(x_ref, pe_ref, w1_ref, w2_ref):
        x = x_ref[...].astype(F32)
        a = _dot((x + pe_ref[:, :half]).astype(BF16), w1_ref[:half, :].astype(BF16))
        b = _dot((x + pe_ref[:, half:]).astype(BF16), w1_ref[half:, :].astype(BF16))
        hid = a + pltpu.roll(b, b.shape[0] - 1, 0)
        return _dot(jax.nn.gelu(hid).astype(BF16), w2_ref[...].astype(BF16))

    ok_ref[...] = compress(xk_ref, pek_ref, kw1_ref, kw2_ref)
    ov_ref[...] = compress(xv_ref, pev_ref, vw1_ref, vw2_ref)


def _nsa_compress(xk, xv, pek, pev, kw1, kw2, vw1, vw2):
    rows = xk.shape[0]
    full = lambda a: pl.BlockSpec(a.shape, lambda i: (0,) * a.ndim)
    args = (xk, xv, pek, pev, kw1, kw2, vw1, vw2)
    return pl.pallas_call(
        _nsa_compress_kernel,
        out_shape=(jax.ShapeDtypeStruct((rows, HEAD_DIM), F32), jax.ShapeDtypeStruct((rows, HEAD_DIM), F32)),
        grid=(1,),
        in_specs=[full(a) for a in args],
        out_specs=(pl.BlockSpec((rows, HEAD_DIM), lambda i: (0, 0)), pl.BlockSpec((rows, HEAD_DIM), lambda i: (0, 0))),
        compiler_params=_cparams(1, VMEM_LIMIT_BYTES),
        name="nsa_compress",
    )(*args)


def _nsa_kernel(q_ref, misc_ref, kc_ref, vc_ref, ks_ref, vs_ref, kw_ref, vw_ref, cov_ref, slab_ref, tri_ref, o_ref,
                key_scr, thr_scr, s_scr, p_scr):
    qb = pl.program_id(1)
    t0 = qb * Q_TILE
    scale = HEAD_DIM ** -0.5
    qs = _stack_heads(q_ref[...])
    misc = misc_ref[...].astype(F32)

    n_idx = lax.broadcasted_iota(I32, (N_HEADS * Q_TILE, LANES), 1)
    t_idx = t0 + (lax.broadcasted_iota(I32, (N_HEADS * Q_TILE, LANES), 0) & (Q_TILE - 1))
    cmp_ok = (CMP_STRIDE * n_idx + CMP_LEN - 1) <= t_idx
    lc = jnp.where(cmp_ok, _dot_nt(qs, kc_ref[...].astype(BF16)) * scale, NEG_BIG)
    e = jnp.exp(lc - jnp.max(lc, axis=1, keepdims=True))
    p_cmp = jnp.where(cmp_ok, e / jnp.sum(e, axis=1, keepdims=True), 0.0)
    o_cmp = _dot(p_cmp.astype(BF16), vc_ref[...].astype(BF16))

    p_sum = p_cmp[0:Q_TILE] + p_cmp[Q_TILE:2 * Q_TILE] + p_cmp[2 * Q_TILE:3 * Q_TILE] + p_cmp[3 * Q_TILE:]
    p_hi = p_sum.astype(BF16)
    p_lo = (p_sum - p_hi.astype(F32)).astype(BF16)
    cur = jnp.right_shift(t0 + lax.broadcasted_iota(I32, (Q_TILE, K_CHUNK), 0), SLC_SHIFT)
    for c in range(N_CHUNKS):
        cov = cov_ref[:, c * K_CHUNK:(c + 1) * K_CHUNK]
        imp = _dot(p_hi, cov) + _dot(p_lo, cov)
        jb = jnp.right_shift(c * K_CHUNK + lax.broadcasted_iota(I32, (Q_TILE, K_CHUNK), 1), SLC_SHIFT)
        forced = (jb == 0) | (jb == cur) | (jb == cur - 1)
        imp = jnp.where(jb <= cur, imp + jnp.where(forced, FORCE_SCORE, 0.0), -jnp.inf)
        key_scr[c] = _sortable_key(imp)
    thr = _kth_largest_key(key_scr, N_SEL * SLC_LEN, qb // 2 + 1, thr_scr)
    _break_ties(key_scr, thr, N_SEL * SLC_LEN, tri_ref)
    row = _row_pos(t0)

    def sel_ok(c):
        return (key_scr[c] >= thr) & (_col_pos(c) <= row)

    o_slc = _mqa_masked_attention(qs, ks_ref, vs_ref, slab_ref, t0, _ALL_CHUNKS, sel_ok, s_scr, p_scr)

    def win_ok(c):
        col = _col_pos(c)
        dist = row - col
        return (dist >= 0) & (dist < WINDOW) & (col >= 0)

    win_chunks = [qb // 2 - 2 + n for n in range(3)]
    o_win = _mqa_masked_attention(qs, kw_ref, vw_ref, slab_ref, t0, win_chunks, win_ok, s_scr, p_scr)

    outs = []
    for h in range(N_HEADS):
        g = jax.nn.sigmoid(misc[:, MISC_NSAG + 3 * h:MISC_NSAG + 3 * h + 3])
        rows = slice(h * Q_TILE, (h + 1) * Q_TILE)
        outs.append(g[:, 0:1] * o_cmp[rows] + g[:, 1:2] * o_slc[h] + g[:, 2:3] * o_win[h])
    o_ref[...] = jnp.concatenate(outs, axis=1).astype(o_ref.dtype)


def _nsa(z, k_cmp, v_cmp, cov, slabs, tri):
    nq = SEQ // Q_TILE

    def qidx(cb, b, i):
        return (b * nq + i, cb)

    def kidx(cb, b, i):
        return (b, cb)

    cmp_spec = pl.BlockSpec((LANES, HEAD_DIM), lambda b, i: (b, 0))
    return pl.pallas_call(
        _nsa_kernel,
        out_shape=jax.ShapeDtypeStruct((TOKENS, MIX_WIDTH), BF16),
        grid=(BATCH, nq),
        in_specs=[_zspec_rows('nsa_q', 512, Q_TILE, qidx),
                  _zspec_rows('misc', 128, Q_TILE, qidx),
                  cmp_spec, cmp_spec,
                  _zspec_rows('nsa_ks', 128, SEQ, kidx),
                  _zspec_rows('nsa_vs', 128, SEQ, kidx),
                  _zspec_rows('nsa_kw', 128, SEQ, kidx),
                  _zspec_rows('nsa_vw', 128, SEQ, kidx),
                  pl.BlockSpec((LANES, SEQ), lambda b, i: (0, 0)),
                  _SLAB_SPEC, _TRI_SPEC],
        out_specs=pl.BlockSpec((Q_TILE, MIX_WIDTH), lambda b, i: (b * nq + i, 0)),
        scratch_shapes=_ATTN_SCRATCH,
        compiler_params=_cparams(2, VMEM_LIMIT_BYTES),
        name="nsa_attention",
    )(z, z, k_cmp, v_cmp, z, z, z, z, cov, slabs, tri)


SB_TILE = 256
SB_HEADS_PER_STEP = 2


def _sb_kernel(q_ref, k_ref, v_ref, u_ref, o_ref):
    qb = pl.program_id(2)
    scale = HEAD_DIM ** -0.5
    upper = u_ref[...]
    row = qb * SB_TILE + lax.broadcasted_iota(I32, (SB_TILE, SB_TILE), 0)
    lane = lax.broadcasted_iota(I32, (SB_TILE, SB_TILE), 1)

    def body(i, carry):
        c = qb - i
        start = pl.multiple_of(c * SB_TILE, SB_TILE)
        strict = (c * SB_TILE + lane) < row
        out = []
        for g in range(SB_HEADS_PER_STEP):
            later, acc = carry[g]
            cols = slice(g * HEAD_DIM, (g + 1) * HEAD_DIM)
            zl = _dot_nt(q_ref[:, cols], k_ref[pl.ds(start, SB_TILE), cols]) * scale
            log_beta = jnp.minimum(zl, 0.0) - jnp.log1p(jnp.exp(-jnp.abs(zl)))
            log_keep = jnp.where(strict, log_beta - zl, 0.0)
            keep_hi = log_keep.astype(BF16)
            keep_lo = (log_keep - keep_hi.astype(F32)).astype(BF16)
            within = _dot(keep_hi, upper) + _dot(keep_lo, upper)
            a = jnp.where(strict, jnp.exp(log_beta + within + later), 0.0)
            acc = acc + _dot(a.astype(BF16), v_ref[pl.ds(start, SB_TILE), cols])
            later = later + jnp.sum(log_keep, axis=1, keepdims=True)
            out.append((later, acc))
        return tuple(out)

    init = tuple((jnp.zeros((SB_TILE, 1), F32), jnp.zeros((SB_TILE, HEAD_DIM), F32))
                 for _ in range(SB_HEADS_PER_STEP))
    res = lax.fori_loop(0, qb + 1, body, init)
    o_ref[...] = jnp.concatenate([acc for _, acc in res], axis=1).astype(o_ref.dtype)


def _stick_breaking(z, upper):
    nq = SEQ // SB_TILE
    width = SB_HEADS_PER_STEP * HEAD_DIM
    qcb, kcb, vcb = _OFF['sb_q'] // width, _OFF['sb_k'] // width, _OFF['sb_v'] // width
    return pl.pallas_call(
        _sb_kernel,
        out_shape=jax.ShapeDtypeStruct((TOKENS, MIX_WIDTH), BF16),
        grid=(BATCH, N_HEADS // SB_HEADS_PER_STEP, nq),
        in_specs=[pl.BlockSpec((SB_TILE, width), lambda b, h, i: (b * nq + i, qcb + h)),
                  pl.BlockSpec((SEQ, width), lambda b, h, i: (b, kcb + h)),
                  pl.BlockSpec((SEQ, width), lambda b, h, i: (b, vcb + h)),
                  pl.BlockSpec((SB_TILE, SB_TILE), lambda b, h, i: (0, 0))],
        out_specs=pl.BlockSpec((SB_TILE, width), lambda b, h, i: (b * nq + i, h)),
        compiler_params=_cparams(3, VMEM_LIMIT_BYTES),
        name="stick_breaking_attention",
    )(z, z, z, upper)


MLA_QK = 2 * HEAD_DIM


def _rope128(x, cos, nsin, psin):
    return x * cos + pltpu.roll(x, 96, 1) * nsin + pltpu.roll(x, 32, 1) * psin


def _mla_prep_kernel(cq_ref, ckv_ref, kr_ref, qn_ref, kvn_ref, wqn_ref, wqp_ref, wuk_ref, wuv_ref,
                     cos_ref, nsin_ref, psin_ref, q_ref, k_ref, v_ref):
    def rms(x, g):
        return (x * lax.rsqrt(jnp.mean(x * x, axis=-1, keepdims=True) + RMS_EPS) * g).astype(BF16)

    cos, nsin, psin = cos_ref[...], nsin_ref[...], psin_ref[...]
    xq = rms(cq_ref[...].astype(F32), qn_ref[...])
    xc = rms(ckv_ref[...].astype(F32), kvn_ref[...])
    q_nope = _dot(xq, wqn_ref[...])
    q_rope = _dot(xq, wqp_ref[...])
    k_nope = _dot(xc, wuk_ref[...])
    k_rope = _rope128(kr_ref[...].astype(F32), cos, nsin, psin)
    q_parts, k_parts = [], []
    for h in range(N_HEADS):
        cols = slice(h * HEAD_DIM, (h + 1) * HEAD_DIM)
        q_parts += [q_nope[:, cols], _rope128(q_rope[:, cols], cos, nsin, psin)]
        k_parts += [k_nope[:, cols], k_rope]
    q_ref[...] = jnp.concatenate(q_parts, axis=1).astype(BF16)
    k_ref[...] = jnp.concatenate(k_parts, axis=1).astype(BF16)
    v_ref[...] = _dot(xc, wuv_ref[...]).astype(BF16)


def _mla_prep(z, qn, kvn, wqn, wqp, wuk, wuv, cos, nsin, psin, *, tm):
    nt = SEQ // tm

    def zidx(cb, i):
        return (i, cb)

    wspec = pl.BlockSpec((Q_LORA, 512), lambda i: (0, 0))
    nspec = pl.BlockSpec((1, 512), lambda i: (0, 0))
    tspec = pl.BlockSpec((tm, LANES), lambda i: (i % nt, 0))
    return pl.pallas_call(
        _mla_prep_kernel,
        out_shape=(jax.ShapeDtypeStruct((TOKENS, N_HEADS * MLA_QK), BF16),
                   jax.ShapeDtypeStruct((TOKENS, N_HEADS * MLA_QK), BF16),
                   jax.ShapeDtypeStruct((TOKENS, N_HEADS * HEAD_DIM), BF16)),
        grid=(TOKENS // tm,),
        in_specs=[_zspec_rows('mla_cq', 512, tm, zidx), _zspec_rows('mla_ckv', 512, tm, zidx),
                  _zspec_rows('mla_kr', 128, tm, zidx), nspec, nspec, wspec, wspec, wspec, wspec,
                  tspec, tspec, tspec],
        out_specs=(pl.BlockSpec((tm, N_HEADS * MLA_QK), lambda i: (i, 0)),
                   pl.BlockSpec((tm, N_HEADS * MLA_QK), lambda i: (i, 0)),
                   pl.BlockSpec((tm, N_HEADS * HEAD_DIM), lambda i: (i, 0))),
        compiler_params=_cparams(1, VMEM_LIMIT_BYTES),
        name="mla_prep",
    )(z, z, z, qn, kvn, wqn, wqp, wuk, wuv, cos, nsin, psin)


FLASH_TILE = 256


def _flash_kernel(q_ref, k_ref, v_ref, o_ref, s_scr, *, scale, causal, n_kv):
    qb = pl.program_id(2)
    q = q_ref[...]
    row = qb * FLASH_TILE + lax.broadcasted_iota(I32, (FLASH_TILE, FLASH_TILE), 0)
    lane = lax.broadcasted_iota(I32, (FLASH_TILE, FLASH_TILE), 1)
    for c in range(n_kv):
        cols = slice(c * FLASH_TILE, (c + 1) * FLASH_TILE)
        s = _dot_nt(q, k_ref[cols, :]) * scale
        if causal:
            s = jnp.where(c * FLASH_TILE + lane <= row, s, NEG_BIG)
        s_scr[:, cols] = s
    s = s_scr[...]
    p = jnp.exp(s - jnp.max(s, axis=1, keepdims=True))
    o = _dot(p.astype(BF16), v_ref[...]) / jnp.sum(p, axis=1, keepdims=True)
    o_ref[...] = o.astype(o_ref.dtype)


def _flash(q, k, v, *, dk, kv_len, scale, causal, name):
    nq = SEQ // FLASH_TILE
    return pl.pallas_call(
        functools.partial(_flash_kernel, scale=scale, causal=causal, n_kv=kv_len // FLASH_TILE),
        out_shape=jax.ShapeDtypeStruct((TOKENS, N_HEADS * HEAD_DIM), BF16),
        grid=(BATCH, N_HEADS, nq),
        in_specs=[pl.BlockSpec((FLASH_TILE, dk), lambda b, h, i: (b * nq + i, h)),
                  pl.BlockSpec((kv_len, dk), lambda b, h, i: (b, h)),
                  pl.BlockSpec((kv_len, HEAD_DIM), lambda b, h, i: (b, h))],
        out_specs=pl.BlockSpec((FLASH_TILE, HEAD_DIM), lambda b, h, i: (b * nq + i, h)),
        scratch_shapes=[pltpu.VMEM((FLASH_TILE, kv_len), F32)],
        compiler_params=_cparams(3, VMEM_LIMIT_BYTES),
        name=name,
    )(q, k, v)


FFN_TM = 1024
FFN_SUB = 256
FFN_SUB_SHIFT = 8
DMA_UNROLL_SHIFT = 3
DMA_UNROLL = 1 << DMA_UNROLL_SHIFT


def _swiglu_step(load_x, nv, j, wg_ref, wu_ref, wd_ref, o_ref, wg_b, wu_b, wd_b):
    wg_b[...] = wg_ref[0].astype(BF16)
    wu_b[...] = wu_ref[0].astype(BF16)
    wd_b[...] = wd_ref[0].astype(BF16)

    @pl.when(j == 0)
    def _():
        o_ref[...] = jnp.zeros(o_ref.shape, F32)

    def accumulate(rows):
        x = load_x(rows)
        g = _dot(x, wg_b[...])
        u = _dot(x, wu_b[...])
        o_ref[rows, :] += _dot((g * jax.nn.sigmoid(g) * u).astype(BF16), wd_b[...])

    n_sub = jnp.right_shift(nv + (FFN_SUB - 1), FFN_SUB_SHIFT)
    for n in range(1, FFN_TM // FFN_SUB + 1):
        @pl.when(n_sub == n)
        def _():
            for s in range(n):
                accumulate(slice(s * FFN_SUB, (s + 1) * FFN_SUB))


def _ffn_kernel(te_ref, nv_ref, x_ref, wg_ref, wu_ref, wd_ref, o_ref, wg_b, wu_b, wd_b):
    del te_ref
    nv = nv_ref[pl.program_id(0)]
    _swiglu_step(lambda rows: x_ref[rows, :], nv, pl.program_id(1), wg_ref, wu_ref, wd_ref, o_ref, wg_b, wu_b, wd_b)


def _moe_ffn_kernel(te_ref, nv_ref, tok_ref, h_hbm, wg_ref, wu_ref, wd_ref, o_ref, x_buf, wg_b, wu_b, wd_b, sem):
    del te_ref
    i = pl.program_id(0)
    j = pl.program_id(1)
    nv = nv_ref[i]

    @pl.when((j == 0) & (nv > 0))
    def _():
        n_rows = jnp.left_shift(jnp.right_shift(nv + (FFN_SUB - 1), FFN_SUB_SHIFT), FFN_SUB_SHIFT)

        def row_copy(tok, r):
            return pltpu.make_async_copy(h_hbm.at[pl.ds(tok, 1)], x_buf.at[pl.ds(r, 1)], sem)

        def start(g, carry):
            for u in range(DMA_UNROLL):
                r = g * DMA_UNROLL + u
                row_copy(tok_ref[i * FFN_TM + r], r).start()
            return carry

        lax.fori_loop(0, jnp.right_shift(n_rows, DMA_UNROLL_SHIFT), start, 0)

        def wait(g, carry):
            for _ in range(DMA_UNROLL):
                row_copy(0, 0).wait()
            return carry

        lax.fori_loop(0, jnp.right_shift(n_rows, DMA_UNROLL_SHIFT), wait, 0)

    _swiglu_step(lambda rows: x_buf[rows, :].astype(BF16), nv, j, wg_ref, wu_ref, wd_ref, o_ref, wg_b, wu_b, wd_b)


def _ffn_specs(tf, n_prefetch):
    nff = D_FF // tf

    def jj(i, j, nv):
        return jnp.where(nv[i] > 0, j, nff - 1)

    w_in = pl.BlockSpec((1, D_MODEL, tf), lambda i, j, te, nv, *_: (te[i], 0, jj(i, j, nv)))
    w_out = pl.BlockSpec((1, tf, D_MODEL), lambda i, j, te, nv, *_: (te[i], jj(i, j, nv), 0))
    out = pl.BlockSpec((FFN_TM, D_MODEL), lambda i, j, *_: (i, 0))
    scratch = [pltpu.VMEM((D_MODEL, tf), BF16), pltpu.VMEM((D_MODEL, tf), BF16), pltpu.VMEM((tf, D_MODEL), BF16)]
    return nff, w_in, w_out, out, scratch


def _grouped_swiglu(tile_expert, tile_valid, x, wg, wu, wd, *, tf, name):
    r = x.shape[0]
    nff, w_in, w_out, out, scratch = _ffn_specs(tf, 2)
    grid_spec = pltpu.PrefetchScalarGridSpec(
        num_scalar_prefetch=2,
        grid=(r // FFN_TM, nff),
        in_specs=[pl.BlockSpec((FFN_TM, D_MODEL), lambda i, j, *_: (i, 0)), w_in, w_in, w_out],
        out_specs=out,
        scratch_shapes=scratch,
    )
    return pl.pallas_call(
        _ffn_kernel,
        out_shape=jax.ShapeDtypeStruct((r, D_MODEL), F32),
        grid_spec=grid_spec,
        compiler_params=_cparams(2, VMEM_LIMIT_BYTES),
        name=name,
    )(tile_expert, tile_valid, x, wg, wu, wd)


def _gathered_swiglu(tile_expert, tile_valid, row_token, h, wg, wu, wd, *, tf, name):
    r = row_token.shape[0]
    nff, w_in, w_out, out, scratch = _ffn_specs(tf, 3)
    grid_spec = pltpu.PrefetchScalarGridSpec(
        num_scalar_prefetch=3,
        grid=(r // FFN_TM, nff),
        in_specs=[pl.BlockSpec(memory_space=pl.ANY), w_in, w_in, w_out],
        out_specs=out,
        scratch_shapes=[pltpu.VMEM((FFN_TM, D_MODEL), F32)] + scratch + [pltpu.SemaphoreType.DMA(())],
    )
    return pl.pallas_call(
        _moe_ffn_kernel,
        out_shape=jax.ShapeDtypeStruct((r, D_MODEL), F32),
        grid_spec=grid_spec,
        compiler_params=_cparams(2, VMEM_LIMIT_BYTES),
        name=name,
    )(tile_expert, tile_valid, row_token, h, wg, wu, wd)


ROUTER_TM = 256
META_E0, META_E1, META_W0, META_W1, META_R0, META_R1 = range(6)


def _router_kernel(h_ref, r_ref, lt_ref, meta_ref, cnt_ref, run_scr):
    i = pl.program_id(0)

    @pl.when(i == 0)
    def _():
        run_scr[...] = jnp.zeros(run_scr.shape, F32)

    def split(x):
        hi = x.astype(BF16)
        return hi, (x - hi.astype(F32)).astype(BF16)

    h_hi, h_lo = split(h_ref[...])
    r_hi, r_lo = split(r_ref[...])
    logits = _dot(h_hi, r_hi) + (_dot(h_hi, r_lo) + _dot(h_lo, r_hi))
    lane = lax.broadcasted_iota(I32, (ROUTER_TM, LANES), 1).astype(F32)
    logits = jnp.where(lane < N_EXPERTS, logits, -jnp.inf)

    def top1(x):
        m = jnp.max(x, axis=1, keepdims=True)
        idx = jnp.min(jnp.where(x == m, lane, float(LANES)), axis=1, keepdims=True)
        return m, idx

    m0, e0 = top1(logits)
    m1, e1 = top1(jnp.where(lane == e0, -jnp.inf, logits))
    ex = jnp.exp(m1 - m0)
    w0 = 1.0 / (1.0 + ex)
    w1 = ex / (1.0 + ex)

    hot0 = jnp.where(lane == e0, 1.0, 0.0)
    hot1 = jnp.where(lane == e1, 1.0, 0.0)
    before0 = _dot(lt_ref[...], hot0.astype(BF16)) + run_scr[0:1, :]
    tot0 = jnp.sum(hot0, axis=0, keepdims=True)
    before1 = _dot(lt_ref[...], hot1.astype(BF16)) + run_scr[0:1, :] + tot0
    r0 = jnp.sum(hot0 * before0, axis=1, keepdims=True)
    r1 = jnp.sum(hot1 * before1, axis=1, keepdims=True)
    run_new = run_scr[0:1, :] + tot0 + jnp.sum(hot1, axis=0, keepdims=True)
    run_scr[...] = jnp.broadcast_to(run_new, run_scr.shape)
    cnt_ref[...] = jnp.broadcast_to(run_new, cnt_ref.shape)

    meta = jnp.zeros((ROUTER_TM, LANES), F32)
    for ln, val in ((META_E0, e0), (META_E1, e1), (META_W0, w0), (META_W1, w1),
                    (META_R0, r0), (META_R1, r1)):
        meta = jnp.where(lane == ln, val, meta)
    meta_ref[...] = meta


def _router(h, router_padded, lower_tri):
    return pl.pallas_call(
        _router_kernel,
        out_shape=(jax.ShapeDtypeStruct((TOKENS, LANES), F32), jax.ShapeDtypeStruct((8, LANES), F32)),
        grid=(TOKENS // ROUTER_TM,),
        in_specs=[pl.BlockSpec((ROUTER_TM, D_MODEL), lambda i: (i, 0)),
                  pl.BlockSpec((D_MODEL, LANES), lambda i: (0, 0)),
                  pl.BlockSpec((ROUTER_TM, ROUTER_TM), lambda i: (0, 0))],
        out_specs=(pl.BlockSpec((ROUTER_TM, LANES), lambda i: (i, 0)), pl.BlockSpec((8, LANES), lambda i: (0, 0))),
        scratch_shapes=[pltpu.VMEM((8, LANES), F32)],
        compiler_params=_cparams(1, VMEM_LIMIT_BYTES),
        name="moe_router",
    )(h, router_padded, lower_tri)


COMBINE_TM = 256


def _combine_kernel(rows_ref, y_hbm, h_ref, meta_ref, g_ref, b_ref, o_ref, ob_ref, buf, sem):
    i = pl.program_id(0)

    def row_copy(src, slot, r):
        return pltpu.make_async_copy(y_hbm.at[pl.ds(src, 1)], buf.at[slot, pl.ds(r, 1)], sem)

    def start(g, carry):
        for u in range(DMA_UNROLL):
            r = g * DMA_UNROLL + u
            tok = i * COMBINE_TM + r
            row_copy(rows_ref[tok], 0, r).start()
            row_copy(rows_ref[TOKENS + tok], 1, r).start()
        return carry

    lax.fori_loop(0, COMBINE_TM // DMA_UNROLL, start, 0)

    def wait(g, carry):
        for _ in range(DMA_UNROLL):
            row_copy(0, 0, 0).wait()
            row_copy(0, 1, 0).wait()
        return carry

    lax.fori_loop(0, COMBINE_TM // DMA_UNROLL, wait, 0)
    meta = meta_ref[...]
    f = meta[:, META_W0:META_W0 + 1] * buf[0] + meta[:, META_W1:META_W1 + 1] * buf[1]
    out = _layer_norm_rows(DN_ALPHA * h_ref[...] + f, g_ref[...], b_ref[...])
    o_ref[...] = out
    ob_ref[...] = out.astype(BF16)


def _combine(rows, y, h, meta, g, b, layer):
    grid_spec = pltpu.PrefetchScalarGridSpec(
        num_scalar_prefetch=1,
        grid=(TOKENS // COMBINE_TM,),
        in_specs=[pl.BlockSpec(memory_space=pl.ANY),
                  pl.BlockSpec((COMBINE_TM, D_MODEL), lambda i, rows: (i, 0)),
                  pl.BlockSpec((COMBINE_TM, LANES), lambda i, rows: (i, 0)),
                  pl.BlockSpec((None, 1, D_MODEL), lambda i, rows: (layer, 0, 0)),
                  pl.BlockSpec((None, 1, D_MODEL), lambda i, rows: (layer, 0, 0))],
        out_specs=(pl.BlockSpec((COMBINE_TM, D_MODEL), lambda i, rows: (i, 0)),
                   pl.BlockSpec((COMBINE_TM, D_MODEL), lambda i, rows: (i, 0))),
        scratch_shapes=[pltpu.VMEM((2, COMBINE_TM, D_MODEL), F32), pltpu.SemaphoreType.DMA(())],
    )
    return pl.pallas_call(
        _combine_kernel,
        out_shape=(jax.ShapeDtypeStruct((TOKENS, D_MODEL), F32), jax.ShapeDtypeStruct((TOKENS, D_MODEL), BF16)),
        grid_spec=grid_spec,
        compiler_params=_cparams(1, VMEM_LIMIT_BYTES),
        name="moe_combine",
    )(rows, y, h, meta, g, b)


def _pack_w_in(w_in):
    src = {}
    off = 0
    for name, width in IN_SPLITS:
        src[name] = (off, off + width)
        off += width
    lead = w_in.shape[:-1]

    def cols(name):
        a, b = src[name]
        return w_in[..., a:b]

    def zeros(width):
        return jnp.zeros(lead + (width,), w_in.dtype)

    pieces = []
    for name, width in _PACKED:
        if name == 'idx_k_lo':
            pieces += [cols('idx_k'), zeros(64)]
        elif name == 'idx_k_hi':
            pieces += [zeros(64), cols('idx_k')]
        elif name == 'misc':
            pieces += [cols('idx_w'), cols('nsa_g'), zeros(128 - 28)]
        elif name == 'mla_kr':
            pieces += [cols('mla_kr'), zeros(64)]
        else:
            pieces.append(cols(name))
    return jnp.concatenate(pieces, axis=-1).astype(BF16)


def _t5_bucket_np(dist):
    exact = REL_BUCKETS // 2
    d = np.maximum(dist, 0)
    log_ratio = np.log(np.maximum(d, 1).astype(np.float32) / exact) / math.log(REL_MAX_DIST / exact)
    far = np.minimum(exact + (log_ratio * (REL_BUCKETS - exact)).astype(np.int32), REL_BUCKETS - 1)
    return np.where(d < exact, d, far).astype(np.int32)


def _bias_slabs(table4):
    i = np.arange(Q_TILE)[:, None]
    j = np.arange(K_CHUNK)[None, :]
    bucket = jnp.asarray(np.stack([_t5_bucket_np(i - j - rel) for rel in (-256, -128, 0, -2 * SEQ)]))
    tab = table4.astype(F32)
    out = jnp.zeros((4, N_HEADS, Q_TILE, K_CHUNK), F32)
    for b in range(REL_BUCKETS):
        out = jnp.where((bucket == b)[:, None], tab[b][None, :, None, None], out)
    return out


def _rope_tables():
    half = QK_ROPE // 2
    inv = ROPE_BASE ** (-jnp.arange(half, dtype=F32) / half)
    ang = jnp.arange(SEQ, dtype=F32)[:, None] * inv[None, :]
    cos, sin = jnp.cos(ang), jnp.sin(ang)
    zero = jnp.zeros_like(cos)
    pad = jnp.zeros((SEQ, LANES - QK_ROPE), F32)
    cos_t = jnp.concatenate([cos, cos, pad], axis=1)
    nsin_t = jnp.concatenate([-sin, zero, pad], axis=1)
    psin_t = jnp.concatenate([zero, sin, pad], axis=1)
    return cos_t, nsin_t, psin_t


def _cover_expanded():
    n = np.arange(LANES)[:, None]
    s = np.arange(SEQ)[None, :]
    j = s // SLC_LEN
    cover = (CMP_STRIDE * n < SLC_LEN * j + SLC_LEN) & (CMP_STRIDE * n + CMP_LEN > SLC_LEN * j) & (n < N_CMP)
    return jnp.asarray(cover, BF16)


def _strict_upper(n):
    return jnp.asarray(np.arange(n)[:, None] > np.arange(n)[None, :], BF16)


def _strict_lower(n):
    return jnp.asarray(np.arange(n)[None, :] < np.arange(n)[:, None], BF16)


def _moe_layout(meta, counts):
    n_tiles = 2 * TOKENS // FFN_TM + N_EXPERTS
    cnt = counts[0, :N_EXPERTS].astype(I32)
    padded = ((cnt + FFN_TM - 1) // FFN_TM) * FFN_TM
    ends = jnp.cumsum(padded)
    offs = ends - padded
    e = meta[:, META_E0:META_E1 + 1].astype(I32)
    rank = meta[:, META_R0:META_R1 + 1].astype(I32)
    rows = (offs[e] + rank).T.reshape(-1)
    token = jnp.tile(jnp.arange(TOKENS, dtype=I32), 2)
    row_token = jnp.zeros((n_tiles * FFN_TM,), I32).at[rows].set(token)
    tile_start = jnp.arange(n_tiles, dtype=I32) * FFN_TM
    te = jnp.minimum(jnp.sum(tile_start[:, None] >= ends[None, :], axis=1), N_EXPERTS - 1).astype(I32)
    nv = jnp.clip(cnt[te] - (tile_start - offs[te]), 0, FFN_TM).astype(I32)
    return rows, row_token, te, nv


def kernel(x, mem, rel_table, w_in, mla_q_norm, mla_kv_norm, mla_w_uq, mla_w_uk, mla_w_uv, nsa_pe_k, nsa_pe_v,
           nsa_ck_w1, nsa_ck_w2, nsa_cv_w1, nsa_cv_w2, w_branch, w_out, ln1_g, ln1_b, xa_wq, xa_wk, xa_wv, xa_wo,
           ln2_g, ln2_b, ffn_w_gate, ffn_w_up, ffn_w_down, moe_router, moe_w_gate, moe_w_up, moe_w_down,
           ln3_g, ln3_b):
    h = x.reshape(TOKENS, D_MODEL)
    hb = h.astype(BF16)
    mem_b = mem.reshape(BATCH * MEM_LEN, D_MODEL).astype(BF16)

    w_in_p = _pack_w_in(w_in)
    dsa_slabs = _bias_slabs(rel_table[:, :N_HEADS])
    nsa_slabs = _bias_slabs(rel_table[:, N_HEADS:])
    cos_t, nsin_t, psin_t = _rope_tables()
    cover = _cover_expanded()
    upper = _strict_upper(SB_TILE)
    incl_upper = jnp.asarray(np.arange(K_CHUNK)[:, None] <= np.arange(K_CHUNK)[None, :], BF16)
    lower = _strict_lower(ROUTER_TM)

    uq = mla_w_uq.reshape(DEPTH, Q_LORA, N_HEADS, QK_NOPE + QK_ROPE)
    w_qn = uq[..., :QK_NOPE].reshape(DEPTH, Q_LORA, N_HEADS * QK_NOPE).astype(BF16)
    w_qp = jnp.concatenate([uq[..., QK_NOPE:], jnp.zeros((DEPTH, Q_LORA, N_HEADS, LANES - QK_ROPE), F32)],
                           axis=-1).reshape(DEPTH, Q_LORA, N_HEADS * LANES).astype(BF16)
    w_uk = mla_w_uk.astype(BF16)
    w_uv = mla_w_uv.astype(BF16)
    w_br = w_branch.astype(BF16)
    w_o = w_out.astype(BF16)
    xq, xk, xv, xo = (w.astype(BF16) for w in (xa_wq, xa_wk, xa_wv, xa_wo))
    router_p = jnp.pad(moe_router, ((0, 0), (0, 0), (0, LANES - N_EXPERTS)))
    row2 = lambda a: a.reshape(DEPTH, 1, -1)
    g1, b1, g2, b2, g3, b3 = (row2(a) for a in (ln1_g, ln1_b, ln2_g, ln2_b, ln3_g, ln3_b))
    qn, kvn = row2(mla_q_norm), row2(mla_kv_norm)
    pek = nsa_pe_k.reshape(DEPTH, 1, CMP_LEN * HEAD_DIM)
    pev = nsa_pe_v.reshape(DEPTH, 1, CMP_LEN * HEAD_DIM)
    dense_nv = jnp.full((TOKENS // FFN_TM,), FFN_TM, I32)
    moe_wg = moe_w_gate.reshape((-1,) + moe_w_gate.shape[2:])
    moe_wu = moe_w_up.reshape((-1,) + moe_w_up.shape[2:])
    moe_wd = moe_w_down.reshape((-1,) + moe_w_down.shape[2:])

    for layer in range(DEPTH):
        z = _matmul(hb, w_in_p, layer, tm=1024, tn=1024, out_dtype=BF16, name="in_proj")
        o_dsa = _dsa(z, dsa_slabs, incl_upper)
        o_sb = _stick_breaking(z, upper)
        group = lambda name: z[:, _OFF[name]:_OFF[name] + HEAD_DIM].reshape(BATCH * LANES, CMP_STRIDE * HEAD_DIM)
        k_cmp, v_cmp = _nsa_compress(group('nsa_kc'), group('nsa_vc'), pek[layer], pev[layer],
                                     nsa_ck_w1[layer], nsa_ck_w2[layer], nsa_cv_w1[layer], nsa_cv_w2[layer])
        o_nsa = _nsa(z, k_cmp, v_cmp, cover, nsa_slabs, incl_upper)
        q_cat, k_cat, v_mla = _mla_prep(z, qn[layer], kvn[layer], w_qn[layer], w_qp[layer], w_uk[layer],
                                        w_uv[layer], cos_t, nsin_t, psin_t, tm=512)
        o_mla = _flash(q_cat, k_cat, v_mla, dk=MLA_QK, kv_len=SEQ, scale=(QK_NOPE + QK_ROPE) ** -0.5,
                       causal=True, name="mla_attention")
        mixed = _merge((o_dsa, o_sb, o_nsa, o_mla), z, w_br, layer, tm=512, tn=512)
        h, hb = _matmul_res_ln(mixed, w_o, h, g1, b1, layer, tm=256, name="out_proj_ln")

        q_x = _matmul(hb, xq, layer, tm=1024, tn=512, out_dtype=BF16, name="xa_q_proj")
        k_x = _matmul(mem_b, xk, layer, tm=1024, tn=512, out_dtype=BF16, name="xa_k_proj")
        v_x = _matmul(mem_b, xv, layer, tm=1024, tn=512, out_dtype=BF16, name="xa_v_proj")
        o_x = _flash(q_x, k_x, v_x, dk=HEAD_DIM, kv_len=MEM_LEN, scale=HEAD_DIM ** -0.5, causal=False,
                     name="cross_attention")
        h, hb = _matmul_res_ln(o_x, xo, h, g2, b2, layer, tm=256, name="xa_out_proj_ln")

        i = layer // 2
        if layer % 2 == 0:
            dense_te = jnp.full((TOKENS // FFN_TM,), i, I32)
            y = _grouped_swiglu(dense_te, dense_nv, hb, ffn_w_gate, ffn_w_up, ffn_w_down, tf=256,
                                name="dense_swiglu")
            h, hb = _res_ln(y, h, g3, b3, layer, tm=256, name="ffn_res_ln")
        else:
            meta, counts = _router(h, router_p[i], lower)
            rows, row_token, te, nv = _moe_layout(meta, counts)
            y = _gathered_swiglu(te + i * N_EXPERTS, nv, row_token, h, moe_wg, moe_wu, moe_wd, tf=256,
                                 name="moe_swiglu")
            h, hb = _combine(rows, y, h, meta, g3, b3, layer)
    return h.reshape(BATCH, SEQ, D_MODEL)
```

```python
import functools
import math

import jax
import jax.numpy as jnp
import numpy as np
from jax import lax
from jax.experimental import pallas as pl
from jax.experimental.pallas import tpu as pltpu

F32 = jnp.float32
BF16 = jnp.bfloat16
I32 = jnp.int32

D_MODEL = 2048
BATCH = 4
SEQ = 2048
DEPTH = 4
TOKENS = BATCH * SEQ
MEM_LEN = 256
HEAD_DIM = 128
N_HEADS = 4
DSA_TOPK = min(256, SEQ // 4)
IDX_HEADS = 16
IDX_DIM = 64
CMP_LEN = 32
CMP_STRIDE = 16
CMP_HIDDEN = 256
N_CMP = (SEQ - CMP_LEN) // CMP_STRIDE + 1
SLC_LEN = 64
SLC_SHIFT = 6
N_SLC = SEQ // SLC_LEN
N_SEL = min(16, N_SLC)
WINDOW = 512
FORCE_SCORE = 1.0e4
Q_LORA = 512
KV_LORA = 512
QK_NOPE = 128
QK_ROPE = 64
ROPE_BASE = 10000.0
N_MIXERS = 4
MIX_WIDTH = 512
REL_BUCKETS = 32
REL_MAX_DIST = 128
D_FF = 5632
N_EXPERTS = 8
DN_ALPHA = (2 * DEPTH) ** 0.25
LN_EPS = 1e-5
RMS_EPS = 1e-6
NEG_BIG = -1.0e30

IN_SPLITS = (
    ('dsa_q', 512), ('dsa_k', 128), ('dsa_v', 128),
    ('idx_q', IDX_HEADS * IDX_DIM), ('idx_k', IDX_DIM), ('idx_w', IDX_HEADS),
    ('sb_q', 512), ('sb_k', 512), ('sb_v', 512),
    ('nsa_q', 512),
    ('nsa_kc', 128), ('nsa_vc', 128), ('nsa_ks', 128), ('nsa_vs', 128),
    ('nsa_kw', 128), ('nsa_vw', 128), ('nsa_g', 12),
    ('mla_cq', Q_LORA), ('mla_ckv', KV_LORA), ('mla_kr', QK_ROPE),
    ('gates', N_MIXERS * D_MODEL),
)

LANES = 128
Q_TILE = 128
K_CHUNK = 256
N_CHUNKS = SEQ // K_CHUNK
VMEM_LIMIT_BYTES = 56 * 1024 * 1024
INT_MIN = -2147483648
KEY_NEG_INF = -2139095041

_PACKED = (
    ('idx_q', 1024), ('dsa_q', 512), ('sb_q', 512), ('sb_k', 512), ('sb_v', 512), ('nsa_q', 512),
    ('mla_cq', 512), ('mla_ckv', 512),
    ('dsa_k', 128), ('dsa_v', 128), ('idx_k_lo', 128), ('idx_k_hi', 128), ('misc', 128),
    ('nsa_kc', 128), ('nsa_vc', 128), ('nsa_ks', 128), ('nsa_vs', 128), ('nsa_kw', 128), ('nsa_vw', 128),
    ('mla_kr', 128), ('gates', 8192),
)
_OFF = {}
_o = 0
for _n, _w in _PACKED:
    _OFF[_n] = _o
    _o += _w
Z_WIDTH = _o
MISC_IDXW = 0
MISC_NSAG = 16


def _cparams(n_axes, vmem=None):
    return pltpu.CompilerParams(dimension_semantics=("arbitrary",) * n_axes, vmem_limit_bytes=vmem)


def _dot(a, b):
    return jnp.dot(a, b, preferred_element_type=F32)


def _dot_nt(a, b):
    return lax.dot_general(a, b, (((1,), (1,)), ((), ())), preferred_element_type=F32)


def _layer_norm_rows(v, g, b):
    mu = jnp.mean(v, axis=-1, keepdims=True)
    d = v - mu
    var = jnp.mean(d * d, axis=-1, keepdims=True)
    return d * lax.rsqrt(var + LN_EPS) * g + b


def _matmul_kernel(x_ref, w_ref, o_ref):
    o_ref[...] = _dot(x_ref[...].astype(BF16), w_ref[...].astype(BF16)).astype(o_ref.dtype)


def _matmul(x, w, layer, *, tm, tn, out_dtype, name):
    m, k = x.shape
    n = w.shape[2]
    return pl.pallas_call(
        _matmul_kernel,
        out_shape=jax.ShapeDtypeStruct((m, n), out_dtype),
        grid=(n // tn, m // tm),
        in_specs=[pl.BlockSpec((tm, k), lambda j, i: (i, 0)),
                  pl.BlockSpec((None, k, tn), lambda j, i: (layer, 0, j))],
        out_specs=pl.BlockSpec((tm, tn), lambda j, i: (i, j)),
        compiler_params=_cparams(2, VMEM_LIMIT_BYTES),
        name=name,
    )(x, w)


def _matmul_res_ln_kernel(x_ref, w_ref, h_ref, g_ref, b_ref, o_ref, ob_ref):
    y = _dot(x_ref[...], w_ref[...])
    out = _layer_norm_rows(DN_ALPHA * h_ref[...] + y, g_ref[...], b_ref[...])
    o_ref[...] = out
    ob_ref[...] = out.astype(BF16)


def _matmul_res_ln(x, w, h, g, b, layer, *, tm, name):
    m, k = x.shape
    d = w.shape[2]
    return pl.pallas_call(
        _matmul_res_ln_kernel,
        out_shape=(jax.ShapeDtypeStruct((m, d), F32), jax.ShapeDtypeStruct((m, d), BF16)),
        grid=(m // tm,),
        in_specs=[pl.BlockSpec((tm, k), lambda i: (i, 0)),
                  pl.BlockSpec((None, k, d), lambda i: (layer, 0, 0)),
                  pl.BlockSpec((tm, d), lambda i: (i, 0)),
                  pl.BlockSpec((None, 1, d), lambda i: (layer, 0, 0)),
                  pl.BlockSpec((None, 1, d), lambda i: (layer, 0, 0))],
        out_specs=(pl.BlockSpec((tm, d), lambda i: (i, 0)), pl.BlockSpec((tm, d), lambda i: (i, 0))),
        compiler_params=_cparams(1, VMEM_LIMIT_BYTES),
        name=name,
    )(x, w, h, g, b)


def _res_ln_kernel(y_ref, h_ref, g_ref, b_ref, o_ref, ob_ref):
    out = _layer_norm_rows(DN_ALPHA * h_ref[...] + y_ref[...], g_ref[...], b_ref[...])
    o_ref[...] = out
    ob_ref[...] = out.astype(BF16)


def _res_ln(y, h, g, b, layer, *, tm, name):
    m, d = h.shape
    return pl.pallas_call(
        _res_ln_kernel,
        out_shape=(jax.ShapeDtypeStruct((m, d), F32), jax.ShapeDtypeStruct((m, d), BF16)),
        grid=(m // tm,),
        in_specs=[pl.BlockSpec((tm, d), lambda i: (i, 0)),
                  pl.BlockSpec((tm, d), lambda i: (i, 0)),
                  pl.BlockSpec((None, 1, d), lambda i: (layer, 0, 0)),
                  pl.BlockSpec((None, 1, d), lambda i: (layer, 0, 0))],
        out_specs=(pl.BlockSpec((tm, d), lambda i: (i, 0)), pl.BlockSpec((tm, d), lambda i: (i, 0))),
        compiler_params=_cparams(1, VMEM_LIMIT_BYTES),
        name=name,
    )(y, h, g, b)


def _merge_kernel(b0, b1, b2, b3, g0, g1, g2, g3, wb_ref, o_ref):
    acc = None
    for n, (br, gr) in enumerate(((b0, g0), (b1, g1), (b2, g2), (b3, g3))):
        y = _dot(br[...], wb_ref[n])
        gy = jax.nn.sigmoid(gr[...].astype(F32)) * y
        acc = gy if acc is None else acc + gy
    o_ref[...] = acc.astype(o_ref.dtype)


def _merge(branches, z, wb, layer, *, tm, tn):
    m = z.shape[0]
    gate_specs = []
    for n in range(N_MIXERS):
        base = (_OFF['gates'] + n * D_MODEL) // tn
        gate_specs.append(pl.BlockSpec((tm, tn), lambda j, i, base=base: (i, base + j)))
    return pl.pallas_call(
        _merge_kernel,
        out_shape=jax.ShapeDtypeStruct((m, D_MODEL), BF16),
        grid=(D_MODEL // tn, m // tm),
        in_specs=[pl.BlockSpec((tm, MIX_WIDTH), lambda j, i: (i, 0))] * N_MIXERS + gate_specs
        + [pl.BlockSpec((None, N_MIXERS, MIX_WIDTH, tn), lambda j, i: (layer, 0, 0, j))],
        out_specs=pl.BlockSpec((tm, tn), lambda j, i: (i, j)),
        compiler_params=_cparams(2, VMEM_LIMIT_BYTES),
        name="branch_merge",
    )(*branches, z, z, z, z, wb)


def _sortable_key(score):
    score = jnp.where(score == 0.0, 0.0, score)
    bits = pltpu.bitcast(score, I32)
    return bits ^ (jnp.right_shift(bits, 31) & 0x7FFFFFFF)


def _kth_largest_key(key_scr, k, n_chunks, thr_scr):
    half = Q_TILE // 2

    def search(n):
        def body(i, lo):
            bit = jnp.left_shift(jnp.int32(1), 31 - i)
            out = []
            for g in range(2):
                rows = slice(g * half, (g + 1) * half)
                cand_u = lo[g] | bit
                cand_s = cand_u ^ INT_MIN
                cnt = jnp.zeros((half, K_CHUNK), F32)
                for c in range(n):
                    cnt = cnt + jnp.where(key_scr[c, rows, :] >= cand_s, 1.0, 0.0)
                tot = jnp.sum(cnt, axis=1, keepdims=True)
                out.append(jnp.where(tot >= float(k), cand_u, lo[g]))
            return tuple(out)

        zero = jnp.zeros((half, 1), I32)
        lo = lax.fori_loop(0, 32, body, (zero, zero))
        return jnp.concatenate(lo, axis=0) ^ INT_MIN

    n_even = jnp.left_shift(jnp.right_shift(n_chunks + 1, 1), 1)
    for n in range(2, N_CHUNKS + 1, 2):
        @pl.when(n_even == n)
        def _():
            thr_scr[...] = search(n)

    return thr_scr[...]


def _break_ties(key_scr, thr, k, tri_ref):
    ge = jnp.zeros((Q_TILE, K_CHUNK), F32)
    gt = jnp.zeros((Q_TILE, K_CHUNK), F32)
    for c in range(N_CHUNKS):
        kc = key_scr[c]
        ge = ge + jnp.where(kc >= thr, 1.0, 0.0)
        gt = gt + jnp.where(kc > thr, 1.0, 0.0)
    n_ge = jnp.sum(ge, axis=1, keepdims=True)
    need = float(k) - jnp.sum(gt, axis=1, keepdims=True)
    tied = (n_ge > float(k)) & (thr > KEY_NEG_INF)

    @pl.when(jnp.max(jnp.where(tied, 1.0, 0.0)) > 0.0)
    def _():
        before = jnp.zeros((Q_TILE, 1), F32)
        for c in range(N_CHUNKS):
            kc = key_scr[c]
            eq = kc == thr
            eq_f = jnp.where(eq, 1.0, 0.0)
            rank = _dot(eq_f.astype(BF16), tri_ref[...]) + before
            drop = tied & eq & (rank > need)
            key_scr[c] = jnp.where(drop, INT_MIN, kc)
            before = before + jnp.sum(eq_f, axis=1, keepdims=True)


def _slab_index(rel):
    return jnp.where(rel == 0, 2, jnp.where(rel == -128, 1, jnp.where(rel == -256, 0, 3)))


def _stack_heads(q):
    return jnp.concatenate([q[:, h * HEAD_DIM:(h + 1) * HEAD_DIM] for h in range(N_HEADS)], axis=0)


def _mqa_masked_attention(qs, k_ref, v_ref, slab_ref, t0, chunks, ok_fn, s_scr, p_scr):
    scale = HEAD_DIM ** -0.5
    width = len(chunks) * K_CHUNK
    values = []
    for n, c in enumerate(chunks):
        cols = slice(n * K_CHUNK, (n + 1) * K_CHUNK)
        start = c * K_CHUNK if isinstance(c, int) else pl.multiple_of(jnp.maximum(c, 0) * K_CHUNK, K_CHUNK)
        madd = jnp.where(ok_fn(c), 0.0, NEG_BIG)
        s4 = _dot_nt(qs, k_ref[pl.ds(start, K_CHUNK), :]) * scale
        values.append(v_ref[pl.ds(start, K_CHUNK), :])
        slab = _slab_index(c * K_CHUNK - t0)
        for h in range(N_HEADS):
            s_scr[h, :, cols] = s4[h * Q_TILE:(h + 1) * Q_TILE] + slab_ref[slab, h] + madd
    norms = []
    for h in range(N_HEADS):
        s = s_scr[h, :, :width]
        p = jnp.exp(s - jnp.max(s, axis=1, keepdims=True))
        norms.append(jnp.sum(p, axis=1, keepdims=True))
        p_scr[h * Q_TILE:(h + 1) * Q_TILE, :width] = p.astype(BF16)
    o4 = _dot(p_scr[:, :width], jnp.concatenate(values, axis=0))
    return [o4[h * Q_TILE:(h + 1) * Q_TILE] / norms[h] for h in range(N_HEADS)]


def _row_pos(t0):
    return t0 + lax.broadcasted_iota(I32, (Q_TILE, K_CHUNK), 0)


def _col_pos(c):
    return c * K_CHUNK + lax.broadcasted_iota(I32, (Q_TILE, K_CHUNK), 1)


_ALL_CHUNKS = list(range(N_CHUNKS))
_ATTN_SCRATCH = [pltpu.VMEM((N_CHUNKS, Q_TILE, K_CHUNK), I32),
                 pltpu.VMEM((Q_TILE, 1), I32),
                 pltpu.VMEM((N_HEADS, Q_TILE, SEQ), F32),
                 pltpu.VMEM((N_HEADS * Q_TILE, SEQ), BF16)]
_SLAB_SPEC = pl.BlockSpec((4, N_HEADS, Q_TILE, K_CHUNK), lambda b, i: (0, 0, 0, 0))
_TRI_SPEC = pl.BlockSpec((K_CHUNK, K_CHUNK), lambda b, i: (0, 0))


def _dsa_kernel(q_ref, iq_ref, misc_ref, k_ref, v_ref, iklo_ref, ikhi_ref, slab_ref, tri_ref, o_ref,
                key_scr, thr_scr, s_scr, p_scr):
    qb = pl.program_id(1)
    t0 = qb * Q_TILE
    n_chunks = qb // 2 + 1
    misc = misc_ref[...].astype(F32)
    row = t0 + lax.broadcasted_iota(I32, (Q_TILE, K_CHUNK), 0)

    key_scr[...] = jnp.full(key_scr.shape, KEY_NEG_INF, I32)

    pairs = jnp.concatenate([iq_ref[:, p * LANES:(p + 1) * LANES] for p in range(IDX_HEADS // 2)], axis=0)

    def score_body(c, carry):
        start = pl.multiple_of(c * K_CHUNK, K_CHUNK)
        s_lo = _dot_nt(pairs, iklo_ref[pl.ds(start, K_CHUNK), :])
        s_hi = _dot_nt(pairs, ikhi_ref[pl.ds(start, K_CHUNK), :])
        acc = jnp.zeros((Q_TILE, K_CHUNK), F32)
        for p in range(IDX_HEADS // 2):
            rows = slice(p * Q_TILE, (p + 1) * Q_TILE)
            w0 = misc[:, MISC_IDXW + 2 * p:MISC_IDXW + 2 * p + 1]
            w1 = misc[:, MISC_IDXW + 2 * p + 1:MISC_IDXW + 2 * p + 2]
            acc = acc + jnp.maximum(s_lo[rows], 0.0) * w0
            acc = acc + jnp.maximum(s_hi[rows], 0.0) * w1
        col = c * K_CHUNK + lax.broadcasted_iota(I32, (Q_TILE, K_CHUNK), 1)
        score = jnp.where(col <= row, acc, -jnp.inf)
        key_scr[c] = _sortable_key(score)
        return carry

    lax.fori_loop(0, n_chunks, score_body, 0)
    thr = _kth_largest_key(key_scr, DSA_TOPK, n_chunks, thr_scr)
    _break_ties(key_scr, thr, DSA_TOPK, tri_ref)

    def ok_fn(c):
        return (key_scr[c] >= thr) & (_col_pos(c) <= row)

    o = _mqa_masked_attention(_stack_heads(q_ref[...]), k_ref, v_ref, slab_ref, t0, _ALL_CHUNKS, ok_fn, s_scr, p_scr)
    o_ref[...] = jnp.concatenate(o, axis=1).astype(o_ref.dtype)


def _zspec_rows(name, width, rows, index_fn):
    cb, rem = divmod(_OFF[name], width)
    assert rem == 0, name
    return pl.BlockSpec((rows, width), functools.partial(index_fn, cb))


def _dsa(z, slabs, tri):
    nq = SEQ // Q_TILE

    def qidx(cb, b, i):
        return (b * nq + i, cb)

    def kidx(cb, b, i):
        return (b, cb)

    return pl.pallas_call(
        _dsa_kernel,
        out_shape=jax.ShapeDtypeStruct((TOKENS, MIX_WIDTH), BF16),
        grid=(BATCH, nq),
        in_specs=[_zspec_rows('dsa_q', 512, Q_TILE, qidx),
                  _zspec_rows('idx_q', 1024, Q_TILE, qidx),
                  _zspec_rows('misc', 128, Q_TILE, qidx),
                  _zspec_rows('dsa_k', 128, SEQ, kidx),
                  _zspec_rows('dsa_v', 128, SEQ, kidx),
                  _zspec_rows('idx_k_lo', 128, SEQ, kidx),
                  _zspec_rows('idx_k_hi', 128, SEQ, kidx),
                  _SLAB_SPEC, _TRI_SPEC],
        out_specs=pl.BlockSpec((Q_TILE, MIX_WIDTH), lambda b, i: (b * nq + i, 0)),
        scratch_shapes=_ATTN_SCRATCH,
        compiler_params=_cparams(2, VMEM_LIMIT_BYTES),
        name="dsa_attention",
    )(z, z, z, z, z, z, z, slabs, tri)


def _nsa_compress_kernel(xk_ref, xv_ref, pek_ref, pev_ref, kw1_ref, kw2_ref, vw1_ref, vw2_ref, ok_ref, ov_ref):
    half = CMP_STRIDE * HEAD_DIM

    def compress(x_ref, pe_ref, w1_ref, w2_ref):
        x = x_ref[...].astype(F32)
        a = _dot((x + pe_ref[:, :half]).astype(BF16), w1_ref[:half, :].astype(BF16))
        b = _dot((x + pe_ref[:, half:]).astype(BF16), w1_ref[half:, :].astype(BF16))
        hid = a + pltpu.roll(b, b.shape[0] - 1, 0)
        return _dot(jax.nn.gelu(hid).astype(BF16), w2_ref[...].astype(BF16))

    ok_ref[...] = compress(xk_ref, pek_ref, kw1_ref, kw2_ref)
    ov_ref[...] = compress(xv_ref, pev_ref, vw1_ref, vw2_ref)


def _nsa_compress(xk, xv, pek, pev, kw1, kw2, vw1, vw2):
    rows = xk.shape[0]
    full = lambda a: pl.BlockSpec(a.shape, lambda i: (0,) * a.ndim)
    args = (xk, xv, pek, pev, kw1, kw2, vw1, vw2)
    return pl.pallas_call(
        _nsa_compress_kernel,
        out_shape=(jax.ShapeDtypeStruct((rows, HEAD_DIM), F32), jax.ShapeDtypeStruct((rows, HEAD_DIM), F32)),
        grid=(1,),
        in_specs=[full(a) for a in args],
        out_specs=(pl.BlockSpec((rows, HEAD_DIM), lambda i: (0, 0)), pl.BlockSpec((rows, HEAD_DIM), lambda i: (0, 0))),
        compiler_params=_cparams(1, VMEM_LIMIT_BYTES),
        name="nsa_compress",
    )(*args)


def _nsa_kernel(q_ref, misc_ref, kc_ref, vc_ref, ks_ref, vs_ref, kw_ref, vw_ref, cov_ref, slab_ref, tri_ref, o_ref,
                key_scr, thr_scr, s_scr, p_scr):
    qb = pl.program_id(1)
    t0 = qb * Q_TILE
    scale = HEAD_DIM ** -0.5
    qs = _stack_heads(q_ref[...])
    misc = misc_ref[...].astype(F32)

    n_idx = lax.broadcasted_iota(I32, (N_HEADS * Q_TILE, LANES), 1)
    t_idx = t0 + (lax.broadcasted_iota(I32, (N_HEADS * Q_TILE, LANES), 0) & (Q_TILE - 1))
    cmp_ok = (CMP_STRIDE * n_idx + CMP_LEN - 1) <= t_idx
    lc = jnp.where(cmp_ok, _dot_nt(qs, kc_ref[...].astype(BF16)) * scale, NEG_BIG)
    e = jnp.exp(lc - jnp.max(lc, axis=1, keepdims=True))
    p_cmp = jnp.where(cmp_ok, e / jnp.sum(e, axis=1, keepdims=True), 0.0)
    o_cmp = _dot(p_cmp.astype(BF16), vc_ref[...].astype(BF16))

    p_sum = p_cmp[0:Q_TILE] + p_cmp[Q_TILE:2 * Q_TILE] + p_cmp[2 * Q_TILE:3 * Q_TILE] + p_cmp[3 * Q_TILE:]
    p_hi = p_sum.astype(BF16)
    p_lo = (p_sum - p_hi.astype(F32)).astype(BF16)
    cur = jnp.right_shift(t0 + lax.broadcasted_iota(I32, (Q_TILE, K_CHUNK), 0), SLC_SHIFT)
    for c in range(N_CHUNKS):
        cov = cov_ref[:, c * K_CHUNK:(c + 1) * K_CHUNK]
        imp = _dot(p_hi, cov) + _dot(p_lo, cov)
        jb = jnp.right_shift(c * K_CHUNK + lax.broadcasted_iota(I32, (Q_TILE, K_CHUNK), 1), SLC_SHIFT)
        forced = (jb == 0) | (jb == cur) | (jb == cur - 1)
        imp = jnp.where(jb <= cur, imp + jnp.where(forced, FORCE_SCORE, 0.0), -jnp.inf)
        key_scr[c] = _sortable_key(imp)
    thr = _kth_largest_key(key_scr, N_SEL * SLC_LEN, qb // 2 + 1, thr_scr)
    _break_ties(key_scr, thr, N_SEL * SLC_LEN, tri_ref)
    row = _row_pos(t0)

    def sel_ok(c):
        return (key_scr[c] >= thr) & (_col_pos(c) <= row)

    o_slc = _mqa_masked_attention(qs, ks_ref, vs_ref, slab_ref, t0, _ALL_CHUNKS, sel_ok, s_scr, p_scr)

    def win_ok(c):
        col = _col_pos(c)
        dist = row - col
        return (dist >= 0) & (dist < WINDOW) & (col >= 0)

    win_chunks = [qb // 2 - 2 + n for n in range(3)]
    o_win = _mqa_masked_attention(qs, kw_ref, vw_ref, slab_ref, t0, win_chunks, win_ok, s_scr, p_scr)

    outs = []
    for h in range(N_HEADS):
        g = jax.nn.sigmoid(misc[:, MISC_NSAG + 3 * h:MISC_NSAG + 3 * h + 3])
        rows = slice(h * Q_TILE, (h + 1) * Q_TILE)
        outs.append(g[:, 0:1] * o_cmp[rows] + g[:, 1:2] * o_slc[h] + g[:, 2:3] * o_win[h])
    o_ref[...] = jnp.concatenate(outs, axis=1).astype(o_ref.dtype)


def _nsa(z, k_cmp, v_cmp, cov, slabs, tri):
    nq = SEQ // Q_TILE

    def qidx(cb, b, i):
        return (b * nq + i, cb)

    def kidx(cb, b, i):
        return (b, cb)

    cmp_spec = pl.BlockSpec((LANES, HEAD_DIM), lambda b, i: (b, 0))
    return pl.pallas_call(
        _nsa_kernel,
        out_shape=jax.ShapeDtypeStruct((TOKENS, MIX_WIDTH), BF16),
        grid=(BATCH, nq),
        in_specs=[_zspec_rows('nsa_q', 512, Q_TILE, qidx),
                  _zspec_rows('misc', 128, Q_TILE, qidx),
                  cmp_spec, cmp_spec,
                  _zspec_rows('nsa_ks', 128, SEQ, kidx),
                  _zspec_rows('nsa_vs', 128, SEQ, kidx),
                  _zspec_rows('nsa_kw', 128, SEQ, kidx),
                  _zspec_rows('nsa_vw', 128, SEQ, kidx),
                  pl.BlockSpec((LANES, SEQ), lambda b, i: (0, 0)),
                  _SLAB_SPEC, _TRI_SPEC],
        out_specs=pl.BlockSpec((Q_TILE, MIX_WIDTH), lambda b, i: (b * nq + i, 0)),
        scratch_shapes=_ATTN_SCRATCH,
        compiler_params=_cparams(2, VMEM_LIMIT_BYTES),
        name="nsa_attention",
    )(z, z, k_cmp, v_cmp, z, z, z, z, cov, slabs, tri)


SB_TILE = 256
SB_HEADS_PER_STEP = 2


def _sb_kernel(q_ref, k_ref, v_ref, u_ref, o_ref):
    qb = pl.program_id(2)
    scale = HEAD_DIM ** -0.5
    upper = u_ref[...]
    row = qb * SB_TILE + lax.broadcasted_iota(I32, (SB_TILE, SB_TILE), 0)
    lane = lax.broadcasted_iota(I32, (SB_TILE, SB_TILE), 1)

    def body(i, carry):
        c = qb - i
        start = pl.multiple_of(c * SB_TILE, SB_TILE)
        strict = (c * SB_TILE + lane) < row
        out = []
        for g in range(SB_HEADS_PER_STEP):
            later, acc = carry[g]
            cols = slice(g * HEAD_DIM, (g + 1) * HEAD_DIM)
            zl = _dot_nt(q_ref[:, cols], k_ref[pl.ds(start, SB_TILE), cols]) * scale
            log_beta = jnp.minimum(zl, 0.0) - jnp.log1p(jnp.exp(-jnp.abs(zl)))
            log_keep = jnp.where(strict, log_beta - zl, 0.0)
            keep_hi = log_keep.astype(BF16)
            keep_lo = (log_keep - keep_hi.astype(F32)).astype(BF16)
            within = _dot(keep_hi, upper) + _dot(keep_lo, upper)
            a = jnp.where(strict, jnp.exp(log_beta + within + later), 0.0)
            acc = acc + _dot(a.astype(BF16), v_ref[pl.ds(start, SB_TILE), cols])
            later = later + jnp.sum(log_keep, axis=1, keepdims=True)
            out.append((later, acc))
        return tuple(out)

    init = tuple((jnp.zeros((SB_TILE, 1), F32), jnp.zeros((SB_TILE, HEAD_DIM), F32))
                 for _ in range(SB_HEADS_PER_STEP))
    res = lax.fori_loop(0, qb + 1, body, init)
    o_ref[...] = jnp.concatenate([acc for _, acc in res], axis=1).astype(o_ref.dtype)


def _stick_breaking(z, upper):
    nq = SEQ // SB_TILE
    width = SB_HEADS_PER_STEP * HEAD_DIM
    qcb, kcb, vcb = _OFF['sb_q'] // width, _OFF['sb_k'] // width, _OFF['sb_v'] // width
    return pl.pallas_call(
        _sb_kernel,
        out_shape=jax.ShapeDtypeStruct((TOKENS, MIX_WIDTH), BF16),
        grid=(BATCH, N_HEADS // SB_HEADS_PER_STEP, nq),
        in_specs=[pl.BlockSpec((SB_TILE, width), lambda b, h, i: (b * nq + i, qcb + h)),
                  pl.BlockSpec((SEQ, width), lambda b, h, i: (b, kcb + h)),
                  pl.BlockSpec((SEQ, width), lambda b, h, i: (b, vcb + h)),
                  pl.BlockSpec((SB_TILE, SB_TILE), lambda b, h, i: (0, 0))],
        out_specs=pl.BlockSpec((SB_TILE, width), lambda b, h, i: (b * nq + i, h)),
        compiler_params=_cparams(3, VMEM_LIMIT_BYTES),
        name="stick_breaking_attention",
    )(z, z, z, upper)


MLA_QK = 2 * HEAD_DIM


def _rope128(x, cos, nsin, psin):
    return x * cos + pltpu.roll(x, 96, 1) * nsin + pltpu.roll(x, 32, 1) * psin


def _mla_prep_kernel(cq_ref, ckv_ref, kr_ref, qn_ref, kvn_ref, wqn_ref, wqp_ref, wuk_ref, wuv_ref,
                     cos_ref, nsin_ref, psin_ref, q_ref, k_ref, v_ref):
    def rms(x, g):
        return (x * lax.rsqrt(jnp.mean(x * x, axis=-1, keepdims=True) + RMS_EPS) * g).astype(BF16)

    cos, nsin, psin = cos_ref[...], nsin_ref[...], psin_ref[...]
    xq = rms(cq_ref[...].astype(F32), qn_ref[...])
    xc = rms(ckv_ref[...].astype(F32), kvn_ref[...])
    q_nope = _dot(xq, wqn_ref[...])
    q_rope = _dot(xq, wqp_ref[...])
    k_nope = _dot(xc, wuk_ref[...])
    k_rope = _rope128(kr_ref[...].astype(F32), cos, nsin, psin)
    q_parts, k_parts = [], []
    for h in range(N_HEADS):
        cols = slice(h * HEAD_DIM, (h + 1) * HEAD_DIM)
        q_parts += [q_nope[:, cols], _rope128(q_rope[:, cols], cos, nsin, psin)]
        k_parts += [k_nope[:, cols], k_rope]
    q_ref[...] = jnp.concatenate(q_parts, axis=1).astype(BF16)
    k_ref[...] = jnp.concatenate(k_parts, axis=1).astype(BF16)
    v_ref[...] = _dot(xc, wuv_ref[...]).astype(BF16)


def _mla_prep(z, qn, kvn, wqn, wqp, wuk, wuv, cos, nsin, psin, *, tm):
    nt = SEQ // tm

    def zidx(cb, i):
        return (i, cb)

    wspec = pl.BlockSpec((Q_LORA, 512), lambda i: (0, 0))
    nspec = pl.BlockSpec((1, 512), lambda i: (0, 0))
    tspec = pl.BlockSpec((tm, LANES), lambda i: (i % nt, 0))
    return pl.pallas_call(
        _mla_prep_kernel,
        out_shape=(jax.ShapeDtypeStruct((TOKENS, N_HEADS * MLA_QK), BF16),
                   jax.ShapeDtypeStruct((TOKENS, N_HEADS * MLA_QK), BF16),
                   jax.ShapeDtypeStruct((TOKENS, N_HEADS * HEAD_DIM), BF16)),
        grid=(TOKENS // tm,),
        in_specs=[_zspec_rows('mla_cq', 512, tm, zidx), _zspec_rows('mla_ckv', 512, tm, zidx),
                  _zspec_rows('mla_kr', 128, tm, zidx), nspec, nspec, wspec, wspec, wspec, wspec,
                  tspec, tspec, tspec],
        out_specs=(pl.BlockSpec((tm, N_HEADS * MLA_QK), lambda i: (i, 0)),
                   pl.BlockSpec((tm, N_HEADS * MLA_QK), lambda i: (i, 0)),
                   pl.BlockSpec((tm, N_HEADS * HEAD_DIM), lambda i: (i, 0))),
        compiler_params=_cparams(1, VMEM_LIMIT_BYTES),
        name="mla_prep",
    )(z, z, z, qn, kvn, wqn, wqp, wuk, wuv, cos, nsin, psin)


FLASH_TILE = 256


def _flash_kernel(q_ref, k_ref, v_ref, o_ref, s_scr, *, scale, causal, n_kv):
    qb = pl.program_id(2)
    q = q_ref[...]
    row = qb * FLASH_TILE + lax.broadcasted_iota(I32, (FLASH_TILE, FLASH_TILE), 0)
    lane = lax.broadcasted_iota(I32, (FLASH_TILE, FLASH_TILE), 1)
    for c in range(n_kv):
        cols = slice(c * FLASH_TILE, (c + 1) * FLASH_TILE)
        s = _dot_nt(q, k_ref[cols, :]) * scale
        if causal:
            s = jnp.where(c * FLASH_TILE + lane <= row, s, NEG_BIG)
        s_scr[:, cols] = s
    s = s_scr[...]
    p = jnp.exp(s - jnp.max(s, axis=1, keepdims=True))
    o = _dot(p.astype(BF16), v_ref[...]) / jnp.sum(p, axis=1, keepdims=True)
    o_ref[...] = o.astype(o_ref.dtype)


def _flash(q, k, v, *, dk, kv_len, scale, causal, name):
    nq = SEQ // FLASH_TILE
    return pl.pallas_call(
        functools.partial(_flash_kernel, scale=scale, causal=causal, n_kv=kv_len // FLASH_TILE),
        out_shape=jax.ShapeDtypeStruct((TOKENS, N_HEADS * HEAD_DIM), BF16),
        grid=(BATCH, N_HEADS, nq),
        in_specs=[pl.BlockSpec((FLASH_TILE, dk), lambda b, h, i: (b * nq + i, h)),
                  pl.BlockSpec((kv_len, dk), lambda b, h, i: (b, h)),
                  pl.BlockSpec((kv_len, HEAD_DIM), lambda b, h, i: (b, h))],
        out_specs=pl.BlockSpec((FLASH_TILE, HEAD_DIM), lambda b, h, i: (b * nq + i, h)),
        scratch_shapes=[pltpu.VMEM((FLASH_TILE, kv_len), F32)],
        compiler_params=_cparams(3, VMEM_LIMIT_BYTES),
        name=name,
    )(q, k, v)


FFN_TM = 1024
FFN_SUB = 256
FFN_SUB_SHIFT = 8
DMA_UNROLL_SHIFT = 5
DMA_UNROLL = 1 << DMA_UNROLL_SHIFT


def _swiglu_step(load_x, nv, j, wg_ref, wu_ref, wd_ref, o_ref, wg_b, wu_b, wd_b):
    wg_b[...] = wg_ref[0].astype(BF16)
    wu_b[...] = wu_ref[0].astype(BF16)
    wd_b[...] = wd_ref[0].astype(BF16)

    @pl.when(j == 0)
    def _():
        o_ref[...] = jnp.zeros(o_ref.shape, F32)

    def accumulate(rows):
        x = load_x(rows)
        g = _dot(x, wg_b[...])
        u = _dot(x, wu_b[...])
        o_ref[rows, :] += _dot((g * jax.nn.sigmoid(g) * u).astype(BF16), wd_b[...])

    n_sub = jnp.right_shift(nv + (FFN_SUB - 1), FFN_SUB_SHIFT)
    for n in range(1, FFN_TM // FFN_SUB + 1):
        @pl.when(n_sub == n)
        def _():
            for s in range(n):
                accumulate(slice(s * FFN_SUB, (s + 1) * FFN_SUB))


def _ffn_kernel(te_ref, nv_ref, x_ref, wg_ref, wu_ref, wd_ref, o_ref, wg_b, wu_b, wd_b):
    del te_ref
    nv = nv_ref[pl.program_id(0)]
    _swiglu_step(lambda rows: x_ref[rows, :], nv, pl.program_id(1), wg_ref, wu_ref, wd_ref, o_ref, wg_b, wu_b, wd_b)


def _moe_ffn_kernel(te_ref, nv_ref, tok_ref, h_hbm, wg_ref, wu_ref, wd_ref, o_ref, x_buf, wg_b, wu_b, wd_b, sem):
    del te_ref
    i = pl.program_id(0)
    j = pl.program_id(1)
    nv = nv_ref[i]

    @pl.when((j == 0) & (nv > 0))
    def _():
        n_rows = jnp.left_shift(jnp.right_shift(nv + (FFN_SUB - 1), FFN_SUB_SHIFT), FFN_SUB_SHIFT)

        def row_copy(tok, r):
            return pltpu.make_async_copy(h_hbm.at[pl.ds(tok, 1)], x_buf.at[pl.ds(r, 1)], sem)

        def start(g, carry):
            for u in range(DMA_UNROLL):
                r = g * DMA_UNROLL + u
                row_copy(tok_ref[i * FFN_TM + r], r).start()
            return carry

        lax.fori_loop(0, jnp.right_shift(n_rows, DMA_UNROLL_SHIFT), start, 0)

        def wait(g, carry):
            for _ in range(DMA_UNROLL):
                row_copy(0, 0).wait()
            return carry

        lax.fori_loop(0, jnp.right_shift(n_rows, DMA_UNROLL_SHIFT), wait, 0)

    _swiglu_step(lambda rows: x_buf[rows, :].astype(BF16), nv, j, wg_ref, wu_ref, wd_ref, o_ref, wg_b, wu_b, wd_b)


def _ffn_specs(tf, n_prefetch):
    nff = D_FF // tf

    def jj(i, j, nv):
        return jnp.where(nv[i] > 0, j, nff - 1)

    w_in = pl.BlockSpec((1, D_MODEL, tf), lambda i, j, te, nv, *_: (te[i], 0, jj(i, j, nv)))
    w_out = pl.BlockSpec((1, tf, D_MODEL), lambda i, j, te, nv, *_: (te[i], jj(i, j, nv), 0))
    out = pl.BlockSpec((FFN_TM, D_MODEL), lambda i, j, *_: (i, 0))
    scratch = [pltpu.VMEM((D_MODEL, tf), BF16), pltpu.VMEM((D_MODEL, tf), BF16), pltpu.VMEM((tf, D_MODEL), BF16)]
    return nff, w_in, w_out, out, scratch


def _grouped_swiglu(tile_expert, tile_valid, x, wg, wu, wd, *, tf, name):
    r = x.shape[0]
    nff, w_in, w_out, out, scratch = _ffn_specs(tf, 2)
    grid_spec = pltpu.PrefetchScalarGridSpec(
        num_scalar_prefetch=2,
        grid=(r // FFN_TM, nff),
        in_specs=[pl.BlockSpec((FFN_TM, D_MODEL), lambda i, j, *_: (i, 0)), w_in, w_in, w_out],
        out_specs=out,
        scratch_shapes=scratch,
    )
    return pl.pallas_call(
        _ffn_kernel,
        out_shape=jax.ShapeDtypeStruct((r, D_MODEL), F32),
        grid_spec=grid_spec,
        compiler_params=_cparams(2, VMEM_LIMIT_BYTES),
        name=name,
    )(tile_expert, tile_valid, x, wg, wu, wd)


def _gathered_swiglu(tile_expert, tile_valid, row_token, h, wg, wu, wd, *, tf, name):
    r = row_token.shape[0]
    nff, w_in, w_out, out, scratch = _ffn_specs(tf, 3)
    grid_spec = pltpu.PrefetchScalarGridSpec(
        num_scalar_prefetch=3,
        grid=(r // FFN_TM, nff),
        in_specs=[pl.BlockSpec(memory_space=pl.ANY), w_in, w_in, w_out],
        out_specs=out,
        scratch_shapes=[pltpu.VMEM((FFN_TM, D_MODEL), F32)] + scratch + [pltpu.SemaphoreType.DMA(())],
    )
    return pl.pallas_call(
        _moe_ffn_kernel,
        out_shape=jax.ShapeDtypeStruct((r, D_MODEL), F32),
        grid_spec=grid_spec,
        compiler_params=_cparams(2, VMEM_LIMIT_BYTES),
        name=name,
    )(tile_expert, tile_valid, row_token, h, wg, wu, wd)


ROUTER_TM = 256
META_E0, META_E1, META_W0, META_W1, META_R0, META_R1 = range(6)


def _router_kernel(h_ref, r_ref, lt_ref, meta_ref, cnt_ref, run_scr):
    i = pl.program_id(0)

    @pl.when(i == 0)
    def _():
        run_scr[...] = jnp.zeros(run_scr.shape, F32)

    def split(x):
        hi = x.astype(BF16)
        return hi, (x - hi.astype(F32)).astype(BF16)

    h_hi, h_lo = split(h_ref[...])
    r_hi, r_lo = split(r_ref[...])
    logits = _dot(h_hi, r_hi) + (_dot(h_hi, r_lo) + _dot(h_lo, r_hi))
    lane = lax.broadcasted_iota(I32, (ROUTER_TM, LANES), 1).astype(F32)
    logits = jnp.where(lane < N_EXPERTS, logits, -jnp.inf)

    def top1(x):
        m = jnp.max(x, axis=1, keepdims=True)
        idx = jnp.min(jnp.where(x == m, lane, float(LANES)), axis=1, keepdims=True)
        return m, idx

    m0, e0 = top1(logits)
    m1, e1 = top1(jnp.where(lane == e0, -jnp.inf, logits))
    ex = jnp.exp(m1 - m0)
    w0 = 1.0 / (1.0 + ex)
    w1 = ex / (1.0 + ex)

    hot0 = jnp.where(lane == e0, 1.0, 0.0)
    hot1 = jnp.where(lane == e1, 1.0, 0.0)
    before0 = _dot(lt_ref[...], hot0.astype(BF16)) + run_scr[0:1, :]
    tot0 = jnp.sum(hot0, axis=0, keepdims=True)
    before1 = _dot(lt_ref[...], hot1.astype(BF16)) + run_scr[0:1, :] + tot0
    r0 = jnp.sum(hot0 * before0, axis=1, keepdims=True)
    r1 = jnp.sum(hot1 * before1, axis=1, keepdims=True)
    run_new = run_scr[0:1, :] + tot0 + jnp.sum(hot1, axis=0, keepdims=True)
    run_scr[...] = jnp.broadcast_to(run_new, run_scr.shape)
    cnt_ref[...] = jnp.broadcast_to(run_new, cnt_ref.shape)

    meta = jnp.zeros((ROUTER_TM, LANES), F32)
    for ln, val in ((META_E0, e0), (META_E1, e1), (META_W0, w0), (META_W1, w1),
                    (META_R0, r0), (META_R1, r1)):
        meta = jnp.where(lane == ln, val, meta)
    meta_ref[...] = meta


def _router(h, router_padded, lower_tri):
    return pl.pallas_call(
        _router_kernel,
        out_shape=(jax.ShapeDtypeStruct((TOKENS, LANES), F32), jax.ShapeDtypeStruct((8, LANES), F32)),
        grid=(TOKENS // ROUTER_TM,),
        in_specs=[pl.BlockSpec((ROUTER_TM, D_MODEL), lambda i: (i, 0)),
                  pl.BlockSpec((D_MODEL, LANES), lambda i: (0, 0)),
                  pl.BlockSpec((ROUTER_TM, ROUTER_TM), lambda i: (0, 0))],
        out_specs=(pl.BlockSpec((ROUTER_TM, LANES), lambda i: (i, 0)), pl.BlockSpec((8, LANES), lambda i: (0, 0))),
        scratch_shapes=[pltpu.VMEM((8, LANES), F32)],
        compiler_params=_cparams(1, VMEM_LIMIT_BYTES),
        name="moe_router",
    )(h, router_padded, lower_tri)


COMBINE_TM = 256


def _combine_kernel(rows_ref, y_hbm, h_ref, meta_ref, g_ref, b_ref, o_ref, ob_ref, buf, sem):
    i = pl.program_id(0)

    def row_copy(src, slot, r):
        return pltpu.make_async_copy(y_hbm.at[pl.ds(src, 1)], buf.at[slot, pl.ds(r, 1)], sem)

    def start(g, carry):
        for u in range(DMA_UNROLL):
            r = g * DMA_UNROLL + u
            tok = i * COMBINE_TM + r
            row_copy(rows_ref[tok], 0, r).start()
            row_copy(rows_ref[TOKENS + tok], 1, r).start()
        return carry

    lax.fori_loop(0, COMBINE_TM // DMA_UNROLL, start, 0)

    def wait(g, carry):
        for _ in range(DMA_UNROLL):
            row_copy(0, 0, 0).wait()
            row_copy(0, 1, 0).wait()
        return carry

    lax.fori_loop(0, COMBINE_TM // DMA_UNROLL, wait, 0)
    meta = meta_ref[...]
    f = meta[:, META_W0:META_W0 + 1] * buf[0] + meta[:, META_W1:META_W1 + 1] * buf[1]
    out = _layer_norm_rows(DN_ALPHA * h_ref[...] + f, g_ref[...], b_ref[...])
    o_ref[...] = out
    ob_ref[...] = out.astype(BF16)


def _combine(rows, y, h, meta, g, b, layer):
    grid_spec = pltpu.PrefetchScalarGridSpec(
        num_scalar_prefetch=1,
        grid=(TOKENS // COMBINE_TM,),
        in_specs=[pl.BlockSpec(memory_space=pl.ANY),
                  pl.BlockSpec((COMBINE_TM, D_MODEL), lambda i, rows: (i, 0)),
                  pl.BlockSpec((COMBINE_TM, LANES), lambda i, rows: (i, 0)),
                  pl.BlockSpec((None, 1, D_MODEL), lambda i, rows: (layer, 0, 0)),
                  pl.BlockSpec((None, 1, D_MODEL), lambda i, rows: (layer, 0, 0))],
        out_specs=(pl.BlockSpec((COMBINE_TM, D_MODEL), lambda i, rows: (i, 0)),
                   pl.BlockSpec((COMBINE_TM, D_MODEL), lambda i, rows: (i, 0))),
        scratch_shapes=[pltpu.VMEM((2, COMBINE_TM, D_MODEL), F32), pltpu.SemaphoreType.DMA(())],
    )
    return pl.pallas_call(
        _combine_kernel,
        out_shape=(jax.ShapeDtypeStruct((TOKENS, D_MODEL), F32), jax.ShapeDtypeStruct((TOKENS, D_MODEL), BF16)),
        grid_spec=grid_spec,
        compiler_params=_cparams(1, VMEM_LIMIT_BYTES),
        name="moe_combine",
    )(rows, y, h, meta, g, b)


def _pack_w_in(w_in):
    src = {}
    off = 0
    for name, width in IN_SPLITS:
        src[name] = (off, off + width)
        off += width
    lead = w_in.shape[:-1]

    def cols(name):
        a, b = src[name]
        return w_in[..., a:b]

    def zeros(width):
        return jnp.zeros(lead + (width,), w_in.dtype)

    pieces = []
    for name, width in _PACKED:
        if name == 'idx_k_lo':
            pieces += [cols('idx_k'), zeros(64)]
        elif name == 'idx_k_hi':
            pieces += [zeros(64), cols('idx_k')]
        elif name == 'misc':
            pieces += [cols('idx_w'), cols('nsa_g'), zeros(128 - 28)]
        elif name == 'mla_kr':
            pieces += [cols('mla_kr'), zeros(64)]
        else:
            pieces.append(cols(name))
    return jnp.concatenate(pieces, axis=-1).astype(BF16)


def _t5_bucket_np(dist):
    exact = REL_BUCKETS // 2
    d = np.maximum(dist, 0)
    log_ratio = np.log(np.maximum(d, 1).astype(np.float32) / exact) / math.log(REL_MAX_DIST / exact)
    far = np.minimum(exact + (log_ratio * (REL_BUCKETS - exact)).astype(np.int32), REL_BUCKETS - 1)
    return np.where(d < exact, d, far).astype(np.int32)


def _bias_slabs(table4):
    i = np.arange(Q_TILE)[:, None]
    j = np.arange(K_CHUNK)[None, :]
    bucket = jnp.asarray(np.stack([_t5_bucket_np(i - j - rel) for rel in (-256, -128, 0, -2 * SEQ)]))
    tab = table4.astype(F32)
    out = jnp.zeros((4, N_HEADS, Q_TILE, K_CHUNK), F32)
    for b in range(REL_BUCKETS):
        out = jnp.where((bucket == b)[:, None], tab[b][None, :, None, None], out)
    return out


def _rope_tables():
    half = QK_ROPE // 2
    inv = ROPE_BASE ** (-jnp.arange(half, dtype=F32) / half)
    ang = jnp.arange(SEQ, dtype=F32)[:, None] * inv[None, :]
    cos, sin = jnp.cos(ang), jnp.sin(ang)
    zero = jnp.zeros_like(cos)
    pad = jnp.zeros((SEQ, LANES - QK_ROPE), F32)
    cos_t = jnp.concatenate([cos, cos, pad], axis=1)
    nsin_t = jnp.concatenate([-sin, zero, pad], axis=1)
    psin_t = jnp.concatenate([zero, sin, pad], axis=1)
    return cos_t, nsin_t, psin_t


def _cover_expanded():
    n = np.arange(LANES)[:, None]
    s = np.arange(SEQ)[None, :]
    j = s // SLC_LEN
    cover = (CMP_STRIDE * n < SLC_LEN * j + SLC_LEN) & (CMP_STRIDE * n + CMP_LEN > SLC_LEN * j) & (n < N_CMP)
    return jnp.asarray(cover, BF16)


def _strict_upper(n):
    return jnp.asarray(np.arange(n)[:, None] > np.arange(n)[None, :], BF16)


def _strict_lower(n):
    return jnp.asarray(np.arange(n)[None, :] < np.arange(n)[:, None], BF16)


def _moe_layout(meta, counts):
    n_tiles = 2 * TOKENS // FFN_TM + N_EXPERTS
    cnt = counts[0, :N_EXPERTS].astype(I32)
    padded = ((cnt + FFN_TM - 1) // FFN_TM) * FFN_TM
    ends = jnp.cumsum(padded)
    offs = ends - padded
    e = meta[:, META_E0:META_E1 + 1].astype(I32)
    rank = meta[:, META_R0:META_R1 + 1].astype(I32)
    rows = (offs[e] + rank).T.reshape(-1)
    token = jnp.tile(jnp.arange(TOKENS, dtype=I32), 2)
    row_token = jnp.zeros((n_tiles * FFN_TM,), I32).at[rows].set(token)
    tile_start = jnp.arange(n_tiles, dtype=I32) * FFN_TM
    te = jnp.minimum(jnp.sum(tile_start[:, None] >= ends[None, :], axis=1), N_EXPERTS - 1).astype(I32)
    nv = jnp.clip(cnt[te] - (tile_start - offs[te]), 0, FFN_TM).astype(I32)
    return rows, row_token, te, nv


def kernel(x, mem, rel_table, w_in, mla_q_norm, mla_kv_norm, mla_w_uq, mla_w_uk, mla_w_uv, nsa_pe_k, nsa_pe_v,
           nsa_ck_w1, nsa_ck_w2, nsa_cv_w1, nsa_cv_w2, w_branch, w_out, ln1_g, ln1_b, xa_wq, xa_wk, xa_wv, xa_wo,
           ln2_g, ln2_b, ffn_w_gate, ffn_w_up, ffn_w_down, moe_router, moe_w_gate, moe_w_up, moe_w_down,
           ln3_g, ln3_b):
    h = x.reshape(TOKENS, D_MODEL)
    hb = h.astype(BF16)
    mem_b = mem.reshape(BATCH * MEM_LEN, D_MODEL).astype(BF16)

    w_in_p = _pack_w_in(w_in)
    dsa_slabs = _bias_slabs(rel_table[:, :N_HEADS])
    nsa_slabs = _bias_slabs(rel_table[:, N_HEADS:])
    cos_t, nsin_t, psin_t = _rope_tables()
    cover = _cover_expanded()
    upper = _strict_upper(SB_TILE)
    incl_upper = jnp.asarray(np.arange(K_CHUNK)[:, None] <= np.arange(K_CHUNK)[None, :], BF16)
    lower = _strict_lower(ROUTER_TM)

    uq = mla_w_uq.reshape(DEPTH, Q_LORA, N_HEADS, QK_NOPE + QK_ROPE)
    w_qn = uq[..., :QK_NOPE].reshape(DEPTH, Q_LORA, N_HEADS * QK_NOPE).astype(BF16)
    w_qp = jnp.concatenate([uq[..., QK_NOPE:], jnp.zeros((DEPTH, Q_LORA, N_HEADS, LANES - QK_ROPE), F32)],
                           axis=-1).reshape(DEPTH, Q_LORA, N_HEADS * LANES).astype(BF16)
    w_uk = mla_w_uk.astype(BF16)
    w_uv = mla_w_uv.astype(BF16)
    w_br = w_branch.astype(BF16)
    w_o = w_out.astype(BF16)
    xq, xk, xv, xo = (w.astype(BF16) for w in (xa_wq, xa_wk, xa_wv, xa_wo))
    router_p = jnp.pad(moe_router, ((0, 0), (0, 0), (0, LANES - N_EXPERTS)))
    row2 = lambda a: a.reshape(DEPTH, 1, -1)
    g1, b1, g2, b2, g3, b3 = (row2(a) for a in (ln1_g, ln1_b, ln2_g, ln2_b, ln3_g, ln3_b))
    qn, kvn = row2(mla_q_norm), row2(mla_kv_norm)
    pek = nsa_pe_k.reshape(DEPTH, 1, CMP_LEN * HEAD_DIM)
    pev = nsa_pe_v.reshape(DEPTH, 1, CMP_LEN * HEAD_DIM)
    dense_nv = jnp.full((TOKENS // FFN_TM,), FFN_TM, I32)
    moe_wg = moe_w_gate.reshape((-1,) + moe_w_gate.shape[2:])
    moe_wu = moe_w_up.reshape((-1,) + moe_w_up.shape[2:])
    moe_wd = moe_w_down.reshape((-1,) + moe_w_down.shape[2:])

    for layer in range(DEPTH):
        z = _matmul(hb, w_in_p, layer, tm=1024, tn=1024, out_dtype=BF16, name="in_proj")
        o_dsa = _dsa(z, dsa_slabs, incl_upper)
        o_sb = _stick_breaking(z, upper)
        group = lambda name: z[:, _OFF[name]:_OFF[name] + HEAD_DIM].reshape(BATCH * LANES, CMP_STRIDE * HEAD_DIM)
        k_cmp, v_cmp = _nsa_compress(group('nsa_kc'), group('nsa_vc'), pek[layer], pev[layer],
                                     nsa_ck_w1[layer], nsa_ck_w2[layer], nsa_cv_w1[layer], nsa_cv_w2[layer])
        o_nsa = _nsa(z, k_cmp, v_cmp, cover, nsa_slabs, incl_upper)
        q_cat, k_cat, v_mla = _mla_prep(z, qn[layer], kvn[layer], w_qn[layer], w_qp[layer], w_uk[layer],
                                        w_uv[layer], cos_t, nsin_t, psin_t, tm=512)
        o_mla = _flash(q_cat, k_cat, v_mla, dk=MLA_QK, kv_len=SEQ, scale=(QK_NOPE + QK_ROPE) ** -0.5,
                       causal=True, name="mla_attention")
        mixed = _merge((o_dsa, o_sb, o_nsa, o_mla), z, w_br, layer, tm=512, tn=512)
        h, hb = _matmul_res_ln(mixed, w_o, h, g1, b1, layer, tm=256, name="out_proj_ln")

        q_x = _matmul(hb, xq, layer, tm=1024, tn=512, out_dtype=BF16, name="xa_q_proj")
        k_x = _matmul(mem_b, xk, layer, tm=1024, tn=512, out_dtype=BF16, name="xa_k_proj")
        v_x = _matmul(mem_b, xv, layer, tm=1024, tn=512, out_dtype=BF16, name="xa_v_proj")
        o_x = _flash(q_x, k_x, v_x, dk=HEAD_DIM, kv_len=MEM_LEN, scale=HEAD_DIM ** -0.5, causal=False,
                     name="cross_attention")
        h, hb = _matmul_res_ln(o_x, xo, h, g2, b2, layer, tm=256, name="xa_out_proj_ln")

        i = layer // 2
        if layer % 2 == 0:
            dense_te = jnp.full((TOKENS // FFN_TM,), i, I32)
            y = _grouped_swiglu(dense_te, dense_nv, hb, ffn_w_gate, ffn_w_up, ffn_w_down, tf=256,
                                name="dense_swiglu")
            h, hb = _res_ln(y, h, g3, b3, layer, tm=256, name="ffn_res_ln")
        else:
            meta, counts = _router(h, router_p[i], lower)
            rows, row_token, te, nv = _moe_layout(meta, counts)
            y = _gathered_swiglu(te + i * N_EXPERTS, nv, row_token, h, moe_wg, moe_wu, moe_wd, tf=256,
                                 name="moe_swiglu")
            h, hb = _combine(rows, y, h, meta, g3, b3, layer)
    return h.reshape(BATCH, SEQ, D_MODEL)
```
